```python
import math
import jax, jax.numpy as jnp
from jax import lax
import numpy as np

D_MODEL = 1024
BATCH = 8
SEQ = 2048
DEPTH = 1

N_HEADS_ATTN = 8
HEAD_DIM = 64
ATTN_WIDTH = N_HEADS_ATTN * HEAD_DIM
KV_LATENT = 128
N_HEADS_IDX = 8
IDX_DIM = 64
TOPK_MAX = 256
Q_BLOCK = 128
CONV_WIDTH = 512
CONV_GROUPS = 8
CONV_K = 3
MIX_WIDTH = ATTN_WIDTH + CONV_WIDTH
D_FF = 2816
N_BUCKETS = 32
MAX_DISTANCE = 128
EPS = 1e-6
IN_SIZES = (ATTN_WIDTH, KV_LATENT, N_HEADS_IDX * IDX_DIM, IDX_DIM, N_HEADS_IDX, CONV_WIDTH, CONV_WIDTH, CONV_WIDTH)
IN_WIDTH = ATTN_WIDTH + KV_LATENT + N_HEADS_IDX * IDX_DIM + IDX_DIM + N_HEADS_IDX + 3 * CONV_WIDTH

kernel_name = "hybrid_dsa_shortconv_macaron"


def rmsnorm(x, g):
    xf = x.astype(jnp.float32)
    y = xf * lax.rsqrt(jnp.mean(xf * xf, axis=-1, keepdims=True) + EPS)
    return (y * g.astype(jnp.float32)).astype(x.dtype)


def swiglu(x, w_gate, w_up, w_down):
    return (jax.nn.silu(x @ w_gate) * (x @ w_up)) @ w_down


def rel_bucket(dist):
    max_exact = N_BUCKETS // 2
    d = jnp.maximum(dist, 1).astype(jnp.float32)
    large = max_exact + (jnp.log(d / max_exact) / math.log(MAX_DISTANCE / max_exact)
                         * (N_BUCKETS - max_exact)).astype(jnp.int32)
    large = jnp.minimum(large, N_BUCKETS - 1)
    return jnp.where(dist < max_exact, dist, large)


def dsa_attention(q, c_kv, q_idx, k_idx, w_idx, w_uk, w_uv, rel_bias, n_keep):
    B, S = q.shape[0], q.shape[1]
    nb = S // Q_BLOCK
    q_abs = jnp.einsum('bshd,chd->bshc', q, w_uk)
    key_pos = jnp.arange(S, dtype=jnp.int32)
    idx_scale = (N_HEADS_IDX * IDX_DIM) ** -0.5

    def to_blocks(a):
        return jnp.moveaxis(a.reshape((B, nb, Q_BLOCK) + a.shape[2:]), 1, 0)

    def one_block(args):
        qa, qi, wi, start = args
        qpos = start + jnp.arange(Q_BLOCK, dtype=jnp.int32)
        dots = jnp.einsum('bthd,bsd->bths', qi, k_idx).astype(jnp.float32)
        score = jnp.einsum('bth,bths->bts', wi.astype(jnp.float32) * idx_scale, jax.nn.relu(dots))
        causal = key_pos[None, :] <= qpos[:, None]
        score = jnp.where(causal[None], score, -jnp.inf)
        _, sel = lax.top_k(score, n_keep)
        valid = sel <= qpos[None, :, None]
        ckv_sel = jax.vmap(lambda c, i: c[i])(c_kv, sel)
        logits = jnp.einsum('bthc,btkc->bthk', qa, ckv_sel).astype(jnp.float32) * (HEAD_DIM ** -0.5)
        bias = rel_bias[rel_bucket(qpos[None, :, None] - sel)]
        logits = logits + jnp.moveaxis(bias, -1, 2).astype(jnp.float32)
        logits = jnp.where(valid[:, :, None, :], logits, -jnp.inf)
        p = jax.nn.softmax(logits, axis=-1).astype(c_kv.dtype)
        return jnp.einsum('bthk,btkc->bthc', p, ckv_sel)

    starts = jnp.arange(nb, dtype=jnp.int32) * Q_BLOCK
    ctx = lax.map(one_block, (to_blocks(q_abs), to_blocks(q_idx), to_blocks(w_idx), starts))
    ctx = jnp.moveaxis(ctx, 0, 1).reshape(B, S, N_HEADS_ATTN, KV_LATENT)
    out = jnp.einsum('bshc,chd->bshd', ctx, w_uv)
    return out.reshape(B, S, ATTN_WIDTH)


def short_conv(gate_b, gate_c, u, conv_w):
    v = gate_c * u
    S = v.shape[1]
    vp = jnp.pad(v, ((0, 0), (CONV_K - 1, 0), (0, 0)))
    y = conv_w[0] * vp[:, 0:S]
    for j in range(1, CONV_K):
        y = y + conv_w[j] * vp[:, j:j + S]
    return gate_b * y


def setup_inputs(seed: int = 0) -> dict:
    key = jax.random.key(seed)
    ks = jax.random.split(key, 24)
    f32 = jnp.float32

    def w(k, shape, fan_in):
        return jax.random.normal(k, shape, f32) * (fan_in ** -0.5)

    def gain(k, shape):
        return 1.0 + 0.01 * jax.random.normal(k, shape, f32)

    L = DEPTH
    return {
        "x": jax.random.normal(ks[0], (BATCH, SEQ, D_MODEL), f32),
        "ffn1_norm": gain(ks[1], (L, D_MODEL)),
        "ffn1_w_gate": w(ks[2], (L, D_MODEL, D_FF), D_MODEL),
        "ffn1_w_up": w(ks[3], (L, D_MODEL, D_FF), D_MODEL),
        "ffn1_w_down": w(ks[4], (L, D_FF, D_MODEL), D_FF),
        "mix_norm": gain(ks[5], (L, D_MODEL)),
        "w_in": w(ks[6], (L, D_MODEL, IN_WIDTH), D_MODEL),
        "kv_norm": gain(ks[7], (L, KV_LATENT)),
        "w_uk": w(ks[8], (L, KV_LATENT, N_HEADS_ATTN, HEAD_DIM), KV_LATENT),
        "w_uv": w(ks[9], (L, KV_LATENT, N_HEADS_ATTN, HEAD_DIM), KV_LATENT),
        "rel_bias": 0.5 * jax.random.normal(ks[10], (N_BUCKETS, N_HEADS_ATTN), f32),
        "conv_w": w(ks[11], (L, CONV_K, CONV_WIDTH), CONV_K),
        "attn_out_norm": gain(ks[12], (L, ATTN_WIDTH)),
        "conv_out_norm": gain(ks[13], (L, CONV_WIDTH)),
        "w_out": w(ks[14], (L, MIX_WIDTH, D_MODEL), MIX_WIDTH),
        "ffn2_norm": gain(ks[15], (L, D_MODEL)),
        "ffn2_w_gate": w(ks[16], (L, D_MODEL, D_FF), D_MODEL),
        "ffn2_w_up": w(ks[17], (L, D_MODEL, D_FF), D_MODEL),
        "ffn2_w_down": w(ks[18], (L, D_FF, D_MODEL), D_FF),
        "final_norm": gain(ks[19], (D_MODEL,)),
    }


def reference(x, ffn1_norm, ffn1_w_gate, ffn1_w_up, ffn1_w_down, mix_norm, w_in, kv_norm,
              w_uk, w_uv, rel_bias, conv_w, attn_out_norm, conv_out_norm, w_out,
              ffn2_norm, ffn2_w_gate, ffn2_w_up, ffn2_w_down, final_norm):
    B, S, _ = x.shape
    n_keep = min(TOPK_MAX, S // 4)
    split_at = np.cumsum(IN_SIZES)[:-1].tolist()
    h = x
    for l in range(DEPTH):
        h = h + 0.5 * swiglu(rmsnorm(h, ffn1_norm[l]), ffn1_w_gate[l], ffn1_w_up[l], ffn1_w_down[l])
        u = rmsnorm(h, mix_norm[l])
        proj = u @ w_in[l]
        q, ckv, qi, ki, wi, gb, gc, xin = jnp.split(proj, split_at, axis=-1)
        q = q.reshape(B, S, N_HEADS_ATTN, HEAD_DIM)
        ckv = rmsnorm(ckv, kv_norm[l])
        qi = qi.reshape(B, S, N_HEADS_IDX, IDX_DIM)
        attn_out = dsa_attention(q, ckv, qi, ki, wi, w_uk[l], w_uv[l], rel_bias, n_keep)
        conv_out = short_conv(gb, gc, xin, conv_w[l])
        mixed = jnp.concatenate([rmsnorm(attn_out, attn_out_norm[l]),
                                 rmsnorm(conv_out, conv_out_norm[l])], axis=-1)
        h = h + mixed @ w_out[l]
        h = h + 0.5 * swiglu(rmsnorm(h, ffn2_norm[l]), ffn2_w_gate[l], ffn2_w_up[l], ffn2_w_down[l])
    return rmsnorm(h, final_norm)
```

```python
import functools
import math

import jax
import jax.numpy as jnp
from jax import lax
from jax.experimental import pallas as pl
from jax.experimental.pallas import tpu as pltpu

F32 = jnp.float32
BF16 = jnp.bfloat16

EPS = 1e-6
N_HEADS = 8
HEAD_DIM = 64
ATTN_WIDTH = N_HEADS * HEAD_DIM
KV_LATENT = 128
N_HEADS_IDX = 8
IDX_DIM = 64
TOPK_MAX = 256
CONV_WIDTH = 512
CONV_K = 3
N_BUCKETS = 32
MAX_DISTANCE = 128
IN_SIZES = (ATTN_WIDTH, KV_LATENT, N_HEADS_IDX * IDX_DIM, IDX_DIM, N_HEADS_IDX,
            CONV_WIDTH, CONV_WIDTH, CONV_WIDTH)

SUBLANES = 8
LANES = 128

FFN_TOKENS = 512
FFN_CHUNK = 256
PROJ_TOKENS = 512
Q_BLOCK = 256
MAX_BISECT = 48
MAX_SNAP = 4096
VMEM_LIMIT = 56 * 1024 * 1024


def _rms(x, g):
    return x * lax.rsqrt(jnp.mean(x * x, axis=-1, keepdims=True) + EPS) * g


def _const_spec(shape):
    nd = len(shape)
    return pl.BlockSpec(shape, lambda *_: (0,) * nd, pipeline_mode=pl.Buffered(1))


def _swiglu_into(acc_ref, xn, wg_ref, wu_ref, wd_ref):
    d_ff = wg_ref.shape[1]
    for c in range(d_ff // FFN_CHUNK):
        sl = slice(c * FFN_CHUNK, (c + 1) * FFN_CHUNK)
        g = jnp.dot(xn, wg_ref[:, sl], preferred_element_type=F32)
        u = jnp.dot(xn, wu_ref[:, sl], preferred_element_type=F32)
        a = (g * jax.nn.sigmoid(g) * u).astype(BF16)
        d = jnp.dot(a, wd_ref[sl, :], preferred_element_type=F32)
        if c == 0:
            acc_ref[...] = d
        else:
            acc_ref[...] += d


def _ffn1_kernel(x_ref, g_ref, wg_ref, wu_ref, wd_ref, o_ref, acc_ref):
    x = x_ref[...]
    xn = _rms(x, g_ref[...]).astype(BF16)
    _swiglu_into(acc_ref, xn, wg_ref, wu_ref, wd_ref)
    o_ref[...] = x + 0.5 * acc_ref[...]


def _out_ffn2_kernel(h_ref, a_ref, c_ref, woa_ref, woc_ref, g_ref, wg_ref, wu_ref, wd_ref,
                     gf_ref, o_ref, acc_ref, *, final_norm):
    h = (h_ref[...]
         + jnp.dot(a_ref[...], woa_ref[...], preferred_element_type=F32)
         + jnp.dot(c_ref[...], woc_ref[...], preferred_element_type=F32))
    xn = _rms(h, g_ref[...]).astype(BF16)
    _swiglu_into(acc_ref, xn, wg_ref, wu_ref, wd_ref)
    h = h + 0.5 * acc_ref[...]
    o_ref[...] = _rms(h, gf_ref[...]) if final_norm else h


def _ffn1(x2, g, wg, wu, wd):
    n, d = x2.shape
    tile = pl.BlockSpec((FFN_TOKENS, d), lambda i: (i, 0))
    return pl.pallas_call(
        _ffn1_kernel,
        grid=(n // FFN_TOKENS,),
        in_specs=[tile, _const_spec(g.shape), _const_spec(wg.shape), _const_spec(wu.shape),
                  _const_spec(wd.shape)],
        out_specs=tile,
        out_shape=jax.ShapeDtypeStruct((n, d), F32),
        scratch_shapes=[pltpu.VMEM((FFN_TOKENS, d), F32)],
        compiler_params=pltpu.CompilerParams(dimension_semantics=("arbitrary",),
                                             vmem_limit_bytes=VMEM_LIMIT),
        name="ffn1",
    )(x2, g, wg, wu, wd)


def _out_ffn2(h2, a2, c2, woa, woc, g, wg, wu, wd, gf, final_norm):
    n, d = h2.shape
    tile = pl.BlockSpec((FFN_TOKENS, d), lambda i: (i, 0))
    mix = pl.BlockSpec((FFN_TOKENS, a2.shape[1]), lambda i: (i, 0))
    return pl.pallas_call(
        functools.partial(_out_ffn2_kernel, final_norm=final_norm),
        grid=(n // FFN_TOKENS,),
        in_specs=[tile, mix, mix, _const_spec(woa.shape), _const_spec(woc.shape),
                  _const_spec(g.shape), _const_spec(wg.shape), _const_spec(wu.shape),
                  _const_spec(wd.shape), _const_spec(gf.shape)],
        out_specs=tile,
        out_shape=jax.ShapeDtypeStruct((n, d), F32),
        scratch_shapes=[pltpu.VMEM((FFN_TOKENS, d), F32)],
        compiler_params=pltpu.CompilerParams(dimension_semantics=("arbitrary",),
                                             vmem_limit_bytes=VMEM_LIMIT),
        name="out_ffn2",
    )(h2, a2, c2, woa, woc, g, wg, wu, wd, gf)


_C_Q = 0
_C_KV = _C_Q + ATTN_WIDTH
_C_KI = _C_KV + KV_LATENT
_C_GB = _C_KI + LANES
_C_GC = _C_GB + CONV_WIDTH
_C_X = _C_GC + CONV_WIDTH
_C_END = _C_X + CONV_WIDTH
_T_ROWS = N_HEADS_IDX * IDX_DIM + 2 * SUBLANES


def _proj_kernel(h_ref, gm_ref, wcat_ref, wt_ref, wuk_ref, gkv_ref, cw_ref, gco_ref,
                 qabs_ref, ckv_ref, ki_ref, qit_ref, wit_ref, convn_ref, carry_ref):
    tp = h_ref.shape[0]

    @pl.when(pl.program_id(1) == 0)
    def _():
        carry_ref[...] = jnp.zeros_like(carry_ref)

    un = _rms(h_ref[...], gm_ref[...]).astype(BF16)
    p = jnp.dot(un, wcat_ref[...], preferred_element_type=F32)

    q = p[:, _C_Q:_C_KV].astype(BF16)
    qabs = jnp.dot(q, wuk_ref[...], preferred_element_type=F32) * (HEAD_DIM ** -0.5)
    qabs_ref[...] = qabs.astype(BF16)
    ckv_ref[...] = _rms(p[:, _C_KV:_C_KI], gkv_ref[...]).astype(BF16)
    ki_ref[...] = p[:, _C_KI:_C_KI + IDX_DIM].astype(BF16)

    v = p[:, _C_GC:_C_X] * p[:, _C_X:_C_END]
    row = lax.broadcasted_iota(jnp.int32, v.shape, 0)
    prev1 = jnp.broadcast_to(carry_ref[SUBLANES - 1:SUBLANES, :], v.shape)
    prev2 = jnp.broadcast_to(carry_ref[SUBLANES - 2:SUBLANES - 1, :], v.shape)
    v1 = jnp.where(row == 0, prev1, pltpu.roll(v, 1, 0))
    v2 = jnp.where(row == 0, prev2, jnp.where(row == 1, prev1, pltpu.roll(v, 2, 0)))
    carry_ref[...] = v[tp - SUBLANES:, :]
    y = cw_ref[0:1, :] * v2 + cw_ref[1:2, :] * v1 + cw_ref[2:3, :] * v
    conv = p[:, _C_GB:_C_GC] * y
    convn_ref[...] = _rms(conv, gco_ref[...]).astype(BF16)

    pt = lax.dot_general(wt_ref[...], un, (((1,), (1,)), ((), ())),
                         preferred_element_type=F32)
    nq = N_HEADS_IDX * IDX_DIM
    qit_ref[...] = pt[0:nq, :].astype(BF16)
    wit_ref[...] = pt[nq:nq + N_HEADS_IDX, :] * ((N_HEADS_IDX * IDX_DIM) ** -0.5)


def _proj(h1, gm, wcat, wt, wuk, gkv, cw, gco):
    b, s, d = h1.shape
    tp = PROJ_TOKENS
    nq = N_HEADS_IDX * IDX_DIM
    tok = lambda w: pl.BlockSpec((None, tp, w), lambda bi, ti: (bi, ti, 0))
    out_shape = (
        jax.ShapeDtypeStruct((b, s, N_HEADS * KV_LATENT), BF16),
        jax.ShapeDtypeStruct((b, s, KV_LATENT), BF16),
        jax.ShapeDtypeStruct((b, s, IDX_DIM), BF16),
        jax.ShapeDtypeStruct((b, nq, s), BF16),
        jax.ShapeDtypeStruct((b, N_HEADS_IDX, s), F32),
        jax.ShapeDtypeStruct((b, s, CONV_WIDTH), BF16),
    )
    out_specs = (
        tok(N_HEADS * KV_LATENT), tok(KV_LATENT), tok(IDX_DIM),
        pl.BlockSpec((None, nq, tp), lambda bi, ti: (bi, 0, ti)),
        pl.BlockSpec((None, N_HEADS_IDX, tp), lambda bi, ti: (bi, 0, ti)),
        tok(CONV_WIDTH),
    )
    return pl.pallas_call(
        _proj_kernel,
        grid=(b, s // tp),
        in_specs=[tok(d), _const_spec(gm.shape), _const_spec(wcat.shape), _const_spec(wt.shape),
                  _const_spec(wuk.shape), _const_spec(gkv.shape), _const_spec(cw.shape),
                  _const_spec(gco.shape)],
        out_specs=out_specs,
        out_shape=out_shape,
        scratch_shapes=[pltpu.VMEM((SUBLANES, CONV_WIDTH), F32)],
        compiler_params=pltpu.CompilerParams(dimension_semantics=("arbitrary", "arbitrary"),
                                             vmem_limit_bytes=VMEM_LIMIT),
        name="proj",
    )(h1, gm, wcat, wt, wuk, gkv, cw, gco)


def _rel_bucket(dist):
    max_exact = N_BUCKETS // 2
    d = jnp.maximum(dist, 1).astype(F32)
    large = max_exact + (jnp.log(d / max_exact) / math.log(MAX_DISTANCE / max_exact)
                         * (N_BUCKETS - max_exact)).astype(jnp.int32)
    large = jnp.minimum(large, N_BUCKETS - 1)
    return jnp.where(dist < max_exact, dist, large)


def _bias_kernel(rb_ref, o_ref):
    h = pl.program_id(0)
    tq = o_ref.shape[1]
    t = lax.broadcasted_iota(jnp.int32, (tq, tq), 0)
    s = lax.broadcasted_iota(jnp.int32, (tq, tq), 1)
    far = rb_ref[N_BUCKETS - 1, h]
    for k in range(2):
        bucket = _rel_bucket(jnp.maximum(t - s + k * tq, 0))
        val = jnp.zeros((tq, tq), F32)
        for b in range(N_BUCKETS):
            val = jnp.where(bucket == b, rb_ref[b, h], val)
        o_ref[k] = val - far


def _bias_tiles(rel_bias, tq):
    return pl.pallas_call(
        _bias_kernel,
        grid=(N_HEADS,),
        in_specs=[pl.BlockSpec(memory_space=pltpu.SMEM)],
        out_specs=pl.BlockSpec((None, 2, tq, tq), lambda h: (h, 0, 0, 0)),
        out_shape=jax.ShapeDtypeStruct((N_HEADS, 2, tq, tq), F32),
        compiler_params=pltpu.CompilerParams(dimension_semantics=("arbitrary",)),
        name="bias_tiles",
    )(rel_bias)


def _dsa_kernel(qit_ref, wit_ref, qabs_ref, ki_ref, ckv_ref, bias_ref, wuv_ref, gout_ref,
                o_ref, sc_ref, sel_ref, acc_ref, m_ref, l_ref, *, n_keep):
    tq = qabs_ref.shape[0]
    i = pl.program_id(1)
    nblk = i + 1
    q0 = i * tq
    kf = float(n_keep)
    key_in_blk = lax.broadcasted_iota(jnp.int32, (tq, tq), 0)
    qry_in_blk = lax.broadcasted_iota(jnp.int32, (tq, tq), 1)
    fold = lambda a: a.reshape(tq // SUBLANES, SUBLANES, tq).sum(axis=0)

    def score_body(j, carry):
        mn, mx = carry
        k0 = pl.multiple_of(j * tq, tq)
        kib = ki_ref[pl.ds(k0, tq), :]
        acc = jnp.zeros((tq, tq), F32)
        for h in range(N_HEADS_IDX):
            d = jnp.dot(kib, qit_ref[h * IDX_DIM:(h + 1) * IDX_DIM, :], preferred_element_type=F32)
            acc = acc + jnp.maximum(d, 0.0) * wit_ref[h:h + 1, :]
        causal = (k0 + key_in_blk) <= (q0 + qry_in_blk)
        masked = jnp.where(causal, acc, -jnp.inf)
        sc_ref[j] = masked
        mx = jnp.maximum(mx, jnp.max(masked, axis=0, keepdims=True))
        mn = jnp.minimum(mn, jnp.min(jnp.where(causal, acc, jnp.inf), axis=0, keepdims=True))
        return mn, mx

    mn, mx = lax.fori_loop(0, nblk, score_body,
                           (jnp.full((1, tq), jnp.inf, F32), jnp.full((1, tq), -jnp.inf, F32)))

    def count(pred):
        def body(j, a8):
            return a8 + fold(jnp.where(pred(sc_ref[j]), 1.0, 0.0))
        a8 = lax.fori_loop(0, nblk, body, jnp.zeros((SUBLANES, tq), F32))
        return a8.sum(axis=0, keepdims=True)

    def any_set(flag):
        return jnp.max(flag) > 0.0

    ncausal = (q0 + lax.broadcasted_iota(jnp.int32, (1, tq), 1) + 1).astype(F32)
    small = ncausal <= kf
    c_mx = count(lambda blk: blk >= mx)
    top_tied = c_mx >= kf
    lo = jnp.where(top_tied & ~small, mx, mn)
    clo = jnp.where(top_tied & ~small, c_mx, ncausal)
    hi = jnp.where(top_tied, jnp.inf, mx)
    chi = jnp.where(top_tied, 0.0, c_mx)
    active = jnp.where(~small & ~top_tied & (clo > kf), 1.0, 0.0)

    def bis_cond(st):
        return (st[0] < MAX_BISECT) & st[1]

    def bis_body(st):
        it, _, lo, hi, clo, chi, active = st
        mid = lo + (hi - lo) * 0.5
        live = (active > 0.0) & (mid > lo) & (mid < hi)
        c = count(lambda blk: blk >= mid)
        up = live & (c >= kf)
        dn = live & (c < kf)
        lo = jnp.where(up, mid, lo)
        clo = jnp.where(up, c, clo)
        hi = jnp.where(dn, mid, hi)
        chi = jnp.where(dn, c, chi)
        active = jnp.where(live & (clo > kf), 1.0, 0.0)
        return it + 1, any_set(active), lo, hi, clo, chi, active

    st = lax.while_loop(bis_cond, bis_body,
                        (jnp.int32(0), any_set(active), lo, hi, clo, chi, active))
    _, _, lo, hi, clo, chi, _ = st

    unres = jnp.where(~small & (clo > kf), 1.0, 0.0)

    def snap_cond(st):
        return (st[0] < MAX_SNAP) & st[1]

    def snap_body(st):
        it, _, lo, hi, clo, chi, unres = st

        def gmax_body(j, a8):
            blk = sc_ref[j]
            return jnp.maximum(a8, jnp.where((blk >= lo) & (blk < hi), blk, -jnp.inf)
                               .reshape(tq // SUBLANES, SUBLANES, tq).max(axis=0))
        gmax = lax.fori_loop(0, nblk, gmax_body,
                             jnp.full((SUBLANES, tq), -jnp.inf, F32)).max(axis=0, keepdims=True)
        c = count(lambda blk: blk >= gmax)
        on = unres > 0.0
        found = on & (c >= kf)
        down = on & (c < kf)
        lo = jnp.where(found, gmax, lo)
        clo = jnp.where(found, c, clo)
        hi = jnp.where(down, gmax, hi)
        chi = jnp.where(down, c, chi)
        unres = jnp.where(down, 1.0, 0.0)
        return it + 1, any_set(unres), lo, hi, clo, chi, unres

    st = lax.while_loop(snap_cond, snap_body,
                        (jnp.int32(0), any_set(unres), lo, hi, clo, chi, unres))
    _, _, lo, hi, clo, chi, _ = st

    tied = any_set(jnp.where(~small & (clo > kf), 1.0, 0.0))

    @pl.when(jnp.logical_not(tied))
    def _():
        def body(j, c):
            sel_ref[j] = jnp.where(sc_ref[j] >= lo, 1.0, 0.0).T
            return c
        lax.fori_loop(0, nblk, body, 0)

    @pl.when(tied)
    def _():
        need = kf - count(lambda blk: blk > lo)
        earlier = jnp.where(qry_in_blk < key_in_blk, 1.0, 0.0).astype(BF16)

        def body(j, seen):
            blk = sc_ref[j]
            eq = jnp.where(blk == lo, 1.0, 0.0)
            rank = seen + jnp.dot(earlier, eq.astype(BF16), preferred_element_type=F32)
            keep = (blk > lo) | ((blk == lo) & (rank < need))
            sel_ref[j] = jnp.where(keep, 1.0, 0.0).T
            return seen + fold(eq).sum(axis=0, keepdims=True)
        lax.fori_loop(0, nblk, body, jnp.zeros((1, tq), F32))

    m_ref[...] = jnp.full(m_ref.shape, -jnp.inf, F32)
    l_ref[...] = jnp.zeros(l_ref.shape, F32)
    acc_ref[...] = jnp.zeros(acc_ref.shape, F32)

    def attend(j, bias_idx):
        k0 = pl.multiple_of(j * tq, tq)
        kv = ckv_ref[pl.ds(k0, tq), :]
        sel = sel_ref[j] > 0.5
        for h in range(N_HEADS):
            hs = slice(h * KV_LATENT, (h + 1) * KV_LATENT)
            s = lax.dot_general(qabs_ref[:, hs], kv, (((1,), (1,)), ((), ())),
                                preferred_element_type=F32)
            if bias_idx is not None:
                s = s + bias_ref[h, bias_idx]
            s = jnp.where(sel, s, -jnp.inf)
            m_prev = m_ref[h]
            m_new = jnp.maximum(m_prev, jnp.max(s, axis=1, keepdims=True))
            m_safe = jnp.where(m_new == -jnp.inf, 0.0, m_new)
            alpha = jnp.exp(m_prev - m_safe)
            p = jnp.exp(s - m_safe[:, 0:1])
            l_ref[h] = alpha * l_ref[h] + jnp.sum(p, axis=1, keepdims=True)
            m_ref[h] = m_new
            pv = jnp.dot(p.astype(BF16), kv, preferred_element_type=F32)
            acc_ref[:, hs] = alpha * acc_ref[:, hs] + pv

    def far_body(j, c):
        attend(j, None)
        return c
    lax.fori_loop(0, jnp.maximum(i - 1, 0), far_body, 0)

    @pl.when(i >= 1)
    def _():
        attend(i - 1, 1)

    attend(i, 0)

    for h in range(N_HEADS):
        hs = slice(h * KV_LATENT, (h + 1) * KV_LATENT)
        acc_ref[:, hs] = acc_ref[:, hs] / l_ref[h]
    out = jnp.dot(acc_ref[...].astype(BF16), wuv_ref[...], preferred_element_type=F32)
    o_ref[...] = _rms(out, gout_ref[...]).astype(BF16)


def _dsa(qit, wit, qabs, ki, ckv, bias, wuv, gout, n_keep):
    b, s, _ = qabs.shape
    tq = Q_BLOCK
    nb = s // tq
    nq = N_HEADS_IDX * IDX_DIM
    return pl.pallas_call(
        functools.partial(_dsa_kernel, n_keep=n_keep),
        grid=(b, nb),
        in_specs=[
            pl.BlockSpec((None, nq, tq), lambda bi, i: (bi, 0, i)),
            pl.BlockSpec((None, N_HEADS_IDX, tq), lambda bi, i: (bi, 0, i)),
            pl.BlockSpec((None, tq, N_HEADS * KV_LATENT), lambda bi, i: (bi, i, 0)),
            pl.BlockSpec((None, s, IDX_DIM), lambda bi, i: (bi, 0, 0)),
            pl.BlockSpec((None, s, KV_LATENT), lambda bi, i: (bi, 0, 0)),
            _const_spec(bias.shape), _const_spec(wuv.shape), _const_spec(gout.shape),
        ],
        out_specs=pl.BlockSpec((None, tq, ATTN_WIDTH), lambda bi, i: (bi, i, 0)),
        out_shape=jax.ShapeDtypeStruct((b, s, ATTN_WIDTH), BF16),
        scratch_shapes=[
            pltpu.VMEM((nb, tq, tq), F32),
            pltpu.VMEM((nb, tq, tq), F32),
            pltpu.VMEM((tq, N_HEADS * KV_LATENT), F32),
            pltpu.VMEM((N_HEADS, tq, LANES), F32),
            pltpu.VMEM((N_HEADS, tq, LANES), F32),
        ],
        compiler_params=pltpu.CompilerParams(dimension_semantics=("arbitrary", "arbitrary"),
                                             vmem_limit_bytes=VMEM_LIMIT),
        name="dsa",
    )(qit, wit, qabs, ki, ckv, bias, wuv, gout)


def _block_diag(blocks):
    h, r, c = blocks.shape
    out = jnp.zeros((h, r, h, c), blocks.dtype)
    idx = jnp.arange(h)
    out = out.at[idx, :, idx, :].set(blocks)
    return out.reshape(h * r, h * c)


def kernel(x, ffn1_norm, ffn1_w_gate, ffn1_w_up, ffn1_w_down, mix_norm, w_in, kv_norm, w_uk, w_uv, rel_bias, conv_w, attn_out_norm, conv_out_norm, w_out, ffn2_norm, ffn2_w_gate, ffn2_w_up, ffn2_w_down, final_norm):
    b, s, d = x.shape
    depth = ffn1_norm.shape[0]
    n_keep = min(TOPK_MAX, s // 4)
    assert s % Q_BLOCK == 0 and s % PROJ_TOKENS == 0 and (b * s) % FFN_TOKENS == 0
    assert Q_BLOCK >= MAX_DISTANCE and n_keep <= Q_BLOCK
    row = lambda v: v.reshape(1, -1).astype(F32)
    offs = [0]
    for w in IN_SIZES:
        offs.append(offs[-1] + w)

    bias = _bias_tiles(rel_bias.astype(F32), Q_BLOCK)
    h = x.astype(F32).reshape(b * s, d)
    for l in range(depth):
        h = _ffn1(h, row(ffn1_norm[l]), ffn1_w_gate[l].astype(BF16), ffn1_w_up[l].astype(BF16),
                  ffn1_w_down[l].astype(BF16))

        wi = w_in[l]
        col = lambda k: wi[:, offs[k]:offs[k + 1]]
        wcat = jnp.concatenate(
            [col(0), col(1), col(3), jnp.zeros((d, LANES - IDX_DIM), wi.dtype), col(5), col(6), col(7)],
            axis=1).astype(BF16)
        wt = jnp.concatenate(
            [col(2).T, col(4).T, jnp.zeros((_T_ROWS - offs[3] + offs[2] - N_HEADS_IDX, d), wi.dtype)],
            axis=0).astype(BF16)
        wuk = _block_diag(jnp.transpose(w_uk[l], (1, 2, 0))).astype(BF16)
        wuv = _block_diag(jnp.transpose(w_uv[l], (1, 0, 2))).astype(BF16)

        qabs, ckv, ki, qit, wit, convn = _proj(
            h.reshape(b, s, d), row(mix_norm[l]), wcat, wt, wuk, row(kv_norm[l]),
            conv_w[l].astype(F32), row(conv_out_norm[l]))
        attn = _dsa(qit, wit, qabs, ki, ckv, bias, wuv, row(attn_out_norm[l]), n_keep)

        last = l == depth - 1
        h = _out_ffn2(h, attn.reshape(b * s, ATTN_WIDTH), convn.reshape(b * s, CONV_WIDTH),
                      w_out[l][:ATTN_WIDTH].astype(BF16), w_out[l][ATTN_WIDTH:].astype(BF16),
                      row(ffn2_norm[l]), ffn2_w_gate[l].astype(BF16), ffn2_w_up[l].astype(BF16),
                      ffn2_w_down[l].astype(BF16), row(final_norm), final_norm=last)
    return h.reshape(b, s, d).astype(x.dtype)
```

```python
import functools
import math

import jax
import jax.numpy as jnp
from jax import lax
from jax.experimental import pallas as pl
from jax.experimental.pallas import tpu as pltpu

F32 = jnp.float32
BF16 = jnp.bfloat16

EPS = 1e-6
N_HEADS = 8
HEAD_DIM = 64
ATTN_WIDTH = N_HEADS * HEAD_DIM
KV_LATENT = 128
N_HEADS_IDX = 8
IDX_DIM = 64
TOPK_MAX = 256
CONV_WIDTH = 512
CONV_K = 3
N_BUCKETS = 32
MAX_DISTANCE = 128
IN_SIZES = (ATTN_WIDTH, KV_LATENT, N_HEADS_IDX * IDX_DIM, IDX_DIM, N_HEADS_IDX,
            CONV_WIDTH, CONV_WIDTH, CONV_WIDTH)

SUBLANES = 8
LANES = 128

FFN_TOKENS = 512
FFN_CHUNK = 256
PROJ_TOKENS = 512
Q_BLOCK = 256
MAX_BISECT = 64
BRACKET_STOP = 2
MAX_PEEL = 4096
LOG2E = math.log2(math.e)
VMEM_LIMIT = 56 * 1024 * 1024


def _rms(x, g):
    return x * lax.rsqrt(jnp.mean(x * x, axis=-1, keepdims=True) + EPS) * g


def _const_spec(shape):
    nd = len(shape)
    return pl.BlockSpec(shape, lambda *_: (0,) * nd, pipeline_mode=pl.Buffered(1))


def _swiglu_into(acc_ref, xn, wg_ref, wu_ref, wd_ref):
    d_ff = wg_ref.shape[1]
    for c in range(d_ff // FFN_CHUNK):
        sl = slice(c * FFN_CHUNK, (c + 1) * FFN_CHUNK)
        g = jnp.dot(xn, wg_ref[:, sl], preferred_element_type=F32)
        u = jnp.dot(xn, wu_ref[:, sl], preferred_element_type=F32)
        a = (g * jax.nn.sigmoid(g) * u).astype(BF16)
        d = jnp.dot(a, wd_ref[sl, :], preferred_element_type=F32)
        if c == 0:
            acc_ref[...] = d
        else:
            acc_ref[...] += d


def _ffn1_kernel(x_ref, g_ref, wg_ref, wu_ref, wd_ref, o_ref, acc_ref):
    x = x_ref[...]
    xn = _rms(x, g_ref[...]).astype(BF16)
    _swiglu_into(acc_ref, xn, wg_ref, wu_ref, wd_ref)
    o_ref[...] = x + 0.5 * acc_ref[...]


def _out_ffn2_kernel(h_ref, a_ref, c_ref, woa_ref, woc_ref, g_ref, wg_ref, wu_ref, wd_ref,
                     gf_ref, o_ref, acc_ref, *, final_norm):
    h = (h_ref[...]
         + jnp.dot(a_ref[...], woa_ref[...], preferred_element_type=F32)
         + jnp.dot(c_ref[...], woc_ref[...], preferred_element_type=F32))
    xn = _rms(h, g_ref[...]).astype(BF16)
    _swiglu_into(acc_ref, xn, wg_ref, wu_ref, wd_ref)
    h = h + 0.5 * acc_ref[...]
    o_ref[...] = _rms(h, gf_ref[...]) if final_norm else h


def _ffn1(x2, g, wg, wu, wd):
    n, d = x2.shape
    tile = pl.BlockSpec((FFN_TOKENS, d), lambda i: (i, 0))
    return pl.pallas_call(
        _ffn1_kernel,
        grid=(n // FFN_TOKENS,),
        in_specs=[tile, _const_spec(g.shape), _const_spec(wg.shape), _const_spec(wu.shape),
                  _const_spec(wd.shape)],
        out_specs=tile,
        out_shape=jax.ShapeDtypeStruct((n, d), F32),
        scratch_shapes=[pltpu.VMEM((FFN_TOKENS, d), F32)],
        compiler_params=pltpu.CompilerParams(dimension_semantics=("arbitrary",),
                                             vmem_limit_bytes=VMEM_LIMIT),
        name="ffn1",
    )(x2, g, wg, wu, wd)


def _out_ffn2(h2, a2, c2, woa, woc, g, wg, wu, wd, gf, final_norm):
    n, d = h2.shape
    tile = pl.BlockSpec((FFN_TOKENS, d), lambda i: (i, 0))
    mix = pl.BlockSpec((FFN_TOKENS, a2.shape[1]), lambda i: (i, 0))
    return pl.pallas_call(
        functools.partial(_out_ffn2_kernel, final_norm=final_norm),
        grid=(n // FFN_TOKENS,),
        in_specs=[tile, mix, mix, _const_spec(woa.shape), _const_spec(woc.shape),
                  _const_spec(g.shape), _const_spec(wg.shape), _const_spec(wu.shape),
                  _const_spec(wd.shape), _const_spec(gf.shape)],
        out_specs=tile,
        out_shape=jax.ShapeDtypeStruct((n, d), F32),
        scratch_shapes=[pltpu.VMEM((FFN_TOKENS, d), F32)],
        compiler_params=pltpu.CompilerParams(dimension_semantics=("arbitrary",),
                                             vmem_limit_bytes=VMEM_LIMIT),
        name="out_ffn2",
    )(h2, a2, c2, woa, woc, g, wg, wu, wd, gf)


_C_Q = 0
_C_KV = _C_Q + ATTN_WIDTH
_C_KI = _C_KV + KV_LATENT
_C_GB = _C_KI + LANES
_C_GC = _C_GB + CONV_WIDTH
_C_X = _C_GC + CONV_WIDTH
_C_END = _C_X + CONV_WIDTH
_T_ROWS = N_HEADS_IDX * IDX_DIM + 2 * SUBLANES


def _proj_kernel(h_ref, gm_ref, wcat_ref, wt_ref, wuk_ref, gkv_ref, cw_ref, gco_ref,
                 qabs_ref, ckv_ref, ki_ref, qit_ref, wit_ref, convn_ref, carry_ref):
    tp = h_ref.shape[0]

    @pl.when(pl.program_id(1) == 0)
    def _():
        carry_ref[...] = jnp.zeros_like(carry_ref)

    un = _rms(h_ref[...], gm_ref[...]).astype(BF16)
    p = jnp.dot(un, wcat_ref[...], preferred_element_type=F32)

    q = p[:, _C_Q:_C_KV].astype(BF16)
    qabs = jnp.dot(q, wuk_ref[...], preferred_element_type=F32) * (HEAD_DIM ** -0.5 * LOG2E)
    for h in range(N_HEADS):
        qabs_ref[h] = qabs[:, h * KV_LATENT:(h + 1) * KV_LATENT].astype(BF16)
    ckv_ref[:, :KV_LATENT] = _rms(p[:, _C_KV:_C_KI], gkv_ref[...]).astype(BF16)
    ckv_ref[:, KV_LATENT:] = jnp.ones((tp, KV_LATENT), BF16)
    ki_ref[...] = p[:, _C_KI:_C_KI + IDX_DIM].astype(BF16)

    v = p[:, _C_GC:_C_X] * p[:, _C_X:_C_END]
    row = lax.broadcasted_iota(jnp.int32, v.shape, 0)
    prev1 = jnp.broadcast_to(carry_ref[SUBLANES - 1:SUBLANES, :], v.shape)
    prev2 = jnp.broadcast_to(carry_ref[SUBLANES - 2:SUBLANES - 1, :], v.shape)
    v1 = jnp.where(row == 0, prev1, pltpu.roll(v, 1, 0))
    v2 = jnp.where(row == 0, prev2, jnp.where(row == 1, prev1, pltpu.roll(v, 2, 0)))
    carry_ref[...] = v[tp - SUBLANES:, :]
    y = cw_ref[0:1, :] * v2 + cw_ref[1:2, :] * v1 + cw_ref[2:3, :] * v
    conv = p[:, _C_GB:_C_GC] * y
    convn_ref[...] = _rms(conv, gco_ref[...]).astype(BF16)

    pt = lax.dot_general(wt_ref[...], un, (((1,), (1,)), ((), ())),
                         preferred_element_type=F32)
    nq = N_HEADS_IDX * IDX_DIM
    qit_ref[...] = pt[0:nq, :].astype(BF16)
    wit_ref[...] = pt[nq:nq + N_HEADS_IDX, :] * ((N_HEADS_IDX * IDX_DIM) ** -0.5)


def _proj(h1, gm, wcat, wt, wuk, gkv, cw, gco):
    b, s, d = h1.shape
    tp = PROJ_TOKENS
    nq = N_HEADS_IDX * IDX_DIM
    tok = lambda w: pl.BlockSpec((None, tp, w), lambda bi, ti: (bi, ti, 0))
    out_shape = (
        jax.ShapeDtypeStruct((b, N_HEADS, s, KV_LATENT), BF16),
        jax.ShapeDtypeStruct((b, s, 2 * KV_LATENT), BF16),
        jax.ShapeDtypeStruct((b, s, IDX_DIM), BF16),
        jax.ShapeDtypeStruct((b, nq, s), BF16),
        jax.ShapeDtypeStruct((b, N_HEADS_IDX, s), F32),
        jax.ShapeDtypeStruct((b, s, CONV_WIDTH), BF16),
    )
    out_specs = (
        pl.BlockSpec((None, N_HEADS, tp, KV_LATENT), lambda bi, ti: (bi, 0, ti, 0)),
        tok(2 * KV_LATENT), tok(IDX_DIM),
        pl.BlockSpec((None, nq, tp), lambda bi, ti: (bi, 0, ti)),
        pl.BlockSpec((None, N_HEADS_IDX, tp), lambda bi, ti: (bi, 0, ti)),
        tok(CONV_WIDTH),
    )
    return pl.pallas_call(
        _proj_kernel,
        grid=(b, s // tp),
        in_specs=[tok(d), _const_spec(gm.shape), _const_spec(wcat.shape), _const_spec(wt.shape),
                  _const_spec(wuk.shape), _const_spec(gkv.shape), _const_spec(cw.shape),
                  _const_spec(gco.shape)],
        out_specs=out_specs,
        out_shape=out_shape,
        scratch_shapes=[pltpu.VMEM((SUBLANES, CONV_WIDTH), F32)],
        compiler_params=pltpu.CompilerParams(dimension_semantics=("arbitrary", "arbitrary"),
                                             vmem_limit_bytes=VMEM_LIMIT),
        name="proj",
    )(h1, gm, wcat, wt, wuk, gkv, cw, gco)


def _rel_bucket(dist):
    max_exact = N_BUCKETS // 2
    d = jnp.maximum(dist, 1).astype(F32)
    large = max_exact + (jnp.log(d / max_exact) / math.log(MAX_DISTANCE / max_exact)
                         * (N_BUCKETS - max_exact)).astype(jnp.int32)
    large = jnp.minimum(large, N_BUCKETS - 1)
    return jnp.where(dist < max_exact, dist, large)


def _bias_kernel(rb_ref, o_ref):
    h = pl.program_id(0)
    tq = o_ref.shape[1]
    t = lax.broadcasted_iota(jnp.int32, (tq, tq), 0)
    s = lax.broadcasted_iota(jnp.int32, (tq, tq), 1)
    far = rb_ref[N_BUCKETS - 1, h]
    for k in range(2):
        bucket = _rel_bucket(jnp.maximum(t - s + k * tq, 0))
        val = jnp.zeros((tq, tq), F32)
        for b in range(N_BUCKETS):
            val = jnp.where(bucket == b, rb_ref[b, h], val)
        o_ref[k] = (val - far) * LOG2E


def _bias_tiles(rel_bias, tq):
    return pl.pallas_call(
        _bias_kernel,
        grid=(N_HEADS,),
        in_specs=[pl.BlockSpec(memory_space=pltpu.SMEM)],
        out_specs=pl.BlockSpec((None, 2, tq, tq), lambda h: (h, 0, 0, 0)),
        out_shape=jax.ShapeDtypeStruct((N_HEADS, 2, tq, tq), F32),
        compiler_params=pltpu.CompilerParams(dimension_semantics=("arbitrary",)),
        name="bias_tiles",
    )(rel_bias)


def _dsa_kernel(qit_ref, wit_ref, qabs_ref, ki_ref, ckv_ref, bias_ref, wuv_ref, gout_ref,
                o_ref, sc_ref, sel_ref, acc_ref, m_ref, *, n_keep):
    tq = o_ref.shape[0]
    i = pl.program_id(1)
    nblk = i + 1
    q0 = i * tq
    kf = float(n_keep)
    key_in_blk = lax.broadcasted_iota(jnp.int32, (tq, tq), 0)
    qry_in_blk = lax.broadcasted_iota(jnp.int32, (tq, tq), 1)
    ind = lambda mask: jnp.where(mask, 1.0, 0.0)
    fold = lambda a: a.reshape(tq // SUBLANES, SUBLANES, tq).sum(axis=0)
    fold_max = lambda a: a.reshape(tq // SUBLANES, SUBLANES, tq).max(axis=0)
    zeros8 = jnp.zeros((SUBLANES, tq), F32)
    ninf8 = jnp.full((SUBLANES, tq), -jnp.inf, F32)

    def score_body(j, carry):
        mn, mx = carry
        k0 = pl.multiple_of(j * tq, tq)
        kib = ki_ref[pl.ds(k0, tq), :]
        acc = jnp.zeros((tq, tq), F32)
        for h in range(N_HEADS_IDX):
            d = jnp.dot(kib, qit_ref[h * IDX_DIM:(h + 1) * IDX_DIM, :], preferred_element_type=F32)
            acc = acc + jnp.maximum(d, 0.0) * wit_ref[h:h + 1, :]
        causal = (k0 + key_in_blk) <= (q0 + qry_in_blk)
        masked = jnp.where(causal, acc, -jnp.inf)
        sc_ref[j] = masked
        mx = jnp.maximum(mx, jnp.max(masked, axis=0, keepdims=True))
        mn = jnp.minimum(mn, jnp.min(jnp.where(causal, acc, jnp.inf), axis=0, keepdims=True))
        return mn, mx

    mn, mx = lax.fori_loop(0, nblk, score_body,
                           (jnp.full((1, tq), jnp.inf, F32), jnp.full((1, tq), -jnp.inf, F32)))

    def count(pred):
        a8 = lax.fori_loop(0, nblk, lambda j, a: a + fold(ind(pred(sc_ref[j]))), zeros8)
        return a8.sum(axis=0, keepdims=True)

    def any_set(flag):
        return jnp.max(flag) > 0.0

    def init_body(j, st):
        a, b, c = st
        blk = sc_ref[j]
        return a + fold(ind(blk >= 0.0)), b + fold(ind(blk > 0.0)), c + fold(ind(blk >= mx))
    c_ge0, c_gt0, c_mx = [a.sum(axis=0, keepdims=True)
                          for a in lax.fori_loop(0, nblk, init_body, (zeros8, zeros8, zeros8))]
    ncausal = (q0 + lax.broadcasted_iota(jnp.int32, (1, tq), 1) + 1).astype(F32)
    small = ncausal <= kf
    top_tied = c_mx >= kf
    neg = c_ge0 < kf
    zero_tie = (c_gt0 < kf) & ~neg
    lo = jnp.where(small | neg, mn, 0.0)
    lo = jnp.where(top_tied & ~small, mx, lo)
    clo = jnp.where(small | neg, ncausal, c_ge0)
    clo = jnp.where(top_tied & ~small, c_mx, clo)
    hi = jnp.where(neg, 0.0, mx)
    chi = jnp.where(neg, c_ge0, c_mx)
    searching = ~small & ~top_tied & ~zero_tie

    def bis_cond(st):
        return (st[0] < MAX_BISECT) & st[1]

    def bis_body(st):
        it, _, lo, hi, clo, chi, active = st
        mid = lo + (hi - lo) * 0.5
        live = (active > 0.0) & (mid > lo) & (mid < hi)
        c = count(lambda blk: blk >= mid)
        up = live & (c >= kf)
        dn = live & (c < kf)
        lo = jnp.where(up, mid, lo)
        clo = jnp.where(up, c, clo)
        hi = jnp.where(dn, mid, hi)
        chi = jnp.where(dn, c, chi)
        active = ind(live & (clo > kf) & (clo - chi > BRACKET_STOP))
        return it + 1, any_set(active), lo, hi, clo, chi, active

    active = ind(searching & (clo > kf) & (clo - chi > BRACKET_STOP))
    st = lax.while_loop(bis_cond, bis_body,
                        (jnp.int32(0), any_set(active), lo, hi, clo, chi, active))
    _, _, lo, hi, clo, chi, _ = st

    unres = ind(searching & (clo > kf))

    def peel_cond(st):
        return (st[0] < MAX_PEEL) & st[1]

    def peel_body(st):
        it, _, lo, hi, clo, chi, cand, unres = st

        def body(j, s2):
            a8, m8 = s2
            blk = sc_ref[j]
            ge = blk >= cand
            below = jnp.where((blk >= lo) & ~ge, blk, -jnp.inf)
            return a8 + fold(ind(ge)), jnp.maximum(m8, fold_max(below))
        a8, m8 = lax.fori_loop(0, nblk, body, (zeros8, ninf8))
        c = a8.sum(axis=0, keepdims=True)
        nxt = m8.max(axis=0, keepdims=True)
        on = unres > 0.0
        found = on & (c >= kf)
        down = on & (c < kf)
        lo = jnp.where(found, cand, lo)
        clo = jnp.where(found, c, clo)
        hi = jnp.where(down, cand, hi)
        chi = jnp.where(down, c, chi)
        cand = jnp.where(down, nxt, cand)
        return it + 1, any_set(ind(down)), lo, hi, clo, chi, cand, ind(down)

    st = lax.while_loop(peel_cond, peel_body,
                        (jnp.int32(0), any_set(unres), lo, hi, clo, chi, hi, unres))
    _, _, lo, hi, clo, chi, _, _ = st

    tied = any_set(ind(~small & (clo > kf)))

    @pl.when(jnp.logical_not(tied))
    def _():
        def body(j, c):
            sel_ref[j] = ind(sc_ref[j] >= lo).T
            return c
        lax.fori_loop(0, nblk, body, 0)

    @pl.when(tied)
    def _():
        need = kf - count(lambda blk: blk > lo)
        earlier = ind(qry_in_blk < key_in_blk).astype(BF16)

        def body(j, seen):
            blk = sc_ref[j]
            eq = ind(blk == lo)
            rank = seen + jnp.dot(earlier, eq.astype(BF16), preferred_element_type=F32)
            keep = (blk > lo) | ((blk == lo) & (rank < need))
            sel_ref[j] = ind(keep).T
            return seen + fold(eq).sum(axis=0, keepdims=True)
        lax.fori_loop(0, nblk, body, jnp.zeros((1, tq), F32))

    nrow = N_HEADS * tq
    m_ref[...] = jnp.full(m_ref.shape, -jnp.inf, F32)
    acc_ref[...] = jnp.zeros(acc_ref.shape, F32)
    wide = lambda a: jnp.concatenate([a, a], axis=1)

    def attend(j, bias_idx):
        k0 = pl.multiple_of(j * tq, tq)
        kv1 = ckv_ref[pl.ds(k0, tq), :]
        q = qabs_ref[...].reshape(nrow, KV_LATENT)
        s = lax.dot_general(q, kv1[:, :KV_LATENT], (((1,), (1,)), ((), ())),
                            preferred_element_type=F32)
        s = s.reshape(N_HEADS, tq, tq)
        if bias_idx is not None:
            s = s + bias_ref[:, bias_idx]
        s = jnp.where((sel_ref[j] > 0.5)[None], s, -jnp.inf).reshape(nrow, tq)
        m_prev = m_ref[...]
        m_new = jnp.maximum(m_prev, jnp.max(s, axis=1, keepdims=True))
        m_safe = jnp.where(m_new == -jnp.inf, 0.0, m_new)
        alpha = jnp.exp2(m_prev - m_safe)
        p = jnp.exp2(s - wide(m_safe))
        m_ref[...] = m_new
        pv = jnp.dot(p.astype(BF16), kv1, preferred_element_type=F32)
        acc_ref[...] = wide(alpha) * acc_ref[...] + pv

    def far_body(j, c):
        attend(j, None)
        return c
    lax.fori_loop(0, jnp.maximum(i - 1, 0), far_body, 0)

    @pl.when(i >= 1)
    def _():
        attend(i - 1, 1)

    attend(i, 0)

    ctx = []
    for h in range(N_HEADS):
        a = acc_ref[h * tq:(h + 1) * tq, :]
        ctx.append(a[:, :KV_LATENT] / a[:, KV_LATENT:])
    ctx = jnp.concatenate(ctx, axis=1).astype(BF16)
    out = jnp.dot(ctx, wuv_ref[...], preferred_element_type=F32)
    o_ref[...] = _rms(out, gout_ref[...]).astype(BF16)


def _dsa(qit, wit, qabs, ki, ckv, bias, wuv, gout, n_keep):
    b, _, s, _ = qabs.shape
    tq = Q_BLOCK
    nb = s // tq
    nq = N_HEADS_IDX * IDX_DIM
    return pl.pallas_call(
        functools.partial(_dsa_kernel, n_keep=n_keep),
        grid=(b, nb),
        in_specs=[
            pl.BlockSpec((None, nq, tq), lambda bi, i: (bi, 0, i)),
            pl.BlockSpec((None, N_HEADS_IDX, tq), lambda bi, i: (bi, 0, i)),
            pl.BlockSpec((None, N_HEADS, tq, KV_LATENT), lambda bi, i: (bi, 0, i, 0)),
            pl.BlockSpec((None, s, IDX_DIM), lambda bi, i: (bi, 0, 0)),
            pl.BlockSpec((None, s, 2 * KV_LATENT), lambda bi, i: (bi, 0, 0)),
            _const_spec(bias.shape), _const_spec(wuv.shape), _const_spec(gout.shape),
        ],
        out_specs=pl.BlockSpec((None, tq, ATTN_WIDTH), lambda bi, i: (bi, i, 0)),
        out_shape=jax.ShapeDtypeStruct((b, s, ATTN_WIDTH), BF16),
        scratch_shapes=[
            pltpu.VMEM((nb, tq, tq), F32),
            pltpu.VMEM((nb, tq, tq), F32),
            pltpu.VMEM((N_HEADS * tq, 2 * KV_LATENT), F32),
            pltpu.VMEM((N_HEADS * tq, LANES), F32),
        ],
        compiler_params=pltpu.CompilerParams(dimension_semantics=("arbitrary", "arbitrary"),
                                             vmem_limit_bytes=VMEM_LIMIT),
        name="dsa",
    )(qit, wit, qabs, ki, ckv, bias, wuv, gout)


def _block_diag(blocks):
    h, r, c = blocks.shape
    out = jnp.zeros((h, r, h, c), blocks.dtype)
    idx = jnp.arange(h)
    out = out.at[idx, :, idx, :].set(blocks)
    return out.reshape(h * r, h * c)


def kernel(x, ffn1_norm, ffn1_w_gate, ffn1_w_up, ffn1_w_down, mix_norm, w_in, kv_norm, w_uk, w_uv, rel_bias, conv_w, attn_out_norm, conv_out_norm, w_out, ffn2_norm, ffn2_w_gate, ffn2_w_up, ffn2_w_down, final_norm):
    b, s, d = x.shape
    depth = ffn1_norm.shape[0]
    n_keep = min(TOPK_MAX, s // 4)
    assert s % Q_BLOCK == 0 and s % PROJ_TOKENS == 0 and (b * s) % FFN_TOKENS == 0
    assert Q_BLOCK >= MAX_DISTANCE and n_keep <= Q_BLOCK
    row = lambda v: v.reshape(1, -1).astype(F32)
    offs = [0]
    for w in IN_SIZES:
        offs.append(offs[-1] + w)

    bias = _bias_tiles(rel_bias.astype(F32), Q_BLOCK)
    h = x.astype(F32).reshape(b * s, d)
    for l in range(depth):
        h = _ffn1(h, row(ffn1_norm[l]), ffn1_w_gate[l].astype(BF16), ffn1_w_up[l].astype(BF16),
                  ffn1_w_down[l].astype(BF16))

        wi = w_in[l]
        col = lambda k: wi[:, offs[k]:offs[k + 1]]
        wcat = jnp.concatenate(
            [col(0), col(1), col(3), jnp.zeros((d, LANES - IDX_DIM), wi.dtype), col(5), col(6), col(7)],
            axis=1).astype(BF16)
        wt = jnp.concatenate(
            [col(2).T, col(4).T, jnp.zeros((_T_ROWS - offs[3] + offs[2] - N_HEADS_IDX, d), wi.dtype)],
            axis=0).astype(BF16)
        wuk = _block_diag(jnp.transpose(w_uk[l], (1, 2, 0))).astype(BF16)
        wuv = _block_diag(jnp.transpose(w_uv[l], (1, 0, 2))).astype(BF16)

        qabs, ckv, ki, qit, wit, convn = _proj(
            h.reshape(b, s, d), row(mix_norm[l]), wcat, wt, wuk, row(kv_norm[l]),
            conv_w[l].astype(F32), row(conv_out_norm[l]))
        attn = _dsa(qit, wit, qabs, ki, ckv, bias, wuv, row(attn_out_norm[l]), n_keep)

        last = l == depth - 1
        h = _out_ffn2(h, attn.reshape(b * s, ATTN_WIDTH), convn.reshape(b * s, CONV_WIDTH),
                      w_out[l][:ATTN_WIDTH].astype(BF16), w_out[l][ATTN_WIDTH:].astype(BF16),
                      row(ffn2_norm[l]), ffn2_w_gate[l].astype(BF16), ffn2_w_up[l].astype(BF16),
                      ffn2_w_down[l].astype(BF16), row(final_norm), final_norm=last)
    return h.reshape(b, s, d).astype(x.dtype)
```

```python
import functools
import math

import jax
import jax.numpy as jnp
from jax import lax
from jax.experimental import pallas as pl
from jax.experimental.pallas import tpu as pltpu

F32 = jnp.float32
BF16 = jnp.bfloat16

EPS = 1e-6
N_HEADS = 8
HEAD_DIM = 64
ATTN_WIDTH = N_HEADS * HEAD_DIM
KV_LATENT = 128
N_HEADS_IDX = 8
IDX_DIM = 64
TOPK_MAX = 256
CONV_WIDTH = 512
CONV_K = 3
N_BUCKETS = 32
MAX_DISTANCE = 128
IN_SIZES = (ATTN_WIDTH, KV_LATENT, N_HEADS_IDX * IDX_DIM, IDX_DIM, N_HEADS_IDX,
            CONV_WIDTH, CONV_WIDTH, CONV_WIDTH)

SUBLANES = 8
LANES = 128

FFN_TOKENS = 512
FFN_CHUNK = 256
PROJ_TOKENS = 512
Q_BLOCK = 256
MAX_BISECT = 64
ATT_GROUPS = 8
SWEEP_ACCS = 4
BRACKET_STOP = 0
MAX_PEEL = 4096
LOG2E = math.log2(math.e)
VMEM_LIMIT = 56 * 1024 * 1024


def _rms(x, g):
    return x * lax.rsqrt(jnp.mean(x * x, axis=-1, keepdims=True) + EPS) * g


def _const_spec(shape):
    nd = len(shape)
    return pl.BlockSpec(shape, lambda *_: (0,) * nd, pipeline_mode=pl.Buffered(1))


def _swiglu_into(acc_ref, xn, wg_ref, wu_ref, wd_ref):
    d_ff = wg_ref.shape[1]
    for c in range(d_ff // FFN_CHUNK):
        sl = slice(c * FFN_CHUNK, (c + 1) * FFN_CHUNK)
        g = jnp.dot(xn, wg_ref[:, sl], preferred_element_type=F32)
        u = jnp.dot(xn, wu_ref[:, sl], preferred_element_type=F32)
        a = (g * jax.nn.sigmoid(g) * u).astype(BF16)
        d = jnp.dot(a, wd_ref[sl, :], preferred_element_type=F32)
        if c == 0:
            acc_ref[...] = d
        else:
            acc_ref[...] += d


def _ffn1_kernel(x_ref, g_ref, wg_ref, wu_ref, wd_ref, o_ref, acc_ref):
    x = x_ref[...]
    xn = _rms(x, g_ref[...]).astype(BF16)
    _swiglu_into(acc_ref, xn, wg_ref, wu_ref, wd_ref)
    o_ref[...] = x + 0.5 * acc_ref[...]


def _out_ffn2_kernel(h_ref, a_ref, c_ref, woa_ref, woc_ref, g_ref, wg_ref, wu_ref, wd_ref,
                     gf_ref, o_ref, acc_ref, *, final_norm):
    h = (h_ref[...]
         + jnp.dot(a_ref[...], woa_ref[...], preferred_element_type=F32)
         + jnp.dot(c_ref[...], woc_ref[...], preferred_element_type=F32))
    xn = _rms(h, g_ref[...]).astype(BF16)
    _swiglu_into(acc_ref, xn, wg_ref, wu_ref, wd_ref)
    h = h + 0.5 * acc_ref[...]
    o_ref[...] = _rms(h, gf_ref[...]) if final_norm else h


def _ffn1(x2, g, wg, wu, wd):
    n, d = x2.shape
    tile = pl.BlockSpec((FFN_TOKENS, d), lambda i: (i, 0))
    return pl.pallas_call(
        _ffn1_kernel,
        grid=(n // FFN_TOKENS,),
        in_specs=[tile, _const_spec(g.shape), _const_spec(wg.shape), _const_spec(wu.shape),
                  _const_spec(wd.shape)],
        out_specs=tile,
        out_shape=jax.ShapeDtypeStruct((n, d), F32),
        scratch_shapes=[pltpu.VMEM((FFN_TOKENS, d), F32)],
        compiler_params=pltpu.CompilerParams(dimension_semantics=("arbitrary",),
                                             vmem_limit_bytes=VMEM_LIMIT),
        name="ffn1",
    )(x2, g, wg, wu, wd)


def _out_ffn2(h2, a2, c2, woa, woc, g, wg, wu, wd, gf, final_norm):
    n, d = h2.shape
    tile = pl.BlockSpec((FFN_TOKENS, d), lambda i: (i, 0))
    mix = pl.BlockSpec((FFN_TOKENS, a2.shape[1]), lambda i: (i, 0))
    return pl.pallas_call(
        functools.partial(_out_ffn2_kernel, final_norm=final_norm),
        grid=(n // FFN_TOKENS,),
        in_specs=[tile, mix, mix, _const_spec(woa.shape), _const_spec(woc.shape),
                  _const_spec(g.shape), _const_spec(wg.shape), _const_spec(wu.shape),
                  _const_spec(wd.shape), _const_spec(gf.shape)],
        out_specs=tile,
        out_shape=jax.ShapeDtypeStruct((n, d), F32),
        scratch_shapes=[pltpu.VMEM((FFN_TOKENS, d), F32)],
        compiler_params=pltpu.CompilerParams(dimension_semantics=("arbitrary",),
                                             vmem_limit_bytes=VMEM_LIMIT),
        name="out_ffn2",
    )(h2, a2, c2, woa, woc, g, wg, wu, wd, gf)


_C_Q = 0
_C_KV = _C_Q + ATTN_WIDTH
_C_KI = _C_KV + KV_LATENT
_C_GB = _C_KI + LANES
_C_GC = _C_GB + CONV_WIDTH
_C_X = _C_GC + CONV_WIDTH
_C_END = _C_X + CONV_WIDTH
_T_ROWS = N_HEADS_IDX * IDX_DIM + 2 * SUBLANES


def _proj_kernel(h_ref, gm_ref, wcat_ref, wt_ref, wuk_ref, gkv_ref, cw_ref, gco_ref,
                 qabs_ref, ckv_ref, ki_ref, qit_ref, wit_ref, convn_ref, carry_ref):
    tp = h_ref.shape[0]

    @pl.when(pl.program_id(1) == 0)
    def _():
        carry_ref[...] = jnp.zeros_like(carry_ref)

    un = _rms(h_ref[...], gm_ref[...]).astype(BF16)
    p = jnp.dot(un, wcat_ref[...], preferred_element_type=F32)

    q = p[:, _C_Q:_C_KV].astype(BF16)
    qabs = jnp.dot(q, wuk_ref[...], preferred_element_type=F32) * (HEAD_DIM ** -0.5 * LOG2E)
    for h in range(N_HEADS):
        qabs_ref[h] = qabs[:, h * KV_LATENT:(h + 1) * KV_LATENT].astype(BF16)
    ckv_ref[:, :KV_LATENT] = _rms(p[:, _C_KV:_C_KI], gkv_ref[...]).astype(BF16)
    ckv_ref[:, KV_LATENT:] = jnp.ones((tp, KV_LATENT), BF16)
    ki_ref[...] = p[:, _C_KI:_C_KI + IDX_DIM].astype(BF16)

    v = p[:, _C_GC:_C_X] * p[:, _C_X:_C_END]
    row = lax.broadcasted_iota(jnp.int32, v.shape, 0)
    prev1 = jnp.broadcast_to(carry_ref[SUBLANES - 1:SUBLANES, :], v.shape)
    prev2 = jnp.broadcast_to(carry_ref[SUBLANES - 2:SUBLANES - 1, :], v.shape)
    v1 = jnp.where(row == 0, prev1, pltpu.roll(v, 1, 0))
    v2 = jnp.where(row == 0, prev2, jnp.where(row == 1, prev1, pltpu.roll(v, 2, 0)))
    carry_ref[...] = v[tp - SUBLANES:, :]
    y = cw_ref[0:1, :] * v2 + cw_ref[1:2, :] * v1 + cw_ref[2:3, :] * v
    conv = p[:, _C_GB:_C_GC] * y
    convn_ref[...] = _rms(conv, gco_ref[...]).astype(BF16)

    pt = lax.dot_general(wt_ref[...], un, (((1,), (1,)), ((), ())),
                         preferred_element_type=F32)
    nq = N_HEADS_IDX * IDX_DIM
    qit_ref[...] = pt[0:nq, :].astype(BF16)
    wit_ref[...] = pt[nq:nq + N_HEADS_IDX, :] * ((N_HEADS_IDX * IDX_DIM) ** -0.5)


def _proj(h1, gm, wcat, wt, wuk, gkv, cw, gco):
    b, s, d = h1.shape
    tp = PROJ_TOKENS
    nq = N_HEADS_IDX * IDX_DIM
    tok = lambda w: pl.BlockSpec((None, tp, w), lambda bi, ti: (bi, ti, 0))
    out_shape = (
        jax.ShapeDtypeStruct((b, N_HEADS, s, KV_LATENT), BF16),
        jax.ShapeDtypeStruct((b, s, 2 * KV_LATENT), BF16),
        jax.ShapeDtypeStruct((b, s, IDX_DIM), BF16),
        jax.ShapeDtypeStruct((b, nq, s), BF16),
        jax.ShapeDtypeStruct((b, N_HEADS_IDX, s), F32),
        jax.ShapeDtypeStruct((b, s, CONV_WIDTH), BF16),
    )
    out_specs = (
        pl.BlockSpec((None, N_HEADS, tp, KV_LATENT), lambda bi, ti: (bi, 0, ti, 0)),
        tok(2 * KV_LATENT), tok(IDX_DIM),
        pl.BlockSpec((None, nq, tp), lambda bi, ti: (bi, 0, ti)),
        pl.BlockSpec((None, N_HEADS_IDX, tp), lambda bi, ti: (bi, 0, ti)),
        tok(CONV_WIDTH),
    )
    return pl.pallas_call(
        _proj_kernel,
        grid=(b, s // tp),
        in_specs=[tok(d), _const_spec(gm.shape), _const_spec(wcat.shape), _const_spec(wt.shape),
                  _const_spec(wuk.shape), _const_spec(gkv.shape), _const_spec(cw.shape),
                  _const_spec(gco.shape)],
        out_specs=out_specs,
        out_shape=out_shape,
        scratch_shapes=[pltpu.VMEM((SUBLANES, CONV_WIDTH), F32)],
        compiler_params=pltpu.CompilerParams(dimension_semantics=("arbitrary", "arbitrary"),
                                             vmem_limit_bytes=VMEM_LIMIT),
        name="proj",
    )(h1, gm, wcat, wt, wuk, gkv, cw, gco)


def _rel_bucket(dist):
    max_exact = N_BUCKETS // 2
    d = jnp.maximum(dist, 1).astype(F32)
    large = max_exact + (jnp.log(d / max_exact) / math.log(MAX_DISTANCE / max_exact)
                         * (N_BUCKETS - max_exact)).astype(jnp.int32)
    large = jnp.minimum(large, N_BUCKETS - 1)
    return jnp.where(dist < max_exact, dist, large)


def _bias_kernel(rb_ref, o_ref):
    h = pl.program_id(0)
    tq, tk = o_ref.shape
    t = lax.broadcasted_iota(jnp.int32, (tq, tk), 0)
    s = lax.broadcasted_iota(jnp.int32, (tq, tk), 1)
    bucket = _rel_bucket(jnp.maximum(t - s + tq, 0))
    val = jnp.zeros((tq, tk), F32)
    for b in range(N_BUCKETS):
        val = jnp.where(bucket == b, rb_ref[b, h], val)
    o_ref[...] = (val - rb_ref[N_BUCKETS - 1, h]) * LOG2E


def _bias_tiles(rel_bias, tq):
    return pl.pallas_call(
        _bias_kernel,
        grid=(N_HEADS,),
        in_specs=[pl.BlockSpec(memory_space=pltpu.SMEM)],
        out_specs=pl.BlockSpec((None, tq, 2 * tq), lambda h: (h, 0, 0)),
        out_shape=jax.ShapeDtypeStruct((N_HEADS, tq, 2 * tq), F32),
        compiler_params=pltpu.CompilerParams(dimension_semantics=("arbitrary",)),
        name="bias_tiles",
    )(rel_bias)


def _dsa_kernel(qit_ref, wit_ref, qabs_ref, ki_ref, ckv_ref, bias_ref, wuv_ref, gout_ref,
                o_ref, sc_ref, sel_ref, acc_ref, m_ref, *, n_keep):
    tq = o_ref.shape[0]
    i = pl.program_id(1)
    nblk = i + 1
    q0 = i * tq
    kf = float(n_keep)
    key_in_blk = lax.broadcasted_iota(jnp.int32, (tq, tq), 0)
    qry_in_blk = lax.broadcasted_iota(jnp.int32, (tq, tq), 1)
    ind = lambda mask: jnp.where(mask, 1.0, 0.0)
    part = lambda a: a.reshape(tq // (SWEEP_ACCS * SUBLANES), SWEEP_ACCS, SUBLANES, tq)
    fold = lambda a: part(a).sum(axis=0)
    fold_max = lambda a: part(a).max(axis=0)
    total = lambda a: a.sum(axis=0).sum(axis=0, keepdims=True)
    total_max = lambda a: a.max(axis=0).max(axis=0, keepdims=True)
    zeros8 = jnp.zeros((SWEEP_ACCS, SUBLANES, tq), F32)
    ninf8 = jnp.full((SWEEP_ACCS, SUBLANES, tq), -jnp.inf, F32)

    def score_body(j, carry):
        mn, mx = carry
        k0 = pl.multiple_of(j * tq, tq)
        kib = ki_ref[pl.ds(k0, tq), :]
        acc = jnp.zeros((tq, tq), F32)
        for h in range(N_HEADS_IDX):
            d = jnp.dot(kib, qit_ref[h * IDX_DIM:(h + 1) * IDX_DIM, :], preferred_element_type=F32)
            acc = acc + jnp.maximum(d, 0.0) * wit_ref[h:h + 1, :]
        causal = (k0 + key_in_blk) <= (q0 + qry_in_blk)
        masked = jnp.where(causal, acc, -jnp.inf)
        sc_ref[j] = masked
        mx = jnp.maximum(mx, jnp.max(masked, axis=0, keepdims=True))
        mn = jnp.minimum(mn, jnp.min(jnp.where(causal, acc, jnp.inf), axis=0, keepdims=True))
        return mn, mx

    mn, mx = lax.fori_loop(0, nblk, score_body,
                           (jnp.full((1, tq), jnp.inf, F32), jnp.full((1, tq), -jnp.inf, F32)))

    def count(pred):
        a8 = lax.fori_loop(0, nblk, lambda j, a: a + fold(ind(pred(sc_ref[j]))), zeros8)
        return total(a8)

    def any_set(flag):
        return jnp.max(flag) > 0.0

    def init_body(j, st):
        a, b, c = st
        blk = sc_ref[j]
        return a + fold(ind(blk >= 0.0)), b + fold(ind(blk > 0.0)), c + fold(ind(blk >= mx))
    c_ge0, c_gt0, c_mx = [total(a)
                          for a in lax.fori_loop(0, nblk, init_body, (zeros8, zeros8, zeros8))]
    ncausal = (q0 + lax.broadcasted_iota(jnp.int32, (1, tq), 1) + 1).astype(F32)
    small = ncausal <= kf
    top_tied = c_mx >= kf
    neg = c_ge0 < kf
    zero_tie = (c_gt0 < kf) & ~neg
    lo = jnp.where(small | neg, mn, 0.0)
    lo = jnp.where(top_tied & ~small, mx, lo)
    clo = jnp.where(small | neg, ncausal, c_ge0)
    clo = jnp.where(top_tied & ~small, c_mx, clo)
    hi = jnp.where(neg, 0.0, mx)
    chi = jnp.where(neg, c_ge0, c_mx)
    searching = ~small & ~top_tied & ~zero_tie

    def bis_cond(st):
        return (st[0] < MAX_BISECT) & st[1]

    def bis_body(st):
        it, _, lo, hi, clo, chi, active = st
        mid = lo + (hi - lo) * 0.5
        live = (active > 0.0) & (mid > lo) & (mid < hi)
        c = count(lambda blk: blk >= mid)
        up = live & (c >= kf)
        dn = live & (c < kf)
        lo = jnp.where(up, mid, lo)
        clo = jnp.where(up, c, clo)
        hi = jnp.where(dn, mid, hi)
        chi = jnp.where(dn, c, chi)
        active = ind(live & (clo > kf) & (clo - chi > BRACKET_STOP))
        return it + 1, any_set(active), lo, hi, clo, chi, active

    active = ind(searching & (clo > kf) & (clo - chi > BRACKET_STOP))
    st = lax.while_loop(bis_cond, bis_body,
                        (jnp.int32(0), any_set(active), lo, hi, clo, chi, active))
    _, _, lo, hi, clo, chi, _ = st

    unres = ind(searching & (clo > kf))

    def peel_cond(st):
        return (st[0] < MAX_PEEL) & st[1]

    def peel_body(st):
        it, _, lo, hi, clo, chi, cand, unres = st

        def body(j, s2):
            a8, m8 = s2
            blk = sc_ref[j]
            ge = blk >= cand
            below = jnp.where((blk >= lo) & ~ge, blk, -jnp.inf)
            return a8 + fold(ind(ge)), jnp.maximum(m8, fold_max(below))
        a8, m8 = lax.fori_loop(0, nblk, body, (zeros8, ninf8))
        c = total(a8)
        nxt = total_max(m8)
        on = unres > 0.0
        found = on & (c >= kf)
        down = on & (c < kf)
        lo = jnp.where(found, cand, lo)
        clo = jnp.where(found, c, clo)
        hi = jnp.where(down, cand, hi)
        chi = jnp.where(down, c, chi)
        cand = jnp.where(down, nxt, cand)
        return it + 1, any_set(ind(down)), lo, hi, clo, chi, cand, ind(down)

    st = lax.while_loop(peel_cond, peel_body,
                        (jnp.int32(0), any_set(unres), lo, hi, clo, chi, hi, unres))
    _, _, lo, hi, clo, chi, _, _ = st

    tied = any_set(ind(~small & (clo > kf)))

    @pl.when(jnp.logical_not(tied))
    def _():
        def body(j, c):
            sel_ref[j] = ind(sc_ref[j] >= lo).T
            return c
        lax.fori_loop(0, nblk, body, 0)

    @pl.when(tied)
    def _():
        need = kf - count(lambda blk: blk > lo)
        earlier = ind(qry_in_blk < key_in_blk).astype(BF16)

        def body(j, seen):
            blk = sc_ref[j]
            eq = ind(blk == lo)
            rank = seen + jnp.dot(earlier, eq.astype(BF16), preferred_element_type=F32)
            keep = (blk > lo) | ((blk == lo) & (rank < need))
            sel_ref[j] = ind(keep).T
            return seen + total(fold(eq))
        lax.fori_loop(0, nblk, body, jnp.zeros((1, tq), F32))

    nrow = N_HEADS * tq
    m_ref[...] = jnp.full(m_ref.shape, -jnp.inf, F32)
    acc_ref[...] = jnp.zeros(acc_ref.shape, F32)
    wide = lambda a, n: jnp.concatenate([a] * n, axis=1)

    def attend(j, nkb, bias):
        tk = nkb * tq
        k0 = pl.multiple_of(j * tq, tq)
        kv1 = ckv_ref[pl.ds(k0, tk), :]
        sel = jnp.concatenate([sel_ref[j + b] for b in range(nkb)], axis=1) > 0.5
        hg = N_HEADS // ATT_GROUPS
        for g in range(ATT_GROUPS):
            rows = slice(g * hg * tq, (g + 1) * hg * tq)
            q = qabs_ref[g * hg:(g + 1) * hg].reshape(hg * tq, KV_LATENT)
            s = lax.dot_general(q, kv1[:, :KV_LATENT], (((1,), (1,)), ((), ())),
                                preferred_element_type=F32)
            s = s.reshape(hg, tq, tk)
            if bias is not None:
                s = s + bias(slice(g * hg, (g + 1) * hg))
            s = jnp.where(sel[None], s, -jnp.inf).reshape(hg * tq, tk)
            m_prev = m_ref[rows, :]
            m_new = jnp.maximum(m_prev, jnp.max(s, axis=1, keepdims=True))
            m_safe = jnp.where(m_new == -jnp.inf, 0.0, m_new)
            alpha = jnp.exp2(m_prev - m_safe)
            p = jnp.exp2(s - wide(m_safe, tk // LANES))
            m_ref[rows, :] = m_new
            pv = jnp.dot(p.astype(BF16), kv1, preferred_element_type=F32)
            acc_ref[rows, :] = wide(alpha, 2) * acc_ref[rows, :] + pv

    nfar = jnp.maximum(i - 1, 0)

    def far_pair(pi, c):
        attend(2 * pi, 2, None)
        return c
    lax.fori_loop(0, lax.shift_right_logical(nfar, 1), far_pair, 0)

    @pl.when((nfar & 1) == 1)
    def _():
        attend(nfar - 1, 1, None)

    @pl.when(i >= 1)
    def _():
        attend(i - 1, 2, lambda hs: bias_ref[hs])

    @pl.when(i == 0)
    def _():
        attend(0, 1, lambda hs: bias_ref[hs, :, tq:])

    ctx = []
    for h in range(N_HEADS):
        a = acc_ref[h * tq:(h + 1) * tq, :]
        ctx.append(a[:, :KV_LATENT] / a[:, KV_LATENT:])
    ctx = jnp.concatenate(ctx, axis=1).astype(BF16)
    out = jnp.dot(ctx, wuv_ref[...], preferred_element_type=F32)
    o_ref[...] = _rms(out, gout_ref[...]).astype(BF16)


def _dsa(qit, wit, qabs, ki, ckv, bias, wuv, gout, n_keep):
    b, _, s, _ = qabs.shape
    tq = Q_BLOCK
    nb = s // tq
    nq = N_HEADS_IDX * IDX_DIM
    return pl.pallas_call(
        functools.partial(_dsa_kernel, n_keep=n_keep),
        grid=(b, nb),
        in_specs=[
            pl.BlockSpec((None, nq, tq), lambda bi, i: (bi, 0, i)),
            pl.BlockSpec((None, N_HEADS_IDX, tq), lambda bi, i: (bi, 0, i)),
            pl.BlockSpec((None, N_HEADS, tq, KV_LATENT), lambda bi, i: (bi, 0, i, 0)),
            pl.BlockSpec((None, s, IDX_DIM), lambda bi, i: (bi, 0, 0)),
            pl.BlockSpec((None, s, 2 * KV_LATENT), lambda bi, i: (bi, 0, 0)),
            _const_spec(bias.shape), _const_spec(wuv.shape), _const_spec(gout.shape),
        ],
        out_specs=pl.BlockSpec((None, tq, ATTN_WIDTH), lambda bi, i: (bi, i, 0)),
        out_shape=jax.ShapeDtypeStruct((b, s, ATTN_WIDTH), BF16),
        scratch_shapes=[
            pltpu.VMEM((nb, tq, tq), F32),
            pltpu.VMEM((nb, tq, tq), F32),
            pltpu.VMEM((N_HEADS * tq, 2 * KV_LATENT), F32),
            pltpu.VMEM((N_HEADS * tq, LANES), F32),
        ],
        compiler_params=pltpu.CompilerParams(dimension_semantics=("arbitrary", "arbitrary"),
                                             vmem_limit_bytes=VMEM_LIMIT),
        name="dsa",
    )(qit, wit, qabs, ki, ckv, bias, wuv, gout)


def _block_diag(blocks):
    h, r, c = blocks.shape
    out = jnp.zeros((h, r, h, c), blocks.dtype)
    idx = jnp.arange(h)
    out = out.at[idx, :, idx, :].set(blocks)
    return out.reshape(h * r, h * c)


def kernel(x, ffn1_norm, ffn1_w_gate, ffn1_w_up, ffn1_w_down, mix_norm, w_in, kv_norm, w_uk, w_uv, rel_bias, conv_w, attn_out_norm, conv_out_norm, w_out, ffn2_norm, ffn2_w_gate, ffn2_w_up, ffn2_w_down, final_norm):
    b, s, d = x.shape
    depth = ffn1_norm.shape[0]
    n_keep = min(TOPK_MAX, s // 4)
    assert s % Q_BLOCK == 0 and s % PROJ_TOKENS == 0 and (b * s) % FFN_TOKENS == 0
    assert Q_BLOCK >= MAX_DISTANCE and n_keep <= Q_BLOCK
    row = lambda v: v.reshape(1, -1).astype(F32)
    offs = [0]
    for w in IN_SIZES:
        offs.append(offs[-1] + w)

    bias = _bias_tiles(rel_bias.astype(F32), Q_BLOCK)
    h = x.astype(F32).reshape(b * s, d)
    for l in range(depth):
        h = _ffn1(h, row(ffn1_norm[l]), ffn1_w_gate[l].astype(BF16), ffn1_w_up[l].astype(BF16),
                  ffn1_w_down[l].astype(BF16))

        wi = w_in[l]
        col = lambda k: wi[:, offs[k]:offs[k + 1]]
        wcat = jnp.concatenate(
            [col(0), col(1), col(3), jnp.zeros((d, LANES - IDX_DIM), wi.dtype), col(5), col(6), col(7)],
            axis=1).astype(BF16)
        wt = jnp.concatenate(
            [col(2).T, col(4).T, jnp.zeros((_T_ROWS - offs[3] + offs[2] - N_HEADS_IDX, d), wi.dtype)],
            axis=0).astype(BF16)
        wuk = _block_diag(jnp.transpose(w_uk[l], (1, 2, 0))).astype(BF16)
        wuv = _block_diag(jnp.transpose(w_uv[l], (1, 0, 2))).astype(BF16)

        qabs, ckv, ki, qit, wit, convn = _proj(
            h.reshape(b, s, d), row(mix_norm[l]), wcat, wt, wuk, row(kv_norm[l]),
            conv_w[l].astype(F32), row(conv_out_norm[l]))
        attn = _dsa(qit, wit, qabs, ki, ckv, bias, wuv, row(attn_out_norm[l]), n_keep)

        last = l == depth - 1
        h = _out_ffn2(h, attn.reshape(b * s, ATTN_WIDTH), convn.reshape(b * s, CONV_WIDTH),
                      w_out[l][:ATTN_WIDTH].astype(BF16), w_out[l][ATTN_WIDTH:].astype(BF16),
                      row(ffn2_norm[l]), ffn2_w_gate[l].astype(BF16), ffn2_w_up[l].astype(BF16),
                      ffn2_w_down[l].astype(BF16), row(final_norm), final_norm=last)
    return h.reshape(b, s, d).astype(x.dtype)
```

```python
import functools
import math

import jax
import jax.numpy as jnp
from jax import lax
from jax.experimental import pallas as pl
from jax.experimental.pallas import tpu as pltpu

F32 = jnp.float32
BF16 = jnp.bfloat16

EPS = 1e-6
N_HEADS = 8
HEAD_DIM = 64
ATTN_WIDTH = N_HEADS * HEAD_DIM
KV_LATENT = 128
N_HEADS_IDX = 8
IDX_DIM = 64
TOPK_MAX = 256
CONV_WIDTH = 512
CONV_K = 3
N_BUCKETS = 32
MAX_DISTANCE = 128
IN_SIZES = (ATTN_WIDTH, KV_LATENT, N_HEADS_IDX * IDX_DIM, IDX_DIM, N_HEADS_IDX,
            CONV_WIDTH, CONV_WIDTH, CONV_WIDTH)

SUBLANES = 8
LANES = 128

FFN_TOKENS = 512
FFN_CHUNK = 256
PROJ_TOKENS = 512
Q_BLOCK = 256
BISECT_CHUNK = 4
MAX_BISECT = 64
ATT_GROUPS = 8
SWEEP_ACCS = 4
BRACKET_STOP = 0
MAX_PEEL = 4096
LOG2E = math.log2(math.e)
VMEM_LIMIT = 56 * 1024 * 1024


def _rms(x, g):
    return x * lax.rsqrt(jnp.mean(x * x, axis=-1, keepdims=True) + EPS) * g


def _const_spec(shape):
    nd = len(shape)
    return pl.BlockSpec(shape, lambda *_: (0,) * nd, pipeline_mode=pl.Buffered(1))


def _swiglu_into(acc_ref, xn, wg_ref, wu_ref, wd_ref):
    d_ff = wg_ref.shape[1]
    for c in range(d_ff // FFN_CHUNK):
        sl = slice(c * FFN_CHUNK, (c + 1) * FFN_CHUNK)
        g = jnp.dot(xn, wg_ref[:, sl], preferred_element_type=F32)
        u = jnp.dot(xn, wu_ref[:, sl], preferred_element_type=F32)
        a = (g * jax.nn.sigmoid(g) * u).astype(BF16)
        d = jnp.dot(a, wd_ref[sl, :], preferred_element_type=F32)
        if c == 0:
            acc_ref[...] = d
        else:
            acc_ref[...] += d


def _ffn1_kernel(x_ref, g_ref, wg_ref, wu_ref, wd_ref, o_ref, acc_ref):
    x = x_ref[...]
    xn = _rms(x, g_ref[...]).astype(BF16)
    _swiglu_into(acc_ref, xn, wg_ref, wu_ref, wd_ref)
    o_ref[...] = x + 0.5 * acc_ref[...]


def _out_ffn2_kernel(h_ref, a_ref, c_ref, woa_ref, woc_ref, g_ref, wg_ref, wu_ref, wd_ref,
                     gf_ref, o_ref, acc_ref, *, final_norm):
    h = (h_ref[...]
         + jnp.dot(a_ref[...], woa_ref[...], preferred_element_type=F32)
         + jnp.dot(c_ref[...], woc_ref[...], preferred_element_type=F32))
    xn = _rms(h, g_ref[...]).astype(BF16)
    _swiglu_into(acc_ref, xn, wg_ref, wu_ref, wd_ref)
    h = h + 0.5 * acc_ref[...]
    o_ref[...] = _rms(h, gf_ref[...]) if final_norm else h


def _ffn1(x2, g, wg, wu, wd):
    n, d = x2.shape
    tile = pl.BlockSpec((FFN_TOKENS, d), lambda i: (i, 0))
    return pl.pallas_call(
        _ffn1_kernel,
        grid=(n // FFN_TOKENS,),
        in_specs=[tile, _const_spec(g.shape), _const_spec(wg.shape), _const_spec(wu.shape),
                  _const_spec(wd.shape)],
        out_specs=tile,
        out_shape=jax.ShapeDtypeStruct((n, d), F32),
        scratch_shapes=[pltpu.VMEM((FFN_TOKENS, d), F32)],
        compiler_params=pltpu.CompilerParams(dimension_semantics=("arbitrary",),
                                             vmem_limit_bytes=VMEM_LIMIT),
        name="ffn1",
    )(x2, g, wg, wu, wd)


def _out_ffn2(h2, a2, c2, woa, woc, g, wg, wu, wd, gf, final_norm):
    n, d = h2.shape
    tile = pl.BlockSpec((FFN_TOKENS, d), lambda i: (i, 0))
    mix = pl.BlockSpec((FFN_TOKENS, a2.shape[1]), lambda i: (i, 0))
    return pl.pallas_call(
        functools.partial(_out_ffn2_kernel, final_norm=final_norm),
        grid=(n // FFN_TOKENS,),
        in_specs=[tile, mix, mix, _const_spec(woa.shape), _const_spec(woc.shape),
                  _const_spec(g.shape), _const_spec(wg.shape), _const_spec(wu.shape),
                  _const_spec(wd.shape), _const_spec(gf.shape)],
        out_specs=tile,
        out_shape=jax.ShapeDtypeStruct((n, d), F32),
        scratch_shapes=[pltpu.VMEM((FFN_TOKENS, d), F32)],
        compiler_params=pltpu.CompilerParams(dimension_semantics=("arbitrary",),
                                             vmem_limit_bytes=VMEM_LIMIT),
        name="out_ffn2",
    )(h2, a2, c2, woa, woc, g, wg, wu, wd, gf)


_C_Q = 0
_C_KV = _C_Q + ATTN_WIDTH
_C_KI = _C_KV + KV_LATENT
_C_GB = _C_KI + LANES
_C_GC = _C_GB + CONV_WIDTH
_C_X = _C_GC + CONV_WIDTH
_C_END = _C_X + CONV_WIDTH
_T_ROWS = N_HEADS_IDX * IDX_DIM + 2 * SUBLANES


def _proj_kernel(h_ref, gm_ref, wcat_ref, wt_ref, wuk_ref, gkv_ref, cw_ref, gco_ref,
                 qabs_ref, ckv_ref, ki_ref, qit_ref, wit_ref, convn_ref, carry_ref):
    tp = h_ref.shape[0]

    @pl.when(pl.program_id(1) == 0)
    def _():
        carry_ref[...] = jnp.zeros_like(carry_ref)

    un = _rms(h_ref[...], gm_ref[...]).astype(BF16)
    p = jnp.dot(un, wcat_ref[...], preferred_element_type=F32)

    q = p[:, _C_Q:_C_KV].astype(BF16)
    qabs = jnp.dot(q, wuk_ref[...], preferred_element_type=F32) * (HEAD_DIM ** -0.5 * LOG2E)
    for h in range(N_HEADS):
        qabs_ref[h] = qabs[:, h * KV_LATENT:(h + 1) * KV_LATENT].astype(BF16)
    ckv_ref[:, :KV_LATENT] = _rms(p[:, _C_KV:_C_KI], gkv_ref[...]).astype(BF16)
    ckv_ref[:, KV_LATENT:] = jnp.ones((tp, KV_LATENT), BF16)
    ki_ref[...] = p[:, _C_KI:_C_KI + IDX_DIM].astype(BF16)

    v = p[:, _C_GC:_C_X] * p[:, _C_X:_C_END]
    row = lax.broadcasted_iota(jnp.int32, v.shape, 0)
    prev1 = jnp.broadcast_to(carry_ref[SUBLANES - 1:SUBLANES, :], v.shape)
    prev2 = jnp.broadcast_to(carry_ref[SUBLANES - 2:SUBLANES - 1, :], v.shape)
    v1 = jnp.where(row == 0, prev1, pltpu.roll(v, 1, 0))
    v2 = jnp.where(row == 0, prev2, jnp.where(row == 1, prev1, pltpu.roll(v, 2, 0)))
    carry_ref[...] = v[tp - SUBLANES:, :]
    y = cw_ref[0:1, :] * v2 + cw_ref[1:2, :] * v1 + cw_ref[2:3, :] * v
    conv = p[:, _C_GB:_C_GC] * y
    convn_ref[...] = _rms(conv, gco_ref[...]).astype(BF16)

    pt = lax.dot_general(wt_ref[...], un, (((1,), (1,)), ((), ())),
                         preferred_element_type=F32)
    nq = N_HEADS_IDX * IDX_DIM
    qit_ref[...] = pt[0:nq, :].astype(BF16)
    wit_ref[...] = pt[nq:nq + N_HEADS_IDX, :] * ((N_HEADS_IDX * IDX_DIM) ** -0.5)


def _proj(h1, gm, wcat, wt, wuk, gkv, cw, gco):
    b, s, d = h1.shape
    tp = PROJ_TOKENS
    nq = N_HEADS_IDX * IDX_DIM
    tok = lambda w: pl.BlockSpec((None, tp, w), lambda bi, ti: (bi, ti, 0))
    out_shape = (
        jax.ShapeDtypeStruct((b, N_HEADS, s, KV_LATENT), BF16),
        jax.ShapeDtypeStruct((b, s, 2 * KV_LATENT), BF16),
        jax.ShapeDtypeStruct((b, s, IDX_DIM), BF16),
        jax.ShapeDtypeStruct((b, nq, s), BF16),
        jax.ShapeDtypeStruct((b, N_HEADS_IDX, s), F32),
        jax.ShapeDtypeStruct((b, s, CONV_WIDTH), BF16),
    )
    out_specs = (
        pl.BlockSpec((None, N_HEADS, tp, KV_LATENT), lambda bi, ti: (bi, 0, ti, 0)),
        tok(2 * KV_LATENT), tok(IDX_DIM),
        pl.BlockSpec((None, nq, tp), lambda bi, ti: (bi, 0, ti)),
        pl.BlockSpec((None, N_HEADS_IDX, tp), lambda bi, ti: (bi, 0, ti)),
        tok(CONV_WIDTH),
    )
    return pl.pallas_call(
        _proj_kernel,
        grid=(b, s // tp),
        in_specs=[tok(d), _const_spec(gm.shape), _const_spec(wcat.shape), _const_spec(wt.shape),
                  _const_spec(wuk.shape), _const_spec(gkv.shape), _const_spec(cw.shape),
                  _const_spec(gco.shape)],
        out_specs=out_specs,
        out_shape=out_shape,
        scratch_shapes=[pltpu.VMEM((SUBLANES, CONV_WIDTH), F32)],
        compiler_params=pltpu.CompilerParams(dimension_semantics=("arbitrary", "arbitrary"),
                                             vmem_limit_bytes=VMEM_LIMIT),
        name="proj",
    )(h1, gm, wcat, wt, wuk, gkv, cw, gco)


def _rel_bucket(dist):
    max_exact = N_BUCKETS // 2
    d = jnp.maximum(dist, 1).astype(F32)
    large = max_exact + (jnp.log(d / max_exact) / math.log(MAX_DISTANCE / max_exact)
                         * (N_BUCKETS - max_exact)).astype(jnp.int32)
    large = jnp.minimum(large, N_BUCKETS - 1)
    return jnp.where(dist < max_exact, dist, large)


def _bias_kernel(rb_ref, o_ref):
    h = pl.program_id(0)
    tq, tk = o_ref.shape
    t = lax.broadcasted_iota(jnp.int32, (tq, tk), 0)
    s = lax.broadcasted_iota(jnp.int32, (tq, tk), 1)
    bucket = _rel_bucket(jnp.maximum(t - s + tq, 0))
    val = jnp.zeros((tq, tk), F32)
    for b in range(N_BUCKETS):
        val = jnp.where(bucket == b, rb_ref[b, h], val)
    o_ref[...] = (val - rb_ref[N_BUCKETS - 1, h]) * LOG2E


def _bias_tiles(rel_bias, tq):
    return pl.pallas_call(
        _bias_kernel,
        grid=(N_HEADS,),
        in_specs=[pl.BlockSpec(memory_space=pltpu.SMEM)],
        out_specs=pl.BlockSpec((None, tq, 2 * tq), lambda h: (h, 0, 0)),
        out_shape=jax.ShapeDtypeStruct((N_HEADS, tq, 2 * tq), F32),
        compiler_params=pltpu.CompilerParams(dimension_semantics=("arbitrary",)),
        name="bias_tiles",
    )(rel_bias)


def _dsa_kernel(qit_ref, wit_ref, qabs_ref, ki_ref, ckv_ref, bias_ref, wuv_ref, gout_ref,
                o_ref, sc_ref, sel_ref, acc_ref, m_ref, *, n_keep):
    tq = o_ref.shape[0]
    i = pl.program_id(1)
    nblk = i + 1
    q0 = i * tq
    kf = float(n_keep)
    key_in_blk = lax.broadcasted_iota(jnp.int32, (tq, tq), 0)
    qry_in_blk = lax.broadcasted_iota(jnp.int32, (tq, tq), 1)
    ind = lambda mask: jnp.where(mask, 1.0, 0.0)
    part = lambda a: a.reshape(tq // (SWEEP_ACCS * SUBLANES), SWEEP_ACCS, SUBLANES, tq)
    fold = lambda a: part(a).sum(axis=0)
    fold_max = lambda a: part(a).max(axis=0)
    total = lambda a: a.sum(axis=0).sum(axis=0, keepdims=True)
    total_max = lambda a: a.max(axis=0).max(axis=0, keepdims=True)
    zeros8 = jnp.zeros((SWEEP_ACCS, SUBLANES, tq), F32)
    ninf8 = jnp.full((SWEEP_ACCS, SUBLANES, tq), -jnp.inf, F32)

    def score_body(j, carry):
        mn, mx = carry
        k0 = pl.multiple_of(j * tq, tq)
        kib = ki_ref[pl.ds(k0, tq), :]
        acc = jnp.zeros((tq, tq), F32)
        for h in range(N_HEADS_IDX):
            d = jnp.dot(kib, qit_ref[h * IDX_DIM:(h + 1) * IDX_DIM, :], preferred_element_type=F32)
            acc = acc + jnp.maximum(d, 0.0) * wit_ref[h:h + 1, :]
        causal = (k0 + key_in_blk) <= (q0 + qry_in_blk)
        masked = jnp.where(causal, acc, -jnp.inf)
        sc_ref[j] = masked
        mx = jnp.maximum(mx, jnp.max(masked, axis=0, keepdims=True))
        mn = jnp.minimum(mn, jnp.min(jnp.where(causal, acc, jnp.inf), axis=0, keepdims=True))
        return mn, mx

    mn, mx = lax.fori_loop(0, nblk, score_body,
                           (jnp.full((1, tq), jnp.inf, F32), jnp.full((1, tq), -jnp.inf, F32)))

    def count(pred):
        a8 = lax.fori_loop(0, nblk, lambda j, a: a + fold(ind(pred(sc_ref[j]))), zeros8)
        return total(a8)

    def any_set(flag):
        return jnp.max(flag) > 0.0

    def init_body(j, st):
        a, b, c = st
        blk = sc_ref[j]
        return a + fold(ind(blk >= 0.0)), b + fold(ind(blk > 0.0)), c + fold(ind(blk >= mx))
    c_ge0, c_gt0, c_mx = [total(a)
                          for a in lax.fori_loop(0, nblk, init_body, (zeros8, zeros8, zeros8))]
    ncausal = (q0 + lax.broadcasted_iota(jnp.int32, (1, tq), 1) + 1).astype(F32)
    small = ncausal <= kf
    top_tied = c_mx >= kf
    neg = c_ge0 < kf
    zero_tie = (c_gt0 < kf) & ~neg
    lo = jnp.where(small | neg, mn, 0.0)
    lo = jnp.where(top_tied & ~small, mx, lo)
    clo = jnp.where(small | neg, ncausal, c_ge0)
    clo = jnp.where(top_tied & ~small, c_mx, clo)
    hi = jnp.where(neg, 0.0, mx)
    chi = jnp.where(neg, c_ge0, c_mx)
    searching = ~small & ~top_tied & ~zero_tie

    def bis_cond(st):
        return (st[0] < MAX_BISECT) & st[1]

    def bis_body(st):
        it, _, lo, hi, clo, chi, active = st
        for _ in range(BISECT_CHUNK):
            mid = lo + (hi - lo) * 0.5
            live = (active > 0.0) & (mid > lo) & (mid < hi)
            c = count(lambda blk: blk >= mid)
            up = live & (c >= kf)
            dn = live & (c < kf)
            lo = jnp.where(up, mid, lo)
            clo = jnp.where(up, c, clo)
            hi = jnp.where(dn, mid, hi)
            chi = jnp.where(dn, c, chi)
            active = ind(live & (clo > kf) & (clo - chi > BRACKET_STOP))
        return it + 1, any_set(active), lo, hi, clo, chi, active

    active = ind(searching & (clo > kf) & (clo - chi > BRACKET_STOP))
    st = lax.while_loop(bis_cond, bis_body,
                        (jnp.int32(0), any_set(active), lo, hi, clo, chi, active))
    _, _, lo, hi, clo, chi, _ = st

    unres = ind(searching & (clo > kf))

    def peel_cond(st):
        return (st[0] < MAX_PEEL) & st[1]

    def peel_body(st):
        it, _, lo, hi, clo, chi, cand, unres = st

        def body(j, s2):
            a8, m8 = s2
            blk = sc_ref[j]
            ge = blk >= cand
            below = jnp.where((blk >= lo) & ~ge, blk, -jnp.inf)
            return a8 + fold(ind(ge)), jnp.maximum(m8, fold_max(below))
        a8, m8 = lax.fori_loop(0, nblk, body, (zeros8, ninf8))
        c = total(a8)
        nxt = total_max(m8)
        on = unres > 0.0
        found = on & (c >= kf)
        down = on & (c < kf)
        lo = jnp.where(found, cand, lo)
        clo = jnp.where(found, c, clo)
        hi = jnp.where(down, cand, hi)
        chi = jnp.where(down, c, chi)
        cand = jnp.where(down, nxt, cand)
        return it + 1, any_set(ind(down)), lo, hi, clo, chi, cand, ind(down)

    st = lax.while_loop(peel_cond, peel_body,
                        (jnp.int32(0), any_set(unres), lo, hi, clo, chi, hi, unres))
    _, _, lo, hi, clo, chi, _, _ = st

    tied = any_set(ind(~small & (clo > kf)))

    @pl.when(jnp.logical_not(tied))
    def _():
        def body(j, c):
            sel_ref[j] = ind(sc_ref[j] >= lo).T
            return c
        lax.fori_loop(0, nblk, body, 0)

    @pl.when(tied)
    def _():
        need = kf - count(lambda blk: blk > lo)
        earlier = ind(qry_in_blk < key_in_blk).astype(BF16)

        def body(j, seen):
            blk = sc_ref[j]
            eq = ind(blk == lo)
            rank = seen + jnp.dot(earlier, eq.astype(BF16), preferred_element_type=F32)
            keep = (blk > lo) | ((blk == lo) & (rank < need))
            sel_ref[j] = ind(keep).T
            return seen + total(fold(eq))
        lax.fori_loop(0, nblk, body, jnp.zeros((1, tq), F32))

    nrow = N_HEADS * tq
    m_ref[...] = jnp.full(m_ref.shape, -jnp.inf, F32)
    acc_ref[...] = jnp.zeros(acc_ref.shape, F32)
    wide = lambda a, n: jnp.concatenate([a] * n, axis=1)

    def attend(j, nkb, bias):
        tk = nkb * tq
        k0 = pl.multiple_of(j * tq, tq)
        kv1 = ckv_ref[pl.ds(k0, tk), :]
        sel = jnp.concatenate([sel_ref[j + b] for b in range(nkb)], axis=1) > 0.5
        hg = N_HEADS // ATT_GROUPS
        for g in range(ATT_GROUPS):
            rows = slice(g * hg * tq, (g + 1) * hg * tq)
            q = qabs_ref[g * hg:(g + 1) * hg].reshape(hg * tq, KV_LATENT)
            s = lax.dot_general(q, kv1[:, :KV_LATENT], (((1,), (1,)), ((), ())),
                                preferred_element_type=F32)
            s = s.reshape(hg, tq, tk)
            if bias is not None:
                s = s + bias(slice(g * hg, (g + 1) * hg))
            s = jnp.where(sel[None], s, -jnp.inf).reshape(hg * tq, tk)
            m_prev = m_ref[rows, :]
            m_new = jnp.maximum(m_prev, jnp.max(s, axis=1, keepdims=True))
            m_safe = jnp.where(m_new == -jnp.inf, 0.0, m_new)
            alpha = jnp.exp2(m_prev - m_safe)
            p = jnp.exp2(s - wide(m_safe, tk // LANES))
            m_ref[rows, :] = m_new
            pv = jnp.dot(p.astype(BF16), kv1, preferred_element_type=F32)
            acc_ref[rows, :] = wide(alpha, 2) * acc_ref[rows, :] + pv

    nfar = jnp.maximum(i - 1, 0)

    def far_pair(pi, c):
        attend(2 * pi, 2, None)
        return c
    lax.fori_loop(0, lax.shift_right_logical(nfar, 1), far_pair, 0)

    @pl.when((nfar & 1) == 1)
    def _():
        attend(nfar - 1, 1, None)

    @pl.when(i >= 1)
    def _():
        attend(i - 1, 2, lambda hs: bias_ref[hs])

    @pl.when(i == 0)
    def _():
        attend(0, 1, lambda hs: bias_ref[hs, :, tq:])

    ctx = []
    for h in range(N_HEADS):
        a = acc_ref[h * tq:(h + 1) * tq, :]
        ctx.append(a[:, :KV_LATENT] / a[:, KV_LATENT:])
    ctx = jnp.concatenate(ctx, axis=1).astype(BF16)
    out = jnp.dot(ctx, wuv_ref[...], preferred_element_type=F32)
    o_ref[...] = _rms(out, gout_ref[...]).astype(BF16)


def _dsa(qit, wit, qabs, ki, ckv, bias, wuv, gout, n_keep):
    b, _, s, _ = qabs.shape
    tq = Q_BLOCK
    nb = s // tq
    nq = N_HEADS_IDX * IDX_DIM
    return pl.pallas_call(
        functools.partial(_dsa_kernel, n_keep=n_keep),
        grid=(b, nb),
        in_specs=[
            pl.BlockSpec((None, nq, tq), lambda bi, i: (bi, 0, i)),
            pl.BlockSpec((None, N_HEADS_IDX, tq), lambda bi, i: (bi, 0, i)),
            pl.BlockSpec((None, N_HEADS, tq, KV_LATENT), lambda bi, i: (bi, 0, i, 0)),
            pl.BlockSpec((None, s, IDX_DIM), lambda bi, i: (bi, 0, 0)),
            pl.BlockSpec((None, s, 2 * KV_LATENT), lambda bi, i: (bi, 0, 0)),
            _const_spec(bias.shape), _const_spec(wuv.shape), _const_spec(gout.shape),
        ],
        out_specs=pl.BlockSpec((None, tq, ATTN_WIDTH), lambda bi, i: (bi, i, 0)),
        out_shape=jax.ShapeDtypeStruct((b, s, ATTN_WIDTH), BF16),
        scratch_shapes=[
            pltpu.VMEM((nb, tq, tq), F32),
            pltpu.VMEM((nb, tq, tq), F32),
            pltpu.VMEM((N_HEADS * tq, 2 * KV_LATENT), F32),
            pltpu.VMEM((N_HEADS * tq, LANES), F32),
        ],
        compiler_params=pltpu.CompilerParams(dimension_semantics=("arbitrary", "arbitrary"),
                                             vmem_limit_bytes=VMEM_LIMIT),
        name="dsa",
    )(qit, wit, qabs, ki, ckv, bias, wuv, gout)


def _block_diag(blocks):
    h, r, c = blocks.shape
    out = jnp.zeros((h, r, h, c), blocks.dtype)
    idx = jnp.arange(h)
    out = out.at[idx, :, idx, :].set(blocks)
    return out.reshape(h * r, h * c)


def kernel(x, ffn1_norm, ffn1_w_gate, ffn1_w_up, ffn1_w_down, mix_norm, w_in, kv_norm, w_uk, w_uv, rel_bias, conv_w, attn_out_norm, conv_out_norm, w_out, ffn2_norm, ffn2_w_gate, ffn2_w_up, ffn2_w_down, final_norm):
    b, s, d = x.shape
    depth = ffn1_norm.shape[0]
    n_keep = min(TOPK_MAX, s // 4)
    assert s % Q_BLOCK == 0 and s % PROJ_TOKENS == 0 and (b * s) % FFN_TOKENS == 0
    assert Q_BLOCK >= MAX_DISTANCE and n_keep <= Q_BLOCK
    row = lambda v: v.reshape(1, -1).astype(F32)
    offs = [0]
    for w in IN_SIZES:
        offs.append(offs[-1] + w)

    bias = _bias_tiles(rel_bias.astype(F32), Q_BLOCK)
    h = x.astype(F32).reshape(b * s, d)
    for l in range(depth):
        h = _ffn1(h, row(ffn1_norm[l]), ffn1_w_gate[l].astype(BF16), ffn1_w_up[l].astype(BF16),
                  ffn1_w_down[l].astype(BF16))

        wi = w_in[l]
        col = lambda k: wi[:, offs[k]:offs[k + 1]]
        wcat = jnp.concatenate(
            [col(0), col(1), col(3), jnp.zeros((d, LANES - IDX_DIM), wi.dtype), col(5), col(6), col(7)],
            axis=1).astype(BF16)
        wt = jnp.concatenate(
            [col(2).T, col(4).T, jnp.zeros((_T_ROWS - offs[3] + offs[2] - N_HEADS_IDX, d), wi.dtype)],
            axis=0).astype(BF16)
        wuk = _block_diag(jnp.transpose(w_uk[l], (1, 2, 0))).astype(BF16)
        wuv = _block_diag(jnp.transpose(w_uv[l], (1, 0, 2))).astype(BF16)

        qabs, ckv, ki, qit, wit, convn = _proj(
            h.reshape(b, s, d), row(mix_norm[l]), wcat, wt, wuk, row(kv_norm[l]),
            conv_w[l].astype(F32), row(conv_out_norm[l]))
        attn = _dsa(qit, wit, qabs, ki, ckv, bias, wuv, row(attn_out_norm[l]), n_keep)

        last = l == depth - 1
        h = _out_ffn2(h, attn.reshape(b * s, ATTN_WIDTH), convn.reshape(b * s, CONV_WIDTH),
                      w_out[l][:ATTN_WIDTH].astype(BF16), w_out[l][ATTN_WIDTH:].astype(BF16),
                      row(ffn2_norm[l]), ffn2_w_gate[l].astype(BF16), ffn2_w_up[l].astype(BF16),
                      ffn2_w_down[l].astype(BF16), row(final_norm), final_norm=last)
    return h.reshape(b, s, d).astype(x.dtype)
```

```python
import functools
import math

import jax
import jax.numpy as jnp
from jax import lax
from jax.experimental import pallas as pl
from jax.experimental.pallas import tpu as pltpu

F32 = jnp.float32
BF16 = jnp.bfloat16

EPS = 1e-6
N_HEADS = 8
HEAD_DIM = 64
ATTN_WIDTH = N_HEADS * HEAD_DIM
KV_LATENT = 128
N_HEADS_IDX = 8
IDX_DIM = 64
TOPK_MAX = 256
CONV_WIDTH = 512
CONV_K = 3
N_BUCKETS = 32
MAX_DISTANCE = 128
IN_SIZES = (ATTN_WIDTH, KV_LATENT, N_HEADS_IDX * IDX_DIM, IDX_DIM, N_HEADS_IDX,
            CONV_WIDTH, CONV_WIDTH, CONV_WIDTH)

SUBLANES = 8
LANES = 128

FFN_TOKENS = 512
FFN_CHUNK = 256
PROJ_TOKENS = 512
Q_BLOCK = 256
BISECT_CHUNK = 4
MAX_BISECT = 64
SWEEP_ACCS = 4
BRACKET_STOP = 0
MAX_PEEL = 4096
LOG2E = math.log2(math.e)
VMEM_LIMIT = 56 * 1024 * 1024


def _rms(x, g):
    return x * lax.rsqrt(jnp.mean(x * x, axis=-1, keepdims=True) + EPS) * g


def _const_spec(shape):
    nd = len(shape)
    return pl.BlockSpec(shape, lambda *_: (0,) * nd, pipeline_mode=pl.Buffered(1))


def _swiglu_into(acc_ref, xn, wg_ref, wu_ref, wd_ref):
    d_ff = wg_ref.shape[1]
    for c in range(d_ff // FFN_CHUNK):
        sl = slice(c * FFN_CHUNK, (c + 1) * FFN_CHUNK)
        g = jnp.dot(xn, wg_ref[:, sl], preferred_element_type=F32)
        u = jnp.dot(xn, wu_ref[:, sl], preferred_element_type=F32)
        a = (g * jax.nn.sigmoid(g) * u).astype(BF16)
        d = jnp.dot(a, wd_ref[sl, :], preferred_element_type=F32)
        if c == 0:
            acc_ref[...] = d
        else:
            acc_ref[...] += d


def _ffn1_kernel(x_ref, g_ref, wg_ref, wu_ref, wd_ref, o_ref, acc_ref):
    x = x_ref[...]
    xn = _rms(x, g_ref[...]).astype(BF16)
    _swiglu_into(acc_ref, xn, wg_ref, wu_ref, wd_ref)
    o_ref[...] = x + 0.5 * acc_ref[...]


def _out_ffn2_kernel(h_ref, a_ref, c_ref, woa_ref, woc_ref, g_ref, wg_ref, wu_ref, wd_ref,
                     gf_ref, o_ref, acc_ref, *, final_norm):
    h = (h_ref[...]
         + jnp.dot(a_ref[...], woa_ref[...], preferred_element_type=F32)
         + jnp.dot(c_ref[...], woc_ref[...], preferred_element_type=F32))
    xn = _rms(h, g_ref[...]).astype(BF16)
    _swiglu_into(acc_ref, xn, wg_ref, wu_ref, wd_ref)
    h = h + 0.5 * acc_ref[...]
    o_ref[...] = _rms(h, gf_ref[...]) if final_norm else h


def _ffn1(x2, g, wg, wu, wd):
    n, d = x2.shape
    tile = pl.BlockSpec((FFN_TOKENS, d), lambda i: (i, 0))
    return pl.pallas_call(
        _ffn1_kernel,
        grid=(n // FFN_TOKENS,),
        in_specs=[tile, _const_spec(g.shape), _const_spec(wg.shape), _const_spec(wu.shape),
                  _const_spec(wd.shape)],
        out_specs=tile,
        out_shape=jax.ShapeDtypeStruct((n, d), F32),
        scratch_shapes=[pltpu.VMEM((FFN_TOKENS, d), F32)],
        compiler_params=pltpu.CompilerParams(dimension_semantics=("arbitrary",),
                                             vmem_limit_bytes=VMEM_LIMIT),
        name="ffn1",
    )(x2, g, wg, wu, wd)


def _out_ffn2(h2, a2, c2, woa, woc, g, wg, wu, wd, gf, final_norm):
    n, d = h2.shape
    tile = pl.BlockSpec((FFN_TOKENS, d), lambda i: (i, 0))
    mix = pl.BlockSpec((FFN_TOKENS, a2.shape[1]), lambda i: (i, 0))
    return pl.pallas_call(
        functools.partial(_out_ffn2_kernel, final_norm=final_norm),
        grid=(n // FFN_TOKENS,),
        in_specs=[tile, mix, mix, _const_spec(woa.shape), _const_spec(woc.shape),
                  _const_spec(g.shape), _const_spec(wg.shape), _const_spec(wu.shape),
                  _const_spec(wd.shape), _const_spec(gf.shape)],
        out_specs=tile,
        out_shape=jax.ShapeDtypeStruct((n, d), F32),
        scratch_shapes=[pltpu.VMEM((FFN_TOKENS, d), F32)],
        compiler_params=pltpu.CompilerParams(dimension_semantics=("arbitrary",),
                                             vmem_limit_bytes=VMEM_LIMIT),
        name="out_ffn2",
    )(h2, a2, c2, woa, woc, g, wg, wu, wd, gf)


_C_KV = 0
_C_KI = _C_KV + KV_LATENT
_C_GB = _C_KI + LANES
_C_GC = _C_GB + CONV_WIDTH
_C_X = _C_GC + CONV_WIDTH
_C_END = _C_X + CONV_WIDTH
_R_Q = 0
_R_KV = _R_Q + ATTN_WIDTH
_R_QI = _R_KV + KV_LATENT
_R_WI = _R_QI + N_HEADS_IDX * IDX_DIM
_R_END = _R_WI + 2 * SUBLANES
KVT_ROWS = KV_LATENT + 2 * SUBLANES


def _proj_kernel(h_ref, gm_ref, wcat_ref, wt_ref, wukt_ref, gkv_ref, gkvc_ref, cw_ref, gco_ref,
                 qat_ref, ckv_ref, ckvt_ref, ki_ref, qit_ref, wit_ref, convn_ref, carry_ref):
    tp = h_ref.shape[0]

    @pl.when(pl.program_id(1) == 0)
    def _():
        carry_ref[...] = jnp.zeros_like(carry_ref)

    un = _rms(h_ref[...], gm_ref[...]).astype(BF16)
    p = jnp.dot(un, wcat_ref[...], preferred_element_type=F32)
    ckv_ref[...] = _rms(p[:, _C_KV:_C_KI], gkv_ref[...]).astype(BF16)
    ki_ref[...] = p[:, _C_KI:_C_KI + IDX_DIM].astype(BF16)

    v = p[:, _C_GC:_C_X] * p[:, _C_X:_C_END]
    row = lax.broadcasted_iota(jnp.int32, v.shape, 0)
    prev1 = jnp.broadcast_to(carry_ref[SUBLANES - 1:SUBLANES, :], v.shape)
    prev2 = jnp.broadcast_to(carry_ref[SUBLANES - 2:SUBLANES - 1, :], v.shape)
    v1 = jnp.where(row == 0, prev1, pltpu.roll(v, 1, 0))
    v2 = jnp.where(row == 0, prev2, jnp.where(row == 1, prev1, pltpu.roll(v, 2, 0)))
    carry_ref[...] = v[tp - SUBLANES:, :]
    y = cw_ref[0:1, :] * v2 + cw_ref[1:2, :] * v1 + cw_ref[2:3, :] * v
    conv = p[:, _C_GB:_C_GC] * y
    convn_ref[...] = _rms(conv, gco_ref[...]).astype(BF16)

    pt = lax.dot_general(wt_ref[...], un, (((1,), (1,)), ((), ())),
                         preferred_element_type=F32)
    qt = pt[_R_Q:_R_KV, :].astype(BF16)
    qat = jnp.dot(wukt_ref[...], qt, preferred_element_type=F32) * (HEAD_DIM ** -0.5 * LOG2E)
    qat_ref[...] = qat.astype(BF16)
    kvt = pt[_R_KV:_R_QI, :]
    kvt = kvt * lax.rsqrt(jnp.mean(kvt * kvt, axis=0, keepdims=True) + EPS) * gkvc_ref[...]
    tk = ckvt_ref.shape[2]
    tail = jnp.concatenate([jnp.ones((SUBLANES, tk), F32), jnp.zeros((SUBLANES, tk), F32)], axis=0)
    for kb in range(tp // tk):
        blk = jnp.concatenate([kvt[:, kb * tk:(kb + 1) * tk], tail], axis=0)
        ckvt_ref[kb] = blk.astype(BF16)
    qit_ref[...] = pt[_R_QI:_R_WI, :].astype(BF16)
    wit_ref[...] = pt[_R_WI:_R_WI + N_HEADS_IDX, :] * ((N_HEADS_IDX * IDX_DIM) ** -0.5)


def _proj(h1, gm, wcat, wt, wukt, gkv, cw, gco):
    b, s, d = h1.shape
    tp = PROJ_TOKENS
    nq = N_HEADS_IDX * IDX_DIM
    nqa = N_HEADS * KV_LATENT
    gkvc = gkv.reshape(KV_LATENT, 1)
    tok = lambda w: pl.BlockSpec((None, tp, w), lambda bi, ti: (bi, ti, 0))
    tokt = lambda r: pl.BlockSpec((None, r, tp), lambda bi, ti: (bi, 0, ti))
    out_shape = (
        jax.ShapeDtypeStruct((b, nqa, s), BF16),
        jax.ShapeDtypeStruct((b, s, KV_LATENT), BF16),
        jax.ShapeDtypeStruct((b, s // Q_BLOCK, KVT_ROWS, Q_BLOCK), BF16),
        jax.ShapeDtypeStruct((b, s, IDX_DIM), BF16),
        jax.ShapeDtypeStruct((b, nq, s), BF16),
        jax.ShapeDtypeStruct((b, N_HEADS_IDX, s), F32),
        jax.ShapeDtypeStruct((b, s, CONV_WIDTH), BF16),
    )
    out_specs = (
        tokt(nqa), tok(KV_LATENT),
        pl.BlockSpec((None, tp // Q_BLOCK, KVT_ROWS, Q_BLOCK), lambda bi, ti: (bi, ti, 0, 0)),
        tok(IDX_DIM), tokt(nq), tokt(N_HEADS_IDX), tok(CONV_WIDTH),
    )
    return pl.pallas_call(
        _proj_kernel,
        grid=(b, s // tp),
        in_specs=[tok(d), _const_spec(gm.shape), _const_spec(wcat.shape), _const_spec(wt.shape),
                  _const_spec(wukt.shape), _const_spec(gkv.shape), _const_spec(gkvc.shape),
                  _const_spec(cw.shape), _const_spec(gco.shape)],
        out_specs=out_specs,
        out_shape=out_shape,
        scratch_shapes=[pltpu.VMEM((SUBLANES, CONV_WIDTH), F32)],
        compiler_params=pltpu.CompilerParams(dimension_semantics=("arbitrary", "arbitrary"),
                                             vmem_limit_bytes=VMEM_LIMIT),
        name="proj",
    )(h1, gm, wcat, wt, wukt, gkv, gkvc, cw, gco)


def _rel_bucket(dist):
    max_exact = N_BUCKETS // 2
    d = jnp.maximum(dist, 1).astype(F32)
    large = max_exact + (jnp.log(d / max_exact) / math.log(MAX_DISTANCE / max_exact)
                         * (N_BUCKETS - max_exact)).astype(jnp.int32)
    large = jnp.minimum(large, N_BUCKETS - 1)
    return jnp.where(dist < max_exact, dist, large)


def _bias_kernel(rb_ref, o_ref):
    h = pl.program_id(0)
    tk, tq = o_ref.shape
    s = lax.broadcasted_iota(jnp.int32, (tk, tq), 0)
    t = lax.broadcasted_iota(jnp.int32, (tk, tq), 1)
    bucket = _rel_bucket(jnp.maximum(t - s + tq, 0))
    val = jnp.zeros((tk, tq), F32)
    for b in range(N_BUCKETS):
        val = jnp.where(bucket == b, rb_ref[b, h], val)
    o_ref[...] = (val - rb_ref[N_BUCKETS - 1, h]) * LOG2E


def _bias_tiles(rel_bias, tq):
    return pl.pallas_call(
        _bias_kernel,
        grid=(N_HEADS,),
        in_specs=[pl.BlockSpec(memory_space=pltpu.SMEM)],
        out_specs=pl.BlockSpec((None, 2 * tq, tq), lambda h: (h, 0, 0)),
        out_shape=jax.ShapeDtypeStruct((N_HEADS, 2 * tq, tq), F32),
        compiler_params=pltpu.CompilerParams(dimension_semantics=("arbitrary",)),
        name="bias_tiles",
    )(rel_bias)


def _dsa_kernel(qit_ref, wit_ref, qat_ref, ki_ref, ckv_ref, ckvt_ref, bias_ref, wuvt_ref, gout_ref,
                o_ref, sc_ref, mask_ref, acc_ref, m_ref, *, n_keep):
    tq = o_ref.shape[0]
    i = pl.program_id(1)
    nblk = i + 1
    q0 = i * tq
    kf = float(n_keep)
    key_in_blk = lax.broadcasted_iota(jnp.int32, (tq, tq), 0)
    qry_in_blk = lax.broadcasted_iota(jnp.int32, (tq, tq), 1)
    ind = lambda mask: jnp.where(mask, 1.0, 0.0)
    part = lambda a: a.reshape(tq // (SWEEP_ACCS * SUBLANES), SWEEP_ACCS, SUBLANES, tq)
    fold = lambda a: part(a).sum(axis=0)
    fold_max = lambda a: part(a).max(axis=0)
    total = lambda a: a.sum(axis=0).sum(axis=0, keepdims=True)
    total_max = lambda a: a.max(axis=0).max(axis=0, keepdims=True)
    zeros8 = jnp.zeros((SWEEP_ACCS, SUBLANES, tq), F32)
    ninf8 = jnp.full((SWEEP_ACCS, SUBLANES, tq), -jnp.inf, F32)

    def score_body(j, carry):
        mn, mx = carry
        k0 = pl.multiple_of(j * tq, tq)
        kib = ki_ref[pl.ds(k0, tq), :]
        acc = jnp.zeros((tq, tq), F32)
        for h in range(N_HEADS_IDX):
            d = jnp.dot(kib, qit_ref[h * IDX_DIM:(h + 1) * IDX_DIM, :], preferred_element_type=F32)
            acc = acc + jnp.maximum(d, 0.0) * wit_ref[h:h + 1, :]
        causal = (k0 + key_in_blk) <= (q0 + qry_in_blk)
        masked = jnp.where(causal, acc, -jnp.inf)
        sc_ref[j] = masked
        mx = jnp.maximum(mx, jnp.max(masked, axis=0, keepdims=True))
        mn = jnp.minimum(mn, jnp.min(jnp.where(causal, acc, jnp.inf), axis=0, keepdims=True))
        return mn, mx

    mn, mx = lax.fori_loop(0, nblk, score_body,
                           (jnp.full((1, tq), jnp.inf, F32), jnp.full((1, tq), -jnp.inf, F32)))

    def count(pred):
        a8 = lax.fori_loop(0, nblk, lambda j, a: a + fold(ind(pred(sc_ref[j]))), zeros8)
        return total(a8)

    def any_set(flag):
        return jnp.max(flag) > 0.0

    def init_body(j, st):
        a, b, c = st
        blk = sc_ref[j]
        return a + fold(ind(blk >= 0.0)), b + fold(ind(blk > 0.0)), c + fold(ind(blk >= mx))
    c_ge0, c_gt0, c_mx = [total(a)
                          for a in lax.fori_loop(0, nblk, init_body, (zeros8, zeros8, zeros8))]
    ncausal = (q0 + lax.broadcasted_iota(jnp.int32, (1, tq), 1) + 1).astype(F32)
    small = ncausal <= kf
    top_tied = c_mx >= kf
    neg = c_ge0 < kf
    zero_tie = (c_gt0 < kf) & ~neg
    lo = jnp.where(small | neg, mn, 0.0)
    lo = jnp.where(top_tied & ~small, mx, lo)
    clo = jnp.where(small | neg, ncausal, c_ge0)
    clo = jnp.where(top_tied & ~small, c_mx, clo)
    hi = jnp.where(neg, 0.0, mx)
    chi = jnp.where(neg, c_ge0, c_mx)
    searching = ~small & ~top_tied & ~zero_tie

    def bis_cond(st):
        return (st[0] < MAX_BISECT) & st[1]

    def bis_body(st):
        it, _, lo, hi, clo, chi, active = st
        for _ in range(BISECT_CHUNK):
            mid = lo + (hi - lo) * 0.5
            live = (active > 0.0) & (mid > lo) & (mid < hi)
            c = count(lambda blk: blk >= mid)
            up = live & (c >= kf)
            dn = live & (c < kf)
            lo = jnp.where(up, mid, lo)
            clo = jnp.where(up, c, clo)
            hi = jnp.where(dn, mid, hi)
            chi = jnp.where(dn, c, chi)
            active = ind(live & (clo > kf) & (clo - chi > BRACKET_STOP))
        return it + 1, any_set(active), lo, hi, clo, chi, active

    active = ind(searching & (clo > kf) & (clo - chi > BRACKET_STOP))
    st = lax.while_loop(bis_cond, bis_body,
                        (jnp.int32(0), any_set(active), lo, hi, clo, chi, active))
    _, _, lo, hi, clo, chi, _ = st

    unres = ind(searching & (clo > kf))

    def peel_cond(st):
        return (st[0] < MAX_PEEL) & st[1]

    def peel_body(st):
        it, _, lo, hi, clo, chi, cand, unres = st

        def body(j, s2):
            a8, m8 = s2
            blk = sc_ref[j]
            ge = blk >= cand
            below = jnp.where((blk >= lo) & ~ge, blk, -jnp.inf)
            return a8 + fold(ind(ge)), jnp.maximum(m8, fold_max(below))
        a8, m8 = lax.fori_loop(0, nblk, body, (zeros8, ninf8))
        c = total(a8)
        nxt = total_max(m8)
        on = unres > 0.0
        found = on & (c >= kf)
        down = on & (c < kf)
        lo = jnp.where(found, cand, lo)
        clo = jnp.where(found, c, clo)
        hi = jnp.where(down, cand, hi)
        chi = jnp.where(down, c, chi)
        cand = jnp.where(down, nxt, cand)
        return it + 1, any_set(ind(down)), lo, hi, clo, chi, cand, ind(down)

    st = lax.while_loop(peel_cond, peel_body,
                        (jnp.int32(0), any_set(unres), lo, hi, clo, chi, hi, unres))
    _, _, lo, hi, clo, chi, _, _ = st

    tied = any_set(ind(~small & (clo > kf)))

    @pl.when(jnp.logical_not(tied))
    def _():
        def body(j, c):
            mask_ref[j] = jnp.where(sc_ref[j] >= lo, 0.0, -jnp.inf)
            return c
        lax.fori_loop(0, nblk, body, 0)

    @pl.when(tied)
    def _():
        need = kf - count(lambda blk: blk > lo)
        earlier = ind(qry_in_blk < key_in_blk).astype(BF16)

        def body(j, seen):
            blk = sc_ref[j]
            eq = ind(blk == lo)
            rank = seen + jnp.dot(earlier, eq.astype(BF16), preferred_element_type=F32)
            keep = (blk > lo) | ((blk == lo) & (rank < need))
            mask_ref[j] = jnp.where(keep, 0.0, -jnp.inf)
            return seen + total(fold(eq))
        lax.fori_loop(0, nblk, body, jnp.zeros((1, tq), F32))

    m_ref[...] = jnp.full(m_ref.shape, -jnp.inf, F32)
    acc_ref[...] = jnp.zeros(acc_ref.shape, F32)

    def attend(j, nkb, bias):
        tk = nkb * tq
        k0 = pl.multiple_of(j * tq, tq)
        kv = ckv_ref[pl.ds(k0, tk), :]
        kvt1 = jnp.concatenate([ckvt_ref[j + b] for b in range(nkb)], axis=1)
        mask = jnp.concatenate([mask_ref[j + b] for b in range(nkb)], axis=0)
        heads = lambda f: jnp.concatenate([f(h) for h in range(N_HEADS)], axis=1)
        qa = heads(lambda h: qat_ref[h * KV_LATENT:(h + 1) * KV_LATENT, :])
        s = jnp.dot(kv, qa, preferred_element_type=F32)
        if bias is not None:
            s = s + heads(bias)
        s = s + heads(lambda h: mask)
        m_prev = m_ref[...]
        m_new = jnp.maximum(m_prev, jnp.max(s, axis=0, keepdims=True))
        m_safe = jnp.where(m_new == -jnp.inf, 0.0, m_new)
        alpha = jnp.exp2(m_prev - m_safe)
        p = jnp.exp2(s - m_safe).astype(BF16)
        m_ref[...] = m_new
        pv = jnp.dot(kvt1, p, preferred_element_type=F32)
        acc_ref[...] = alpha * acc_ref[...] + pv

    nfar = jnp.maximum(i - 1, 0)

    def far_pair(pi, c):
        attend(2 * pi, 2, None)
        return c
    lax.fori_loop(0, lax.shift_right_logical(nfar, 1), far_pair, 0)

    @pl.when((nfar & 1) == 1)
    def _():
        attend(nfar - 1, 1, None)

    @pl.when(i >= 1)
    def _():
        attend(i - 1, 2, lambda h: bias_ref[h])

    @pl.when(i == 0)
    def _():
        attend(0, 1, lambda h: bias_ref[h, tq:, :])

    ctx = []
    for h in range(N_HEADS):
        a = acc_ref[:, h * tq:(h + 1) * tq]
        ctx.append(a[:KV_LATENT, :] / a[KV_LATENT:KV_LATENT + 1, :])
    ctx = jnp.concatenate(ctx, axis=0).astype(BF16)
    out = jnp.dot(wuvt_ref[...], ctx, preferred_element_type=F32).T
    o_ref[...] = _rms(out, gout_ref[...]).astype(BF16)


def _dsa(qit, wit, qat, ki, ckv, ckvt, bias, wuvt, gout, n_keep):
    b, s, _ = ckv.shape
    tq = Q_BLOCK
    nb = s // tq
    nq = N_HEADS_IDX * IDX_DIM
    colblk = lambda r: pl.BlockSpec((None, r, tq), lambda bi, i: (bi, 0, i))
    return pl.pallas_call(
        functools.partial(_dsa_kernel, n_keep=n_keep),
        grid=(b, nb),
        in_specs=[
            colblk(nq), colblk(N_HEADS_IDX), colblk(N_HEADS * KV_LATENT),
            pl.BlockSpec((None, s, IDX_DIM), lambda bi, i: (bi, 0, 0)),
            pl.BlockSpec((None, s, KV_LATENT), lambda bi, i: (bi, 0, 0)),
            pl.BlockSpec((None, nb, KVT_ROWS, tq), lambda bi, i: (bi, 0, 0, 0)),
            _const_spec(bias.shape), _const_spec(wuvt.shape), _const_spec(gout.shape),
        ],
        out_specs=pl.BlockSpec((None, tq, ATTN_WIDTH), lambda bi, i: (bi, i, 0)),
        out_shape=jax.ShapeDtypeStruct((b, s, ATTN_WIDTH), BF16),
        scratch_shapes=[
            pltpu.VMEM((nb, tq, tq), F32),
            pltpu.VMEM((nb, tq, tq), F32),
            pltpu.VMEM((KVT_ROWS, N_HEADS * tq), F32),
            pltpu.VMEM((1, N_HEADS * tq), F32),
        ],
        compiler_params=pltpu.CompilerParams(dimension_semantics=("arbitrary", "arbitrary"),
                                             vmem_limit_bytes=VMEM_LIMIT),
        name="dsa",
    )(qit, wit, qat, ki, ckv, ckvt, bias, wuvt, gout)


def _block_diag(blocks):
    h, r, c = blocks.shape
    out = jnp.zeros((h, r, h, c), blocks.dtype)
    idx = jnp.arange(h)
    out = out.at[idx, :, idx, :].set(blocks)
    return out.reshape(h * r, h * c)


def kernel(x, ffn1_norm, ffn1_w_gate, ffn1_w_up, ffn1_w_down, mix_norm, w_in, kv_norm, w_uk, w_uv, rel_bias, conv_w, attn_out_norm, conv_out_norm, w_out, ffn2_norm, ffn2_w_gate, ffn2_w_up, ffn2_w_down, final_norm):
    b, s, d = x.shape
    depth = ffn1_norm.shape[0]
    n_keep = min(TOPK_MAX, s // 4)
    assert s % Q_BLOCK == 0 and s % PROJ_TOKENS == 0 and (b * s) % FFN_TOKENS == 0
    assert Q_BLOCK >= MAX_DISTANCE and n_keep <= Q_BLOCK
    row = lambda v: v.reshape(1, -1).astype(F32)
    offs = [0]
    for w in IN_SIZES:
        offs.append(offs[-1] + w)

    bias = _bias_tiles(rel_bias.astype(F32), Q_BLOCK)
    h = x.astype(F32).reshape(b * s, d)
    for l in range(depth):
        h = _ffn1(h, row(ffn1_norm[l]), ffn1_w_gate[l].astype(BF16), ffn1_w_up[l].astype(BF16),
                  ffn1_w_down[l].astype(BF16))

        wi = w_in[l]
        col = lambda k: wi[:, offs[k]:offs[k + 1]]
        wcat = jnp.concatenate(
            [col(1), col(3), jnp.zeros((d, LANES - IDX_DIM), wi.dtype), col(5), col(6), col(7)],
            axis=1).astype(BF16)
        wt = jnp.concatenate(
            [col(0).T, col(1).T, col(2).T, col(4).T,
             jnp.zeros((_R_END - _R_WI - N_HEADS_IDX, d), wi.dtype)], axis=0).astype(BF16)
        wukt = _block_diag(jnp.transpose(w_uk[l], (1, 0, 2))).astype(BF16)
        wuvt = _block_diag(jnp.transpose(w_uv[l], (1, 2, 0))).astype(BF16)

        qat, ckv, ckvt, ki, qit, wit, convn = _proj(
            h.reshape(b, s, d), row(mix_norm[l]), wcat, wt, wukt, row(kv_norm[l]),
            conv_w[l].astype(F32), row(conv_out_norm[l]))
        attn = _dsa(qit, wit, qat, ki, ckv, ckvt, bias, wuvt, row(attn_out_norm[l]), n_keep)

        last = l == depth - 1
        h = _out_ffn2(h, attn.reshape(b * s, ATTN_WIDTH), convn.reshape(b * s, CONV_WIDTH),
                      w_out[l][:ATTN_WIDTH].astype(BF16), w_out[l][ATTN_WIDTH:].astype(BF16),
                      row(ffn2_norm[l]), ffn2_w_gate[l].astype(BF16), ffn2_w_up[l].astype(BF16),
                      ffn2_w_down[l].astype(BF16), row(final_norm), final_norm=last)
    return h.reshape(b, s, d).astype(x.dtype)
```

```python
import functools
import math

import jax
import jax.numpy as jnp
from jax import lax
from jax.experimental import pallas as pl
from jax.experimental.pallas import tpu as pltpu

F32 = jnp.float32
BF16 = jnp.bfloat16

EPS = 1e-6
N_HEADS = 8
HEAD_DIM = 64
ATTN_WIDTH = N_HEADS * HEAD_DIM
KV_LATENT = 128
N_HEADS_IDX = 8
IDX_DIM = 64
TOPK_MAX = 256
CONV_WIDTH = 512
CONV_K = 3
N_BUCKETS = 32
MAX_DISTANCE = 128
IN_SIZES = (ATTN_WIDTH, KV_LATENT, N_HEADS_IDX * IDX_DIM, IDX_DIM, N_HEADS_IDX,
            CONV_WIDTH, CONV_WIDTH, CONV_WIDTH)

SUBLANES = 8
LANES = 128

FFN_TOKENS = 512
FFN_CHUNK = 256
PROJ_TOKENS = 512
Q_BLOCK = 256
BISECT_CHUNK = 4
MAX_BISECT = 64
SWEEP_ACCS = 4
BRACKET_STOP = 0
MAX_PEEL = 4096
LOG2E = math.log2(math.e)
VMEM_LIMIT = 56 * 1024 * 1024


def _rms(x, g):
    return x * lax.rsqrt(jnp.mean(x * x, axis=-1, keepdims=True) + EPS) * g


def _const_spec(shape):
    nd = len(shape)
    return pl.BlockSpec(shape, lambda *_: (0,) * nd, pipeline_mode=pl.Buffered(1))


def _swiglu_into(acc_ref, xn, wg_ref, wu_ref, wd_ref):
    d_ff = wg_ref.shape[1]
    for c in range(d_ff // FFN_CHUNK):
        sl = slice(c * FFN_CHUNK, (c + 1) * FFN_CHUNK)
        g = jnp.dot(xn, wg_ref[:, sl], preferred_element_type=F32)
        u = jnp.dot(xn, wu_ref[:, sl], preferred_element_type=F32)
        a = (g * jax.nn.sigmoid(g) * u).astype(BF16)
        d = jnp.dot(a, wd_ref[sl, :], preferred_element_type=F32)
        if c == 0:
            acc_ref[...] = d
        else:
            acc_ref[...] += d


def _ffn1_kernel(x_ref, g_ref, wg_ref, wu_ref, wd_ref, o_ref, acc_ref):
    x = x_ref[...]
    xn = _rms(x, g_ref[...]).astype(BF16)
    _swiglu_into(acc_ref, xn, wg_ref, wu_ref, wd_ref)
    o_ref[...] = x + 0.5 * acc_ref[...]


def _out_ffn2_kernel(h_ref, a_ref, c_ref, woa_ref, woc_ref, g_ref, wg_ref, wu_ref, wd_ref,
                     gf_ref, o_ref, acc_ref, *, final_norm):
    h = (h_ref[...]
         + jnp.dot(a_ref[...], woa_ref[...], preferred_element_type=F32)
         + jnp.dot(c_ref[...], woc_ref[...], preferred_element_type=F32))
    xn = _rms(h, g_ref[...]).astype(BF16)
    _swiglu_into(acc_ref, xn, wg_ref, wu_ref, wd_ref)
    h = h + 0.5 * acc_ref[...]
    o_ref[...] = _rms(h, gf_ref[...]) if final_norm else h


def _ffn1(x2, g, wg, wu, wd):
    n, d = x2.shape
    tile = pl.BlockSpec((FFN_TOKENS, d), lambda i: (i, 0))
    return pl.pallas_call(
        _ffn1_kernel,
        grid=(n // FFN_TOKENS,),
        in_specs=[tile, _const_spec(g.shape), _const_spec(wg.shape), _const_spec(wu.shape),
                  _const_spec(wd.shape)],
        out_specs=tile,
        out_shape=jax.ShapeDtypeStruct((n, d), F32),
        scratch_shapes=[pltpu.VMEM((FFN_TOKENS, d), F32)],
        compiler_params=pltpu.CompilerParams(dimension_semantics=("arbitrary",),
                                             vmem_limit_bytes=VMEM_LIMIT),
        name="ffn1",
    )(x2, g, wg, wu, wd)


def _out_ffn2(h2, a2, c2, woa, woc, g, wg, wu, wd, gf, final_norm):
    n, d = h2.shape
    tile = pl.BlockSpec((FFN_TOKENS, d), lambda i: (i, 0))
    mix = pl.BlockSpec((FFN_TOKENS, a2.shape[1]), lambda i: (i, 0))
    return pl.pallas_call(
        functools.partial(_out_ffn2_kernel, final_norm=final_norm),
        grid=(n // FFN_TOKENS,),
        in_specs=[tile, mix, mix, _const_spec(woa.shape), _const_spec(woc.shape),
                  _const_spec(g.shape), _const_spec(wg.shape), _const_spec(wu.shape),
                  _const_spec(wd.shape), _const_spec(gf.shape)],
        out_specs=tile,
        out_shape=jax.ShapeDtypeStruct((n, d), F32),
        scratch_shapes=[pltpu.VMEM((FFN_TOKENS, d), F32)],
        compiler_params=pltpu.CompilerParams(dimension_semantics=("arbitrary",),
                                             vmem_limit_bytes=VMEM_LIMIT),
        name="out_ffn2",
    )(h2, a2, c2, woa, woc, g, wg, wu, wd, gf)


_C_KV = 0
_C_KI = _C_KV + KV_LATENT
_C_GB = _C_KI + LANES
_C_GC = _C_GB + CONV_WIDTH
_C_X = _C_GC + CONV_WIDTH
_C_END = _C_X + CONV_WIDTH
_R_Q = 0
_R_KV = _R_Q + ATTN_WIDTH
_R_QI = _R_KV + KV_LATENT
_R_WI = _R_QI + N_HEADS_IDX * IDX_DIM
_R_END = _R_WI + 2 * SUBLANES
KVT_ROWS = KV_LATENT + 2 * SUBLANES


def _proj_kernel(h_ref, gm_ref, wcat_ref, wt_ref, wukt_ref, gkv_ref, gkvc_ref, cw_ref, gco_ref,
                 qat_ref, ckv_ref, ckvt_ref, ki_ref, qit_ref, wit_ref, convn_ref, carry_ref):
    tp = h_ref.shape[0]

    @pl.when(pl.program_id(1) == 0)
    def _():
        carry_ref[...] = jnp.zeros_like(carry_ref)

    un = _rms(h_ref[...], gm_ref[...]).astype(BF16)
    p = jnp.dot(un, wcat_ref[...], preferred_element_type=F32)
    ckv_ref[...] = _rms(p[:, _C_KV:_C_KI], gkv_ref[...]).astype(BF16)
    ki_ref[...] = p[:, _C_KI:_C_KI + IDX_DIM].astype(BF16)

    v = p[:, _C_GC:_C_X] * p[:, _C_X:_C_END]
    row = lax.broadcasted_iota(jnp.int32, v.shape, 0)
    prev1 = jnp.broadcast_to(carry_ref[SUBLANES - 1:SUBLANES, :], v.shape)
    prev2 = jnp.broadcast_to(carry_ref[SUBLANES - 2:SUBLANES - 1, :], v.shape)
    v1 = jnp.where(row == 0, prev1, pltpu.roll(v, 1, 0))
    v2 = jnp.where(row == 0, prev2, jnp.where(row == 1, prev1, pltpu.roll(v, 2, 0)))
    carry_ref[...] = v[tp - SUBLANES:, :]
    y = cw_ref[0:1, :] * v2 + cw_ref[1:2, :] * v1 + cw_ref[2:3, :] * v
    conv = p[:, _C_GB:_C_GC] * y
    convn_ref[...] = _rms(conv, gco_ref[...]).astype(BF16)

    pt = lax.dot_general(wt_ref[...], un, (((1,), (1,)), ((), ())),
                         preferred_element_type=F32)
    qt = pt[_R_Q:_R_KV, :].astype(BF16)
    for h in range(N_HEADS):
        qah = jnp.dot(wukt_ref[h], qt[h * HEAD_DIM:(h + 1) * HEAD_DIM, :],
                      preferred_element_type=F32)
        qat_ref[h * KV_LATENT:(h + 1) * KV_LATENT, :] = (qah * (HEAD_DIM ** -0.5 * LOG2E)).astype(BF16)
    kvt = pt[_R_KV:_R_QI, :]
    kvt = kvt * lax.rsqrt(jnp.mean(kvt * kvt, axis=0, keepdims=True) + EPS) * gkvc_ref[...]
    tk = ckvt_ref.shape[2]
    tail = jnp.concatenate([jnp.ones((SUBLANES, tk), F32), jnp.zeros((SUBLANES, tk), F32)], axis=0)
    for kb in range(tp // tk):
        blk = jnp.concatenate([kvt[:, kb * tk:(kb + 1) * tk], tail], axis=0)
        ckvt_ref[kb] = blk.astype(BF16)
    qit_ref[...] = pt[_R_QI:_R_WI, :].astype(BF16)
    wit_ref[...] = pt[_R_WI:_R_WI + N_HEADS_IDX, :] * ((N_HEADS_IDX * IDX_DIM) ** -0.5)


def _proj(h1, gm, wcat, wt, wukt, gkv, cw, gco):
    b, s, d = h1.shape
    tp = PROJ_TOKENS
    nq = N_HEADS_IDX * IDX_DIM
    nqa = N_HEADS * KV_LATENT
    gkvc = gkv.reshape(KV_LATENT, 1)
    tok = lambda w: pl.BlockSpec((None, tp, w), lambda bi, ti: (bi, ti, 0))
    tokt = lambda r: pl.BlockSpec((None, r, tp), lambda bi, ti: (bi, 0, ti))
    out_shape = (
        jax.ShapeDtypeStruct((b, nqa, s), BF16),
        jax.ShapeDtypeStruct((b, s, KV_LATENT), BF16),
        jax.ShapeDtypeStruct((b, s // Q_BLOCK, KVT_ROWS, Q_BLOCK), BF16),
        jax.ShapeDtypeStruct((b, s, IDX_DIM), BF16),
        jax.ShapeDtypeStruct((b, nq, s), BF16),
        jax.ShapeDtypeStruct((b, N_HEADS_IDX, s), F32),
        jax.ShapeDtypeStruct((b, s, CONV_WIDTH), BF16),
    )
    out_specs = (
        tokt(nqa), tok(KV_LATENT),
        pl.BlockSpec((None, tp // Q_BLOCK, KVT_ROWS, Q_BLOCK), lambda bi, ti: (bi, ti, 0, 0)),
        tok(IDX_DIM), tokt(nq), tokt(N_HEADS_IDX), tok(CONV_WIDTH),
    )
    return pl.pallas_call(
        _proj_kernel,
        grid=(b, s // tp),
        in_specs=[tok(d), _const_spec(gm.shape), _const_spec(wcat.shape), _const_spec(wt.shape),
                  _const_spec(wukt.shape), _const_spec(gkv.shape), _const_spec(gkvc.shape),
                  _const_spec(cw.shape), _const_spec(gco.shape)],
        out_specs=out_specs,
        out_shape=out_shape,
        scratch_shapes=[pltpu.VMEM((SUBLANES, CONV_WIDTH), F32)],
        compiler_params=pltpu.CompilerParams(dimension_semantics=("arbitrary", "arbitrary"),
                                             vmem_limit_bytes=VMEM_LIMIT),
        name="proj",
    )(h1, gm, wcat, wt, wukt, gkv, gkvc, cw, gco)


def _bucket_starts():
    e = N_BUCKETS // 2
    n = N_BUCKETS - e
    assert MAX_DISTANCE % e == 0
    r = MAX_DISTANCE // e
    starts, d = [], e
    for k in range(1, n):
        while d ** n < e ** n * r ** k:
            d += 1
        starts.append(d)
    return starts


def _rel_bucket(dist):
    max_exact = N_BUCKETS // 2
    large = max_exact
    for start in _bucket_starts():
        large = large + (dist >= start).astype(jnp.int32)
    return jnp.where(dist < max_exact, dist, large)


def _bias_kernel(rb_ref, o_ref):
    h = pl.program_id(0)
    tk, tq = o_ref.shape
    s = lax.broadcasted_iota(jnp.int32, (tk, tq), 0)
    t = lax.broadcasted_iota(jnp.int32, (tk, tq), 1)
    bucket = _rel_bucket(jnp.maximum(t - s + tq, 0))
    val = jnp.zeros((tk, tq), F32)
    for b in range(N_BUCKETS):
        val = jnp.where(bucket == b, rb_ref[b, h], val)
    o_ref[...] = (val - rb_ref[N_BUCKETS - 1, h]) * LOG2E


def _bias_tiles(rel_bias, tq):
    return pl.pallas_call(
        _bias_kernel,
        grid=(N_HEADS,),
        in_specs=[pl.BlockSpec(memory_space=pltpu.SMEM)],
        out_specs=pl.BlockSpec((None, 2 * tq, tq), lambda h: (h, 0, 0)),
        out_shape=jax.ShapeDtypeStruct((N_HEADS, 2 * tq, tq), F32),
        compiler_params=pltpu.CompilerParams(dimension_semantics=("arbitrary",)),
        name="bias_tiles",
    )(rel_bias)


def _dsa_kernel(qit_ref, wit_ref, qat_ref, ki_ref, ckv_ref, ckvt_ref, bias_ref, wuvt_ref, gout_ref,
                o_ref, sc_ref, mask_ref, acc_ref, m_ref, *, n_keep):
    tq = o_ref.shape[0]
    i = pl.program_id(1)
    nblk = i + 1
    q0 = i * tq
    kf = float(n_keep)
    key_in_blk = lax.broadcasted_iota(jnp.int32, (tq, tq), 0)
    qry_in_blk = lax.broadcasted_iota(jnp.int32, (tq, tq), 1)
    ind = lambda mask: jnp.where(mask, 1.0, 0.0)
    part = lambda a: a.reshape(tq // (SWEEP_ACCS * SUBLANES), SWEEP_ACCS, SUBLANES, tq)
    fold = lambda a: part(a).sum(axis=0)
    fold_max = lambda a: part(a).max(axis=0)
    total = lambda a: a.sum(axis=0).sum(axis=0, keepdims=True)
    total_max = lambda a: a.max(axis=0).max(axis=0, keepdims=True)
    zeros8 = jnp.zeros((SWEEP_ACCS, SUBLANES, tq), F32)
    ninf8 = jnp.full((SWEEP_ACCS, SUBLANES, tq), -jnp.inf, F32)

    def score_body(j, carry):
        mn, mx = carry
        k0 = pl.multiple_of(j * tq, tq)
        kib = ki_ref[pl.ds(k0, tq), :]
        acc = jnp.zeros((tq, tq), F32)
        for h in range(N_HEADS_IDX):
            d = jnp.dot(kib, qit_ref[h * IDX_DIM:(h + 1) * IDX_DIM, :], preferred_element_type=F32)
            acc = acc + jnp.maximum(d, 0.0) * wit_ref[h:h + 1, :]
        causal = (k0 + key_in_blk) <= (q0 + qry_in_blk)
        masked = jnp.where(causal, acc, -jnp.inf)
        sc_ref[j] = masked
        mx = jnp.maximum(mx, jnp.max(masked, axis=0, keepdims=True))
        mn = jnp.minimum(mn, jnp.min(jnp.where(causal, acc, jnp.inf), axis=0, keepdims=True))
        return mn, mx

    def over_blocks(body, init):
        def pair(pi, c):
            return body(2 * pi + 1, body(2 * pi, c))
        c = lax.fori_loop(0, lax.shift_right_logical(nblk, 1), pair, init)
        return lax.cond((nblk & 1) == 1, lambda c: body(nblk - 1, c), lambda c: c, c)

    mn, mx = over_blocks(score_body,
                         (jnp.full((1, tq), jnp.inf, F32), jnp.full((1, tq), -jnp.inf, F32)))

    def count(pred):
        a8 = lax.fori_loop(0, nblk, lambda j, a: a + fold(ind(pred(sc_ref[j]))), zeros8)
        return total(a8)

    def any_set(flag):
        return jnp.max(flag) > 0.0

    def init_body(j, st):
        a, b, c = st
        blk = sc_ref[j]
        return a + fold(ind(blk >= 0.0)), b + fold(ind(blk > 0.0)), c + fold(ind(blk >= mx))
    c_ge0, c_gt0, c_mx = [total(a)
                          for a in lax.fori_loop(0, nblk, init_body, (zeros8, zeros8, zeros8))]
    ncausal = (q0 + lax.broadcasted_iota(jnp.int32, (1, tq), 1) + 1).astype(F32)
    small = ncausal <= kf
    top_tied = c_mx >= kf
    neg = c_ge0 < kf
    zero_tie = (c_gt0 < kf) & ~neg
    lo = jnp.where(small | neg, mn, 0.0)
    lo = jnp.where(top_tied & ~small, mx, lo)
    clo = jnp.where(small | neg, ncausal, c_ge0)
    clo = jnp.where(top_tied & ~small, c_mx, clo)
    hi = jnp.where(neg, 0.0, mx)
    chi = jnp.where(neg, c_ge0, c_mx)
    searching = ~small & ~top_tied & ~zero_tie

    def bis_cond(st):
        return (st[0] < MAX_BISECT) & st[1]

    def bis_body(st):
        it, _, lo, hi, clo, chi, active = st
        for _ in range(BISECT_CHUNK):
            mid = lo + (hi - lo) * 0.5
            live = (active > 0.0) & (mid > lo) & (mid < hi)
            c = count(lambda blk: blk >= mid)
            up = live & (c >= kf)
            dn = live & (c < kf)
            lo = jnp.where(up, mid, lo)
            clo = jnp.where(up, c, clo)
            hi = jnp.where(dn, mid, hi)
            chi = jnp.where(dn, c, chi)
            active = ind(live & (clo > kf) & (clo - chi > BRACKET_STOP))
        return it + 1, any_set(active), lo, hi, clo, chi, active

    active = ind(searching & (clo > kf) & (clo - chi > BRACKET_STOP))
    st = lax.while_loop(bis_cond, bis_body,
                        (jnp.int32(0), any_set(active), lo, hi, clo, chi, active))
    _, _, lo, hi, clo, chi, _ = st

    unres = ind(searching & (clo > kf))

    def peel_cond(st):
        return (st[0] < MAX_PEEL) & st[1]

    def peel_body(st):
        it, _, lo, hi, clo, chi, cand, unres = st

        def body(j, s2):
            a8, m8 = s2
            blk = sc_ref[j]
            ge = blk >= cand
            below = jnp.where((blk >= lo) & ~ge, blk, -jnp.inf)
            return a8 + fold(ind(ge)), jnp.maximum(m8, fold_max(below))
        a8, m8 = lax.fori_loop(0, nblk, body, (zeros8, ninf8))
        c = total(a8)
        nxt = total_max(m8)
        on = unres > 0.0
        found = on & (c >= kf)
        down = on & (c < kf)
        lo = jnp.where(found, cand, lo)
        clo = jnp.where(found, c, clo)
        hi = jnp.where(down, cand, hi)
        chi = jnp.where(down, c, chi)
        cand = jnp.where(down, nxt, cand)
        return it + 1, any_set(ind(down)), lo, hi, clo, chi, cand, ind(down)

    st = lax.while_loop(peel_cond, peel_body,
                        (jnp.int32(0), any_set(unres), lo, hi, clo, chi, hi, unres))
    _, _, lo, hi, clo, chi, _, _ = st

    tied = any_set(ind(~small & (clo > kf)))

    @pl.when(jnp.logical_not(tied))
    def _():
        def body(j, c):
            mask_ref[j] = jnp.where(sc_ref[j] >= lo, 0.0, -jnp.inf)
            return c
        lax.fori_loop(0, nblk, body, 0)

    @pl.when(tied)
    def _():
        need = kf - count(lambda blk: blk > lo)
        earlier = ind(qry_in_blk < key_in_blk).astype(BF16)

        def body(j, seen):
            blk = sc_ref[j]
            eq = ind(blk == lo)
            rank = seen + jnp.dot(earlier, eq.astype(BF16), preferred_element_type=F32)
            keep = (blk > lo) | ((blk == lo) & (rank < need))
            mask_ref[j] = jnp.where(keep, 0.0, -jnp.inf)
            return seen + total(fold(eq))
        over_blocks(body, jnp.zeros((1, tq), F32))

    m_ref[...] = jnp.full(m_ref.shape, -jnp.inf, F32)
    acc_ref[...] = jnp.zeros(acc_ref.shape, F32)

    def attend(j, nkb, bias):
        tk = nkb * tq
        k0 = pl.multiple_of(j * tq, tq)
        kv = ckv_ref[pl.ds(k0, tk), :]
        kvt1 = jnp.concatenate([ckvt_ref[j + b] for b in range(nkb)], axis=1)
        mask = jnp.concatenate([mask_ref[j + b] for b in range(nkb)], axis=0)
        heads = lambda f: jnp.concatenate([f(h) for h in range(N_HEADS)], axis=1)
        qa = heads(lambda h: qat_ref[h * KV_LATENT:(h + 1) * KV_LATENT, :])
        s = jnp.dot(kv, qa, preferred_element_type=F32)
        if bias is not None:
            s = s + heads(bias)
        s = s + heads(lambda h: mask)
        m_prev = m_ref[...]
        m_new = jnp.maximum(m_prev, jnp.max(s, axis=0, keepdims=True))
        m_safe = jnp.where(m_new == -jnp.inf, 0.0, m_new)
        alpha = jnp.exp2(m_prev - m_safe)
        p = jnp.exp2(s - m_safe).astype(BF16)
        m_ref[...] = m_new
        pv = jnp.dot(kvt1, p, preferred_element_type=F32)
        acc_ref[...] = alpha * acc_ref[...] + pv

    nfar = jnp.maximum(i - 1, 0)

    def far_pair(pi, c):
        attend(2 * pi, 2, None)
        return c
    lax.fori_loop(0, lax.shift_right_logical(nfar, 1), far_pair, 0)

    @pl.when((nfar & 1) == 1)
    def _():
        attend(nfar - 1, 1, None)

    @pl.when(i >= 1)
    def _():
        attend(i - 1, 2, lambda h: bias_ref[h])

    @pl.when(i == 0)
    def _():
        attend(0, 1, lambda h: bias_ref[h, tq:, :])

    outs = []
    for h in range(N_HEADS):
        a = acc_ref[:, h * tq:(h + 1) * tq]
        ctx = (a[:KV_LATENT, :] / a[KV_LATENT:KV_LATENT + 1, :]).astype(BF16)
        outs.append(jnp.dot(wuvt_ref[h], ctx, preferred_element_type=F32))
    out = jnp.concatenate(outs, axis=0).T
    o_ref[...] = _rms(out, gout_ref[...]).astype(BF16)


def _dsa(qit, wit, qat, ki, ckv, ckvt, bias, wuvt, gout, n_keep):
    b, s, _ = ckv.shape
    tq = Q_BLOCK
    nb = s // tq
    nq = N_HEADS_IDX * IDX_DIM
    colblk = lambda r: pl.BlockSpec((None, r, tq), lambda bi, i: (bi, 0, i))
    return pl.pallas_call(
        functools.partial(_dsa_kernel, n_keep=n_keep),
        grid=(b, nb),
        in_specs=[
            colblk(nq), colblk(N_HEADS_IDX), colblk(N_HEADS * KV_LATENT),
            pl.BlockSpec((None, s, IDX_DIM), lambda bi, i: (bi, 0, 0)),
            pl.BlockSpec((None, s, KV_LATENT), lambda bi, i: (bi, 0, 0)),
            pl.BlockSpec((None, nb, KVT_ROWS, tq), lambda bi, i: (bi, 0, 0, 0)),
            _const_spec(bias.shape), _const_spec(wuvt.shape), _const_spec(gout.shape),
        ],
        out_specs=pl.BlockSpec((None, tq, ATTN_WIDTH), lambda bi, i: (bi, i, 0)),
        out_shape=jax.ShapeDtypeStruct((b, s, ATTN_WIDTH), BF16),
        scratch_shapes=[
            pltpu.VMEM((nb, tq, tq), F32),
            pltpu.VMEM((nb, tq, tq), F32),
            pltpu.VMEM((KVT_ROWS, N_HEADS * tq), F32),
            pltpu.VMEM((1, N_HEADS * tq), F32),
        ],
        compiler_params=pltpu.CompilerParams(dimension_semantics=("arbitrary", "arbitrary"),
                                             vmem_limit_bytes=VMEM_LIMIT),
        name="dsa",
    )(qit, wit, qat, ki, ckv, ckvt, bias, wuvt, gout)


def kernel(x, ffn1_norm, ffn1_w_gate, ffn1_w_up, ffn1_w_down, mix_norm, w_in, kv_norm, w_uk, w_uv, rel_bias, conv_w, attn_out_norm, conv_out_norm, w_out, ffn2_norm, ffn2_w_gate, ffn2_w_up, ffn2_w_down, final_norm):
    b, s, d = x.shape
    depth = ffn1_norm.shape[0]
    n_keep = min(TOPK_MAX, s // 4)
    assert s % Q_BLOCK == 0 and s % PROJ_TOKENS == 0 and (b * s) % FFN_TOKENS == 0
    assert Q_BLOCK >= MAX_DISTANCE and n_keep <= Q_BLOCK
    row = lambda v: v.reshape(1, -1).astype(F32)
    offs = [0]
    for w in IN_SIZES:
        offs.append(offs[-1] + w)

    bias = _bias_tiles(rel_bias.astype(F32), Q_BLOCK)
    h = x.astype(F32).reshape(b * s, d)
    for l in range(depth):
        h = _ffn1(h, row(ffn1_norm[l]), ffn1_w_gate[l].astype(BF16), ffn1_w_up[l].astype(BF16),
                  ffn1_w_down[l].astype(BF16))

        wi = w_in[l]
        col = lambda k: wi[:, offs[k]:offs[k + 1]]
        wcat = jnp.concatenate(
            [col(1), col(3), jnp.zeros((d, LANES - IDX_DIM), wi.dtype), col(5), col(6), col(7)],
            axis=1).astype(BF16)
        wt = jnp.concatenate(
            [col(0).T, col(1).T, col(2).T, col(4).T,
             jnp.zeros((_R_END - _R_WI - N_HEADS_IDX, d), wi.dtype)], axis=0).astype(BF16)
        wukt = jnp.transpose(w_uk[l], (1, 0, 2)).astype(BF16)
        wuvt = jnp.transpose(w_uv[l], (1, 2, 0)).astype(BF16)

        qat, ckv, ckvt, ki, qit, wit, convn = _proj(
            h.reshape(b, s, d), row(mix_norm[l]), wcat, wt, wukt, row(kv_norm[l]),
            conv_w[l].astype(F32), row(conv_out_norm[l]))
        attn = _dsa(qit, wit, qat, ki, ckv, ckvt, bias, wuvt, row(attn_out_norm[l]), n_keep)

        last = l == depth - 1
        h = _out_ffn2(h, attn.reshape(b * s, ATTN_WIDTH), convn.reshape(b * s, CONV_WIDTH),
                      w_out[l][:ATTN_WIDTH].astype(BF16), w_out[l][ATTN_WIDTH:].astype(BF16),
                      row(ffn2_norm[l]), ffn2_w_gate[l].astype(BF16), ffn2_w_up[l].astype(BF16),
                      ffn2_w_down[l].astype(BF16), row(final_norm), final_norm=last)
    return h.reshape(b, s, d).astype(x.dtype)
```

```python
import functools
import math

import jax
import jax.numpy as jnp
from jax import lax
from jax.experimental import pallas as pl
from jax.experimental.pallas import tpu as pltpu

F32 = jnp.float32
BF16 = jnp.bfloat16

EPS = 1e-6
N_HEADS = 8
HEAD_DIM = 64
ATTN_WIDTH = N_HEADS * HEAD_DIM
KV_LATENT = 128
N_HEADS_IDX = 8
IDX_DIM = 64
TOPK_MAX = 256
CONV_WIDTH = 512
CONV_K = 3
N_BUCKETS = 32
MAX_DISTANCE = 128
IN_SIZES = (ATTN_WIDTH, KV_LATENT, N_HEADS_IDX * IDX_DIM, IDX_DIM, N_HEADS_IDX,
            CONV_WIDTH, CONV_WIDTH, CONV_WIDTH)

SUBLANES = 8
LANES = 128

FFN_TOKENS = 512
FFN_CHUNK = 256
PROJ_TOKENS = 512
Q_BLOCK = 256
BISECT_CHUNK = 4
MAX_BISECT = 64
DSA_BATCH = 2
SWEEP_ACCS = 2
BRACKET_STOP = 0
MAX_PEEL = 4096
LOG2E = math.log2(math.e)
VMEM_LIMIT = 56 * 1024 * 1024


def _rms(x, g):
    return x * lax.rsqrt(jnp.mean(x * x, axis=-1, keepdims=True) + EPS) * g


def _const_spec(shape):
    nd = len(shape)
    return pl.BlockSpec(shape, lambda *_: (0,) * nd, pipeline_mode=pl.Buffered(1))


def _swiglu_into(acc_ref, xn, wg_ref, wu_ref, wd_ref):
    d_ff = wg_ref.shape[1]
    for c in range(d_ff // FFN_CHUNK):
        sl = slice(c * FFN_CHUNK, (c + 1) * FFN_CHUNK)
        g = jnp.dot(xn, wg_ref[:, sl], preferred_element_type=F32)
        u = jnp.dot(xn, wu_ref[:, sl], preferred_element_type=F32)
        a = (g * jax.nn.sigmoid(g) * u).astype(BF16)
        d = jnp.dot(a, wd_ref[sl, :], preferred_element_type=F32)
        if c == 0:
            acc_ref[...] = d
        else:
            acc_ref[...] += d


def _ffn1_kernel(x_ref, g_ref, wg_ref, wu_ref, wd_ref, o_ref, acc_ref):
    x = x_ref[...]
    xn = _rms(x, g_ref[...]).astype(BF16)
    _swiglu_into(acc_ref, xn, wg_ref, wu_ref, wd_ref)
    o_ref[...] = x + 0.5 * acc_ref[...]


def _out_ffn2_kernel(h_ref, a_ref, c_ref, woa_ref, woc_ref, g_ref, wg_ref, wu_ref, wd_ref,
                     gf_ref, o_ref, acc_ref, *, final_norm):
    h = (h_ref[...]
         + jnp.dot(a_ref[...], woa_ref[...], preferred_element_type=F32)
         + jnp.dot(c_ref[...], woc_ref[...], preferred_element_type=F32))
    xn = _rms(h, g_ref[...]).astype(BF16)
    _swiglu_into(acc_ref, xn, wg_ref, wu_ref, wd_ref)
    h = h + 0.5 * acc_ref[...]
    o_ref[...] = _rms(h, gf_ref[...]) if final_norm else h


def _ffn1(x2, g, wg, wu, wd):
    n, d = x2.shape
    tile = pl.BlockSpec((FFN_TOKENS, d), lambda i: (i, 0))
    return pl.pallas_call(
        _ffn1_kernel,
        grid=(n // FFN_TOKENS,),
        in_specs=[tile, _const_spec(g.shape), _const_spec(wg.shape), _const_spec(wu.shape),
                  _const_spec(wd.shape)],
        out_specs=tile,
        out_shape=jax.ShapeDtypeStruct((n, d), F32),
        scratch_shapes=[pltpu.VMEM((FFN_TOKENS, d), F32)],
        compiler_params=pltpu.CompilerParams(dimension_semantics=("arbitrary",),
                                             vmem_limit_bytes=VMEM_LIMIT),
        name="ffn1",
    )(x2, g, wg, wu, wd)


def _out_ffn2(h2, a2, c2, woa, woc, g, wg, wu, wd, gf, final_norm):
    n, d = h2.shape
    tile = pl.BlockSpec((FFN_TOKENS, d), lambda i: (i, 0))
    mix = pl.BlockSpec((FFN_TOKENS, a2.shape[1]), lambda i: (i, 0))
    return pl.pallas_call(
        functools.partial(_out_ffn2_kernel, final_norm=final_norm),
        grid=(n // FFN_TOKENS,),
        in_specs=[tile, mix, mix, _const_spec(woa.shape), _const_spec(woc.shape),
                  _const_spec(g.shape), _const_spec(wg.shape), _const_spec(wu.shape),
                  _const_spec(wd.shape), _const_spec(gf.shape)],
        out_specs=tile,
        out_shape=jax.ShapeDtypeStruct((n, d), F32),
        scratch_shapes=[pltpu.VMEM((FFN_TOKENS, d), F32)],
        compiler_params=pltpu.CompilerParams(dimension_semantics=("arbitrary",),
                                             vmem_limit_bytes=VMEM_LIMIT),
        name="out_ffn2",
    )(h2, a2, c2, woa, woc, g, wg, wu, wd, gf)


_C_KV = 0
_C_KI = _C_KV + KV_LATENT
_C_GB = _C_KI + LANES
_C_GC = _C_GB + CONV_WIDTH
_C_X = _C_GC + CONV_WIDTH
_C_END = _C_X + CONV_WIDTH
_R_Q = 0
_R_KV = _R_Q + ATTN_WIDTH
_R_QI = _R_KV + KV_LATENT
_R_WI = _R_QI + N_HEADS_IDX * IDX_DIM
_R_END = _R_WI + 2 * SUBLANES
KVT_ROWS = KV_LATENT + 2 * SUBLANES


def _proj_kernel(h_ref, gm_ref, wcat_ref, wt_ref, wukt_ref, gkv_ref, gkvc_ref, cw_ref, gco_ref,
                 qat_ref, ckv_ref, ckvt_ref, ki_ref, qit_ref, wit_ref, convn_ref, carry_ref):
    tp = h_ref.shape[0]

    @pl.when(pl.program_id(1) == 0)
    def _():
        carry_ref[...] = jnp.zeros_like(carry_ref)

    un = _rms(h_ref[...], gm_ref[...]).astype(BF16)
    p = jnp.dot(un, wcat_ref[...], preferred_element_type=F32)
    ckv_ref[...] = _rms(p[:, _C_KV:_C_KI], gkv_ref[...]).astype(BF16)
    ki_ref[...] = p[:, _C_KI:_C_KI + IDX_DIM].astype(BF16)

    v = p[:, _C_GC:_C_X] * p[:, _C_X:_C_END]
    row = lax.broadcasted_iota(jnp.int32, v.shape, 0)
    prev1 = jnp.broadcast_to(carry_ref[SUBLANES - 1:SUBLANES, :], v.shape)
    prev2 = jnp.broadcast_to(carry_ref[SUBLANES - 2:SUBLANES - 1, :], v.shape)
    v1 = jnp.where(row == 0, prev1, pltpu.roll(v, 1, 0))
    v2 = jnp.where(row == 0, prev2, jnp.where(row == 1, prev1, pltpu.roll(v, 2, 0)))
    carry_ref[...] = v[tp - SUBLANES:, :]
    y = cw_ref[0:1, :] * v2 + cw_ref[1:2, :] * v1 + cw_ref[2:3, :] * v
    conv = p[:, _C_GB:_C_GC] * y
    convn_ref[...] = _rms(conv, gco_ref[...]).astype(BF16)

    pt = lax.dot_general(wt_ref[...], un, (((1,), (1,)), ((), ())),
                         preferred_element_type=F32)
    qt = pt[_R_Q:_R_KV, :].astype(BF16)
    for h in range(N_HEADS):
        qah = jnp.dot(wukt_ref[h], qt[h * HEAD_DIM:(h + 1) * HEAD_DIM, :],
                      preferred_element_type=F32)
        qat_ref[h * KV_LATENT:(h + 1) * KV_LATENT, :] = (qah * (HEAD_DIM ** -0.5 * LOG2E)).astype(BF16)
    kvt = pt[_R_KV:_R_QI, :]
    kvt = kvt * lax.rsqrt(jnp.mean(kvt * kvt, axis=0, keepdims=True) + EPS) * gkvc_ref[...]
    tk = ckvt_ref.shape[2]
    tail = jnp.concatenate([jnp.ones((SUBLANES, tk), F32), jnp.zeros((SUBLANES, tk), F32)], axis=0)
    for kb in range(tp // tk):
        blk = jnp.concatenate([kvt[:, kb * tk:(kb + 1) * tk], tail], axis=0)
        ckvt_ref[kb] = blk.astype(BF16)
    qit_ref[...] = pt[_R_QI:_R_WI, :].astype(BF16)
    wit_ref[...] = pt[_R_WI:_R_WI + N_HEADS_IDX, :] * ((N_HEADS_IDX * IDX_DIM) ** -0.5)


def _proj(h1, gm, wcat, wt, wukt, gkv, cw, gco):
    b, s, d = h1.shape
    tp = PROJ_TOKENS
    nq = N_HEADS_IDX * IDX_DIM
    nqa = N_HEADS * KV_LATENT
    gkvc = gkv.reshape(KV_LATENT, 1)
    tok = lambda w: pl.BlockSpec((None, tp, w), lambda bi, ti: (bi, ti, 0))
    tokt = lambda r: pl.BlockSpec((None, r, tp), lambda bi, ti: (bi, 0, ti))
    out_shape = (
        jax.ShapeDtypeStruct((b, nqa, s), BF16),
        jax.ShapeDtypeStruct((b, s, KV_LATENT), BF16),
        jax.ShapeDtypeStruct((b, s // Q_BLOCK, KVT_ROWS, Q_BLOCK), BF16),
        jax.ShapeDtypeStruct((b, s, IDX_DIM), BF16),
        jax.ShapeDtypeStruct((b, nq, s), BF16),
        jax.ShapeDtypeStruct((b, N_HEADS_IDX, s), F32),
        jax.ShapeDtypeStruct((b, s, CONV_WIDTH), BF16),
    )
    out_specs = (
        tokt(nqa), tok(KV_LATENT),
        pl.BlockSpec((None, tp // Q_BLOCK, KVT_ROWS, Q_BLOCK), lambda bi, ti: (bi, ti, 0, 0)),
        tok(IDX_DIM), tokt(nq), tokt(N_HEADS_IDX), tok(CONV_WIDTH),
    )
    return pl.pallas_call(
        _proj_kernel,
        grid=(b, s // tp),
        in_specs=[tok(d), _const_spec(gm.shape), _const_spec(wcat.shape), _const_spec(wt.shape),
                  _const_spec(wukt.shape), _const_spec(gkv.shape), _const_spec(gkvc.shape),
                  _const_spec(cw.shape), _const_spec(gco.shape)],
        out_specs=out_specs,
        out_shape=out_shape,
        scratch_shapes=[pltpu.VMEM((SUBLANES, CONV_WIDTH), F32)],
        compiler_params=pltpu.CompilerParams(dimension_semantics=("arbitrary", "arbitrary"),
                                             vmem_limit_bytes=VMEM_LIMIT),
        name="proj",
    )(h1, gm, wcat, wt, wukt, gkv, gkvc, cw, gco)


def _bucket_starts():
    e = N_BUCKETS // 2
    n = N_BUCKETS - e
    assert MAX_DISTANCE % e == 0
    r = MAX_DISTANCE // e
    starts, d = [], e
    for k in range(1, n):
        while d ** n < e ** n * r ** k:
            d += 1
        starts.append(d)
    return starts


def _rel_bucket(dist):
    max_exact = N_BUCKETS // 2
    large = max_exact
    for start in _bucket_starts():
        large = large + (dist >= start).astype(jnp.int32)
    return jnp.where(dist < max_exact, dist, large)


def _bias_kernel(rb_ref, o_ref):
    h = pl.program_id(0)
    tk, tq = o_ref.shape
    s = lax.broadcasted_iota(jnp.int32, (tk, tq), 0)
    t = lax.broadcasted_iota(jnp.int32, (tk, tq), 1)
    bucket = _rel_bucket(jnp.maximum(t - s + tq, 0))
    val = jnp.zeros((tk, tq), F32)
    for b in range(N_BUCKETS):
        val = jnp.where(bucket == b, rb_ref[b, h], val)
    o_ref[...] = (val - rb_ref[N_BUCKETS - 1, h]) * LOG2E


def _bias_tiles(rel_bias, tq):
    return pl.pallas_call(
        _bias_kernel,
        grid=(N_HEADS,),
        in_specs=[pl.BlockSpec(memory_space=pltpu.SMEM)],
        out_specs=pl.BlockSpec((None, 2 * tq, tq), lambda h: (h, 0, 0)),
        out_shape=jax.ShapeDtypeStruct((N_HEADS, 2 * tq, tq), F32),
        compiler_params=pltpu.CompilerParams(dimension_semantics=("arbitrary",)),
        name="bias_tiles",
    )(rel_bias)


def _dsa_kernel(qit_ref, wit_ref, qat_ref, ki_ref, ckv_ref, ckvt_ref, bias_ref, wuvt_ref, gout_ref,
                o_ref, sc_ref, mask_ref, acc_ref, m_ref, *, n_keep):
    nbat, tq = o_ref.shape[0], o_ref.shape[1]
    tw = nbat * tq
    i = pl.program_id(1)
    nblk = i + 1
    q0 = i * tq
    kf = float(n_keep)
    key_in_blk = lax.broadcasted_iota(jnp.int32, (tq, tq), 0)
    qry_in_blk = lax.broadcasted_iota(jnp.int32, (tq, tq), 1)
    ind = lambda mask: jnp.where(mask, 1.0, 0.0)
    batches = lambda f: jnp.concatenate([f(g) for g in range(nbat)], axis=1)
    part = lambda a: a.reshape(tq // (SWEEP_ACCS * SUBLANES), SWEEP_ACCS, SUBLANES, tw)
    fold = lambda a: part(a).sum(axis=0)
    fold_max = lambda a: part(a).max(axis=0)
    total = lambda a: a.sum(axis=0).sum(axis=0, keepdims=True)
    total_max = lambda a: a.max(axis=0).max(axis=0, keepdims=True)
    zeros8 = jnp.zeros((SWEEP_ACCS, SUBLANES, tw), F32)
    ninf8 = jnp.full((SWEEP_ACCS, SUBLANES, tw), -jnp.inf, F32)

    def score_body(j, carry):
        mn, mx = carry
        k0 = pl.multiple_of(j * tq, tq)
        causal1 = (k0 + key_in_blk) <= (q0 + qry_in_blk)

        def one(g):
            kib = ki_ref[g, pl.ds(k0, tq), :]
            acc = jnp.zeros((tq, tq), F32)
            for h in range(N_HEADS_IDX):
                d = jnp.dot(kib, qit_ref[g, h * IDX_DIM:(h + 1) * IDX_DIM, :],
                            preferred_element_type=F32)
                acc = acc + jnp.maximum(d, 0.0) * wit_ref[g, h:h + 1, :]
            return acc
        acc = batches(one)
        causal = batches(lambda g: causal1)
        masked = jnp.where(causal, acc, -jnp.inf)
        sc_ref[j] = masked
        mx = jnp.maximum(mx, jnp.max(masked, axis=0, keepdims=True))
        mn = jnp.minimum(mn, jnp.min(jnp.where(causal, acc, jnp.inf), axis=0, keepdims=True))
        return mn, mx

    def over_blocks(body, init):
        def pair(pi, c):
            return body(2 * pi + 1, body(2 * pi, c))
        c = lax.fori_loop(0, lax.shift_right_logical(nblk, 1), pair, init)
        return lax.cond((nblk & 1) == 1, lambda c: body(nblk - 1, c), lambda c: c, c)

    mn, mx = over_blocks(score_body,
                         (jnp.full((1, tw), jnp.inf, F32), jnp.full((1, tw), -jnp.inf, F32)))

    def count(bound, strict=False):
        def one(j, g):
            cols = slice(g * tq, (g + 1) * tq)
            blk = sc_ref[j, :, cols]
            hit = blk > bound[:, cols] if strict else blk >= bound[:, cols]
            return ind(hit).reshape(tq // (SWEEP_ACCS * SUBLANES), SWEEP_ACCS, SUBLANES, tq).sum(axis=0)
        a8 = lax.fori_loop(
            0, nblk,
            lambda j, a: a + jnp.concatenate([one(j, g) for g in range(nbat)], axis=-1), zeros8)
        return total(a8)

    def any_set(flag):
        return jnp.max(flag) > 0.0

    def init_body(j, st):
        a, b, c = st
        blk = sc_ref[j]
        return a + fold(ind(blk >= 0.0)), b + fold(ind(blk > 0.0)), c + fold(ind(blk >= mx))
    c_ge0, c_gt0, c_mx = [total(a)
                          for a in lax.fori_loop(0, nblk, init_body, (zeros8, zeros8, zeros8))]
    ncausal = batches(lambda g: (q0 + lax.broadcasted_iota(jnp.int32, (1, tq), 1) + 1).astype(F32))
    small = ncausal <= kf
    top_tied = c_mx >= kf
    neg = c_ge0 < kf
    zero_tie = (c_gt0 < kf) & ~neg
    lo = jnp.where(small | neg, mn, 0.0)
    lo = jnp.where(top_tied & ~small, mx, lo)
    clo = jnp.where(small | neg, ncausal, c_ge0)
    clo = jnp.where(top_tied & ~small, c_mx, clo)
    hi = jnp.where(neg, 0.0, mx)
    chi = jnp.where(neg, c_ge0, c_mx)
    searching = ~small & ~top_tied & ~zero_tie

    def bis_cond(st):
        return (st[0] < MAX_BISECT) & st[1]

    def bis_body(st):
        it, _, lo, hi, clo, chi, active = st
        for _ in range(BISECT_CHUNK):
            mid = lo + (hi - lo) * 0.5
            live = (active > 0.0) & (mid > lo) & (mid < hi)
            c = count(mid)
            up = live & (c >= kf)
            dn = live & (c < kf)
            lo = jnp.where(up, mid, lo)
            clo = jnp.where(up, c, clo)
            hi = jnp.where(dn, mid, hi)
            chi = jnp.where(dn, c, chi)
            active = ind(live & (clo > kf) & (clo - chi > BRACKET_STOP))
        return it + 1, any_set(active), lo, hi, clo, chi, active

    active = ind(searching & (clo > kf) & (clo - chi > BRACKET_STOP))
    st = lax.while_loop(bis_cond, bis_body,
                        (jnp.int32(0), any_set(active), lo, hi, clo, chi, active))
    _, _, lo, hi, clo, chi, _ = st

    unres = ind(searching & (clo > kf))

    def peel_cond(st):
        return (st[0] < MAX_PEEL) & st[1]

    def peel_body(st):
        it, _, lo, hi, clo, chi, cand, unres = st

        def body(j, s2):
            a8, m8 = s2
            blk = sc_ref[j]
            ge = blk >= cand
            below = jnp.where((blk >= lo) & ~ge, blk, -jnp.inf)
            return a8 + fold(ind(ge)), jnp.maximum(m8, fold_max(below))
        a8, m8 = lax.fori_loop(0, nblk, body, (zeros8, ninf8))
        c = total(a8)
        nxt = total_max(m8)
        on = unres > 0.0
        found = on & (c >= kf)
        down = on & (c < kf)
        lo = jnp.where(found, cand, lo)
        clo = jnp.where(found, c, clo)
        hi = jnp.where(down, cand, hi)
        chi = jnp.where(down, c, chi)
        cand = jnp.where(down, nxt, cand)
        return it + 1, any_set(ind(down)), lo, hi, clo, chi, cand, ind(down)

    st = lax.while_loop(peel_cond, peel_body,
                        (jnp.int32(0), any_set(unres), lo, hi, clo, chi, hi, unres))
    _, _, lo, hi, clo, chi, _, _ = st

    tied = any_set(ind(~small & (clo > kf)))

    @pl.when(jnp.logical_not(tied))
    def _():
        def body(j, c):
            mask_ref[j] = jnp.where(sc_ref[j] >= lo, 0.0, -jnp.inf)
            return c
        lax.fori_loop(0, nblk, body, 0)

    @pl.when(tied)
    def _():
        need = kf - count(lo, strict=True)
        earlier = ind(qry_in_blk < key_in_blk).astype(BF16)

        def body(j, seen):
            blk = sc_ref[j]
            eq = ind(blk == lo)
            rank = seen + jnp.dot(earlier, eq.astype(BF16), preferred_element_type=F32)
            keep = (blk > lo) | ((blk == lo) & (rank < need))
            mask_ref[j] = jnp.where(keep, 0.0, -jnp.inf)
            return seen + total(fold(eq))
        over_blocks(body, jnp.zeros((1, tw), F32))

    m_ref[...] = jnp.full(m_ref.shape, -jnp.inf, F32)
    acc_ref[...] = jnp.zeros(acc_ref.shape, F32)

    def attend(j, nkb, bias):
        tk = nkb * tq
        k0 = pl.multiple_of(j * tq, tq)
        heads = lambda f: jnp.concatenate([f(h) for h in range(N_HEADS)], axis=1)
        for g in range(nbat):
            cols = slice(g * tq, (g + 1) * tq)
            kv = ckv_ref[g, pl.ds(k0, tk), :]
            kvt1 = jnp.concatenate([ckvt_ref[g, j + b] for b in range(nkb)], axis=1)
            mask = jnp.concatenate([mask_ref[j + b, :, cols] for b in range(nkb)], axis=0)
            qa = heads(lambda h: qat_ref[g, h * KV_LATENT:(h + 1) * KV_LATENT, :])
            s = jnp.dot(kv, qa, preferred_element_type=F32)
            if bias is not None:
                s = s + heads(bias)
            s = s + heads(lambda h: mask)
            m_prev = m_ref[g]
            m_new = jnp.maximum(m_prev, jnp.max(s, axis=0, keepdims=True))
            m_safe = jnp.where(m_new == -jnp.inf, 0.0, m_new)
            alpha = jnp.exp2(m_prev - m_safe)
            p = jnp.exp2(s - m_safe).astype(BF16)
            m_ref[g] = m_new
            pv = jnp.dot(kvt1, p, preferred_element_type=F32)
            acc_ref[g] = alpha * acc_ref[g] + pv

    nfar = jnp.maximum(i - 1, 0)

    def far_pair(pi, c):
        attend(2 * pi, 2, None)
        return c
    lax.fori_loop(0, lax.shift_right_logical(nfar, 1), far_pair, 0)

    @pl.when((nfar & 1) == 1)
    def _():
        attend(nfar - 1, 1, None)

    @pl.when(i >= 1)
    def _():
        attend(i - 1, 2, lambda h: bias_ref[h])

    @pl.when(i == 0)
    def _():
        attend(0, 1, lambda h: bias_ref[h, tq:, :])

    for g in range(nbat):
        outs = []
        for h in range(N_HEADS):
            a = acc_ref[g, :, h * tq:(h + 1) * tq]
            ctx = (a[:KV_LATENT, :] / a[KV_LATENT:KV_LATENT + 1, :]).astype(BF16)
            outs.append(jnp.dot(wuvt_ref[h], ctx, preferred_element_type=F32))
        out = jnp.concatenate(outs, axis=0).T
        o_ref[g] = _rms(out, gout_ref[...]).astype(BF16)


def _dsa(qit, wit, qat, ki, ckv, ckvt, bias, wuvt, gout, n_keep):
    b, s, _ = ckv.shape
    tq = Q_BLOCK
    nb = s // tq
    nq = N_HEADS_IDX * IDX_DIM
    nbat = DSA_BATCH if b % DSA_BATCH == 0 else 1
    colblk = lambda r: pl.BlockSpec((nbat, r, tq), lambda bi, i: (bi, 0, i))
    return pl.pallas_call(
        functools.partial(_dsa_kernel, n_keep=n_keep),
        grid=(b // nbat, nb),
        in_specs=[
            colblk(nq), colblk(N_HEADS_IDX), colblk(N_HEADS * KV_LATENT),
            pl.BlockSpec((nbat, s, IDX_DIM), lambda bi, i: (bi, 0, 0)),
            pl.BlockSpec((nbat, s, KV_LATENT), lambda bi, i: (bi, 0, 0)),
            pl.BlockSpec((nbat, nb, KVT_ROWS, tq), lambda bi, i: (bi, 0, 0, 0)),
            _const_spec(bias.shape), _const_spec(wuvt.shape), _const_spec(gout.shape),
        ],
        out_specs=pl.BlockSpec((nbat, tq, ATTN_WIDTH), lambda bi, i: (bi, i, 0)),
        out_shape=jax.ShapeDtypeStruct((b, s, ATTN_WIDTH), BF16),
        scratch_shapes=[
            pltpu.VMEM((nb, tq, nbat * tq), F32),
            pltpu.VMEM((nb, tq, nbat * tq), F32),
            pltpu.VMEM((nbat, KVT_ROWS, N_HEADS * tq), F32),
            pltpu.VMEM((nbat, 1, N_HEADS * tq), F32),
        ],
        compiler_params=pltpu.CompilerParams(dimension_semantics=("arbitrary", "arbitrary"),
                                             vmem_limit_bytes=VMEM_LIMIT),
        name="dsa",
    )(qit, wit, qat, ki, ckv, ckvt, bias, wuvt, gout)


def kernel(x, ffn1_norm, ffn1_w_gate, ffn1_w_up, ffn1_w_down, mix_norm, w_in, kv_norm, w_uk, w_uv, rel_bias, conv_w, attn_out_norm, conv_out_norm, w_out, ffn2_norm, ffn2_w_gate, ffn2_w_up, ffn2_w_down, final_norm):
    b, s, d = x.shape
    depth = ffn1_norm.shape[0]
    n_keep = min(TOPK_MAX, s // 4)
    assert s % Q_BLOCK == 0 and s % PROJ_TOKENS == 0 and (b * s) % FFN_TOKENS == 0
    assert Q_BLOCK >= MAX_DISTANCE and n_keep <= Q_BLOCK
    row = lambda v: v.reshape(1, -1).astype(F32)
    offs = [0]
    for w in IN_SIZES:
        offs.append(offs[-1] + w)

    bias = _bias_tiles(rel_bias.astype(F32), Q_BLOCK)
    h = x.astype(F32).reshape(b * s, d)
    for l in range(depth):
        h = _ffn1(h, row(ffn1_norm[l]), ffn1_w_gate[l].astype(BF16), ffn1_w_up[l].astype(BF16),
                  ffn1_w_down[l].astype(BF16))

        wi = w_in[l]
        col = lambda k: wi[:, offs[k]:offs[k + 1]]
        wcat = jnp.concatenate(
            [col(1), col(3), jnp.zeros((d, LANES - IDX_DIM), wi.dtype), col(5), col(6), col(7)],
            axis=1).astype(BF16)
        wt = jnp.concatenate(
            [col(0).T, col(1).T, col(2).T, col(4).T,
             jnp.zeros((_R_END - _R_WI - N_HEADS_IDX, d), wi.dtype)], axis=0).astype(BF16)
        wukt = jnp.transpose(w_uk[l], (1, 0, 2)).astype(BF16)
        wuvt = jnp.transpose(w_uv[l], (1, 2, 0)).astype(BF16)

        qat, ckv, ckvt, ki, qit, wit, convn = _proj(
            h.reshape(b, s, d), row(mix_norm[l]), wcat, wt, wukt, row(kv_norm[l]),
            conv_w[l].astype(F32), row(conv_out_norm[l]))
        attn = _dsa(qit, wit, qat, ki, ckv, ckvt, bias, wuvt, row(attn_out_norm[l]), n_keep)

        last = l == depth - 1
        h = _out_ffn2(h, attn.reshape(b * s, ATTN_WIDTH), convn.reshape(b * s, CONV_WIDTH),
                      w_out[l][:ATTN_WIDTH].astype(BF16), w_out[l][ATTN_WIDTH:].astype(BF16),
                      row(ffn2_norm[l]), ffn2_w_gate[l].astype(BF16), ffn2_w_up[l].astype(BF16),
                      ffn2_w_down[l].astype(BF16), row(final_norm), final_norm=last)
    return h.reshape(b, s, d).astype(x.dtype)
```

```python
import functools
import math

import jax
import jax.numpy as jnp
from jax import lax
from jax.experimental import pallas as pl
from jax.experimental.pallas import tpu as pltpu

F32 = jnp.float32
BF16 = jnp.bfloat16

EPS = 1e-6
N_HEADS = 8
HEAD_DIM = 64
ATTN_WIDTH = N_HEADS * HEAD_DIM
KV_LATENT = 128
N_HEADS_IDX = 8
IDX_DIM = 64
TOPK_MAX = 256
CONV_WIDTH = 512
CONV_K = 3
N_BUCKETS = 32
MAX_DISTANCE = 128
IN_SIZES = (ATTN_WIDTH, KV_LATENT, N_HEADS_IDX * IDX_DIM, IDX_DIM, N_HEADS_IDX,
            CONV_WIDTH, CONV_WIDTH, CONV_WIDTH)

SUBLANES = 8
LANES = 128

FFN_TOKENS = 512
FFN_CHUNK = 256
PROJ_TOKENS = 512
Q_BLOCK = 256
BISECT_CHUNK = 4
MAX_BISECT = 64
DSA_BATCH = 4
SWEEP_ACCS = 2
BRACKET_STOP = 0
MAX_PEEL = 4096
LOG2E = math.log2(math.e)
VMEM_LIMIT = 56 * 1024 * 1024


def _rms(x, g):
    return x * lax.rsqrt(jnp.mean(x * x, axis=-1, keepdims=True) + EPS) * g


def _const_spec(shape):
    nd = len(shape)
    return pl.BlockSpec(shape, lambda *_: (0,) * nd, pipeline_mode=pl.Buffered(1))


def _swiglu_into(acc_ref, xn, wg_ref, wu_ref, wd_ref):
    d_ff = wg_ref.shape[1]
    for c in range(d_ff // FFN_CHUNK):
        sl = slice(c * FFN_CHUNK, (c + 1) * FFN_CHUNK)
        g = jnp.dot(xn, wg_ref[:, sl], preferred_element_type=F32)
        u = jnp.dot(xn, wu_ref[:, sl], preferred_element_type=F32)
        a = (g * jax.nn.sigmoid(g) * u).astype(BF16)
        d = jnp.dot(a, wd_ref[sl, :], preferred_element_type=F32)
        if c == 0:
            acc_ref[...] = d
        else:
            acc_ref[...] += d


def _ffn1_kernel(x_ref, g_ref, wg_ref, wu_ref, wd_ref, o_ref, acc_ref):
    x = x_ref[...]
    xn = _rms(x, g_ref[...]).astype(BF16)
    _swiglu_into(acc_ref, xn, wg_ref, wu_ref, wd_ref)
    o_ref[...] = x + 0.5 * acc_ref[...]


def _out_ffn2_kernel(h_ref, a_ref, c_ref, woa_ref, woc_ref, g_ref, wg_ref, wu_ref, wd_ref,
                     gf_ref, o_ref, acc_ref, *, final_norm):
    h = (h_ref[...]
         + jnp.dot(a_ref[...], woa_ref[...], preferred_element_type=F32)
         + jnp.dot(c_ref[...], woc_ref[...], preferred_element_type=F32))
    xn = _rms(h, g_ref[...]).astype(BF16)
    _swiglu_into(acc_ref, xn, wg_ref, wu_ref, wd_ref)
    h = h + 0.5 * acc_ref[...]
    o_ref[...] = _rms(h, gf_ref[...]) if final_norm else h


def _ffn1(x2, g, wg, wu, wd):
    n, d = x2.shape
    tile = pl.BlockSpec((FFN_TOKENS, d), lambda i: (i, 0))
    return pl.pallas_call(
        _ffn1_kernel,
        grid=(n // FFN_TOKENS,),
        in_specs=[tile, _const_spec(g.shape), _const_spec(wg.shape), _const_spec(wu.shape),
                  _const_spec(wd.shape)],
        out_specs=tile,
        out_shape=jax.ShapeDtypeStruct((n, d), F32),
        scratch_shapes=[pltpu.VMEM((FFN_TOKENS, d), F32)],
        compiler_params=pltpu.CompilerParams(dimension_semantics=("arbitrary",),
                                             vmem_limit_bytes=VMEM_LIMIT),
        name="ffn1",
    )(x2, g, wg, wu, wd)


def _out_ffn2(h2, a2, c2, woa, woc, g, wg, wu, wd, gf, final_norm):
    n, d = h2.shape
    tile = pl.BlockSpec((FFN_TOKENS, d), lambda i: (i, 0))
    mix = pl.BlockSpec((FFN_TOKENS, a2.shape[1]), lambda i: (i, 0))
    return pl.pallas_call(
        functools.partial(_out_ffn2_kernel, final_norm=final_norm),
        grid=(n // FFN_TOKENS,),
        in_specs=[tile, mix, mix, _const_spec(woa.shape), _const_spec(woc.shape),
                  _const_spec(g.shape), _const_spec(wg.shape), _const_spec(wu.shape),
                  _const_spec(wd.shape), _const_spec(gf.shape)],
        out_specs=tile,
        out_shape=jax.ShapeDtypeStruct((n, d), F32),
        scratch_shapes=[pltpu.VMEM((FFN_TOKENS, d), F32)],
        compiler_params=pltpu.CompilerParams(dimension_semantics=("arbitrary",),
                                             vmem_limit_bytes=VMEM_LIMIT),
        name="out_ffn2",
    )(h2, a2, c2, woa, woc, g, wg, wu, wd, gf)


_C_KV = 0
_C_KI = _C_KV + KV_LATENT
_C_GB = _C_KI + LANES
_C_GC = _C_GB + CONV_WIDTH
_C_X = _C_GC + CONV_WIDTH
_C_END = _C_X + CONV_WIDTH
_R_Q = 0
_R_KV = _R_Q + ATTN_WIDTH
_R_QI = _R_KV + KV_LATENT
_R_WI = _R_QI + N_HEADS_IDX * IDX_DIM
_R_END = _R_WI + 2 * SUBLANES
KVT_ROWS = KV_LATENT + 2 * SUBLANES


def _proj_kernel(h_ref, gm_ref, wcat_ref, wt_ref, wukt_ref, gkv_ref, gkvc_ref, cw_ref, gco_ref,
                 qat_ref, ckv_ref, ckvt_ref, ki_ref, qit_ref, wit_ref, convn_ref, carry_ref):
    tp = h_ref.shape[0]

    @pl.when(pl.program_id(1) == 0)
    def _():
        carry_ref[...] = jnp.zeros_like(carry_ref)

    un = _rms(h_ref[...], gm_ref[...]).astype(BF16)
    p = jnp.dot(un, wcat_ref[...], preferred_element_type=F32)
    ckv_ref[...] = _rms(p[:, _C_KV:_C_KI], gkv_ref[...]).astype(BF16)
    ki_ref[...] = p[:, _C_KI:_C_KI + IDX_DIM].astype(BF16)

    v = p[:, _C_GC:_C_X] * p[:, _C_X:_C_END]
    row = lax.broadcasted_iota(jnp.int32, v.shape, 0)
    prev1 = jnp.broadcast_to(carry_ref[SUBLANES - 1:SUBLANES, :], v.shape)
    prev2 = jnp.broadcast_to(carry_ref[SUBLANES - 2:SUBLANES - 1, :], v.shape)
    v1 = jnp.where(row == 0, prev1, pltpu.roll(v, 1, 0))
    v2 = jnp.where(row == 0, prev2, jnp.where(row == 1, prev1, pltpu.roll(v, 2, 0)))
    carry_ref[...] = v[tp - SUBLANES:, :]
    y = cw_ref[0:1, :] * v2 + cw_ref[1:2, :] * v1 + cw_ref[2:3, :] * v
    conv = p[:, _C_GB:_C_GC] * y
    convn_ref[...] = _rms(conv, gco_ref[...]).astype(BF16)

    pt = lax.dot_general(wt_ref[...], un, (((1,), (1,)), ((), ())),
                         preferred_element_type=F32)
    qt = pt[_R_Q:_R_KV, :].astype(BF16)
    for h in range(N_HEADS):
        qah = jnp.dot(wukt_ref[h], qt[h * HEAD_DIM:(h + 1) * HEAD_DIM, :],
                      preferred_element_type=F32)
        qat_ref[h * KV_LATENT:(h + 1) * KV_LATENT, :] = (qah * (HEAD_DIM ** -0.5 * LOG2E)).astype(BF16)
    kvt = pt[_R_KV:_R_QI, :]
    kvt = kvt * lax.rsqrt(jnp.mean(kvt * kvt, axis=0, keepdims=True) + EPS) * gkvc_ref[...]
    tk = ckvt_ref.shape[2]
    tail = jnp.concatenate([jnp.ones((SUBLANES, tk), F32), jnp.zeros((SUBLANES, tk), F32)], axis=0)
    for kb in range(tp // tk):
        blk = jnp.concatenate([kvt[:, kb * tk:(kb + 1) * tk], tail], axis=0)
        ckvt_ref[kb] = blk.astype(BF16)
    qit_ref[...] = pt[_R_QI:_R_WI, :].astype(BF16)
    wit_ref[...] = pt[_R_WI:_R_WI + N_HEADS_IDX, :] * ((N_HEADS_IDX * IDX_DIM) ** -0.5)


def _proj(h1, gm, wcat, wt, wukt, gkv, cw, gco):
    b, s, d = h1.shape
    tp = PROJ_TOKENS
    nq = N_HEADS_IDX * IDX_DIM
    nqa = N_HEADS * KV_LATENT
    gkvc = gkv.reshape(KV_LATENT, 1)
    tok = lambda w: pl.BlockSpec((None, tp, w), lambda bi, ti: (bi, ti, 0))
    tokt = lambda r: pl.BlockSpec((None, r, tp), lambda bi, ti: (bi, 0, ti))
    out_shape = (
        jax.ShapeDtypeStruct((b, nqa, s), BF16),
        jax.ShapeDtypeStruct((b, s, KV_LATENT), BF16),
        jax.ShapeDtypeStruct((b, s // Q_BLOCK, KVT_ROWS, Q_BLOCK), BF16),
        jax.ShapeDtypeStruct((b, s, IDX_DIM), BF16),
        jax.ShapeDtypeStruct((b, nq, s), BF16),
        jax.ShapeDtypeStruct((b, N_HEADS_IDX, s), F32),
        jax.ShapeDtypeStruct((b, s, CONV_WIDTH), BF16),
    )
    out_specs = (
        tokt(nqa), tok(KV_LATENT),
        pl.BlockSpec((None, tp // Q_BLOCK, KVT_ROWS, Q_BLOCK), lambda bi, ti: (bi, ti, 0, 0)),
        tok(IDX_DIM), tokt(nq), tokt(N_HEADS_IDX), tok(CONV_WIDTH),
    )
    return pl.pallas_call(
        _proj_kernel,
        grid=(b, s // tp),
        in_specs=[tok(d), _const_spec(gm.shape), _const_spec(wcat.shape), _const_spec(wt.shape),
                  _const_spec(wukt.shape), _const_spec(gkv.shape), _const_spec(gkvc.shape),
                  _const_spec(cw.shape), _const_spec(gco.shape)],
        out_specs=out_specs,
        out_shape=out_shape,
        scratch_shapes=[pltpu.VMEM((SUBLANES, CONV_WIDTH), F32)],
        compiler_params=pltpu.CompilerParams(dimension_semantics=("arbitrary", "arbitrary"),
                                             vmem_limit_bytes=VMEM_LIMIT),
        name="proj",
    )(h1, gm, wcat, wt, wukt, gkv, gkvc, cw, gco)


def _bucket_starts():
    e = N_BUCKETS // 2
    n = N_BUCKETS - e
    assert MAX_DISTANCE % e == 0
    r = MAX_DISTANCE // e
    starts, d = [], e
    for k in range(1, n):
        while d ** n < e ** n * r ** k:
            d += 1
        starts.append(d)
    return starts


def _rel_bucket(dist):
    max_exact = N_BUCKETS // 2
    large = max_exact
    for start in _bucket_starts():
        large = large + (dist >= start).astype(jnp.int32)
    return jnp.where(dist < max_exact, dist, large)


def _bias_kernel(rb_ref, o_ref):
    h = pl.program_id(0)
    tk, tq = o_ref.shape
    s = lax.broadcasted_iota(jnp.int32, (tk, tq), 0)
    t = lax.broadcasted_iota(jnp.int32, (tk, tq), 1)
    bucket = _rel_bucket(jnp.maximum(t - s + tq, 0))
    val = jnp.zeros((tk, tq), F32)
    for b in range(N_BUCKETS):
        val = jnp.where(bucket == b, rb_ref[b, h], val)
    o_ref[...] = (val - rb_ref[N_BUCKETS - 1, h]) * LOG2E


def _bias_tiles(rel_bias, tq):
    return pl.pallas_call(
        _bias_kernel,
        grid=(N_HEADS,),
        in_specs=[pl.BlockSpec(memory_space=pltpu.SMEM)],
        out_specs=pl.BlockSpec((None, 2 * tq, tq), lambda h: (h, 0, 0)),
        out_shape=jax.ShapeDtypeStruct((N_HEADS, 2 * tq, tq), F32),
        compiler_params=pltpu.CompilerParams(dimension_semantics=("arbitrary",)),
        name="bias_tiles",
    )(rel_bias)


def _dsa_kernel(qit_ref, wit_ref, qat_ref, ki_ref, ckv_ref, ckvt_ref, bias_ref, wuvt_ref, gout_ref,
                o_ref, sc_ref, mask_ref, acc_ref, m_ref, *, n_keep):
    nbat, tq = o_ref.shape[0], o_ref.shape[1]
    tw = nbat * tq
    i = pl.program_id(1)
    nblk = i + 1
    q0 = i * tq
    kf = float(n_keep)
    key_in_blk = lax.broadcasted_iota(jnp.int32, (tq, tq), 0)
    qry_in_blk = lax.broadcasted_iota(jnp.int32, (tq, tq), 1)
    ind = lambda mask: jnp.where(mask, 1.0, 0.0)
    batches = lambda f: jnp.concatenate([f(g) for g in range(nbat)], axis=1)
    part = lambda a: a.reshape(tq // (SWEEP_ACCS * SUBLANES), SWEEP_ACCS, SUBLANES, tw)
    fold = lambda a: part(a).sum(axis=0)
    fold_max = lambda a: part(a).max(axis=0)
    total = lambda a: a.sum(axis=0).sum(axis=0, keepdims=True)
    total_max = lambda a: a.max(axis=0).max(axis=0, keepdims=True)
    zeros8 = jnp.zeros((SWEEP_ACCS, SUBLANES, tw), F32)
    ninf8 = jnp.full((SWEEP_ACCS, SUBLANES, tw), -jnp.inf, F32)

    def score_body(j, carry):
        mn, mx = carry
        k0 = pl.multiple_of(j * tq, tq)
        causal1 = (k0 + key_in_blk) <= (q0 + qry_in_blk)

        def one(g):
            kib = ki_ref[g, pl.ds(k0, tq), :]
            acc = jnp.zeros((tq, tq), F32)
            for h in range(N_HEADS_IDX):
                d = jnp.dot(kib, qit_ref[g, h * IDX_DIM:(h + 1) * IDX_DIM, :],
                            preferred_element_type=F32)
                acc = acc + jnp.maximum(d, 0.0) * wit_ref[g, h:h + 1, :]
            return acc
        acc = batches(one)
        causal = batches(lambda g: causal1)
        masked = jnp.where(causal, acc, -jnp.inf)
        sc_ref[j] = masked
        mx = jnp.maximum(mx, jnp.max(masked, axis=0, keepdims=True))
        mn = jnp.minimum(mn, jnp.min(jnp.where(causal, acc, jnp.inf), axis=0, keepdims=True))
        return mn, mx

    def over_blocks(body, init):
        def pair(pi, c):
            return body(2 * pi + 1, body(2 * pi, c))
        c = lax.fori_loop(0, lax.shift_right_logical(nblk, 1), pair, init)
        return lax.cond((nblk & 1) == 1, lambda c: body(nblk - 1, c), lambda c: c, c)

    mn, mx = over_blocks(score_body,
                         (jnp.full((1, tw), jnp.inf, F32), jnp.full((1, tw), -jnp.inf, F32)))

    def count(bound, strict=False):
        def one(j, g):
            cols = slice(g * tq, (g + 1) * tq)
            blk = sc_ref[j, :, cols]
            hit = blk > bound[:, cols] if strict else blk >= bound[:, cols]
            return ind(hit).reshape(tq // (SWEEP_ACCS * SUBLANES), SWEEP_ACCS, SUBLANES, tq).sum(axis=0)
        a8 = lax.fori_loop(
            0, nblk,
            lambda j, a: a + jnp.concatenate([one(j, g) for g in range(nbat)], axis=-1), zeros8)
        return total(a8)

    def any_set(flag):
        return jnp.max(flag) > 0.0

    def init_body(j, st):
        a, b, c = st
        blk = sc_ref[j]
        return a + fold(ind(blk >= 0.0)), b + fold(ind(blk > 0.0)), c + fold(ind(blk >= mx))
    c_ge0, c_gt0, c_mx = [total(a)
                          for a in lax.fori_loop(0, nblk, init_body, (zeros8, zeros8, zeros8))]
    ncausal = batches(lambda g: (q0 + lax.broadcasted_iota(jnp.int32, (1, tq), 1) + 1).astype(F32))
    small = ncausal <= kf
    top_tied = c_mx >= kf
    neg = c_ge0 < kf
    zero_tie = (c_gt0 < kf) & ~neg
    lo = jnp.where(small | neg, mn, 0.0)
    lo = jnp.where(top_tied & ~small, mx, lo)
    clo = jnp.where(small | neg, ncausal, c_ge0)
    clo = jnp.where(top_tied & ~small, c_mx, clo)
    hi = jnp.where(neg, 0.0, mx)
    chi = jnp.where(neg, c_ge0, c_mx)
    searching = ~small & ~top_tied & ~zero_tie

    def bis_cond(st):
        return (st[0] < MAX_BISECT) & st[1]

    def bis_body(st):
        it, _, lo, hi, clo, chi, active = st
        for _ in range(BISECT_CHUNK):
            mid = lo + (hi - lo) * 0.5
            live = (active > 0.0) & (mid > lo) & (mid < hi)
            c = count(mid)
            up = live & (c >= kf)
            dn = live & (c < kf)
            lo = jnp.where(up, mid, lo)
            clo = jnp.where(up, c, clo)
            hi = jnp.where(dn, mid, hi)
            chi = jnp.where(dn, c, chi)
            active = ind(live & (clo > kf) & (clo - chi > BRACKET_STOP))
        return it + 1, any_set(active), lo, hi, clo, chi, active

    active = ind(searching & (clo > kf) & (clo - chi > BRACKET_STOP))
    st = lax.while_loop(bis_cond, bis_body,
                        (jnp.int32(0), any_set(active), lo, hi, clo, chi, active))
    _, _, lo, hi, clo, chi, _ = st

    unres = ind(searching & (clo > kf))

    def peel_cond(st):
        return (st[0] < MAX_PEEL) & st[1]

    def peel_body(st):
        it, _, lo, hi, clo, chi, cand, unres = st

        def body(j, s2):
            a8, m8 = s2
            blk = sc_ref[j]
            ge = blk >= cand
            below = jnp.where((blk >= lo) & ~ge, blk, -jnp.inf)
            return a8 + fold(ind(ge)), jnp.maximum(m8, fold_max(below))
        a8, m8 = lax.fori_loop(0, nblk, body, (zeros8, ninf8))
        c = total(a8)
        nxt = total_max(m8)
        on = unres > 0.0
        found = on & (c >= kf)
        down = on & (c < kf)
        lo = jnp.where(found, cand, lo)
        clo = jnp.where(found, c, clo)
        hi = jnp.where(down, cand, hi)
        chi = jnp.where(down, c, chi)
        cand = jnp.where(down, nxt, cand)
        return it + 1, any_set(ind(down)), lo, hi, clo, chi, cand, ind(down)

    st = lax.while_loop(peel_cond, peel_body,
                        (jnp.int32(0), any_set(unres), lo, hi, clo, chi, hi, unres))
    _, _, lo, hi, clo, chi, _, _ = st

    tied = any_set(ind(~small & (clo > kf)))

    @pl.when(jnp.logical_not(tied))
    def _():
        def body(j, c):
            mask_ref[j] = jnp.where(sc_ref[j] >= lo, 0.0, -jnp.inf)
            return c
        lax.fori_loop(0, nblk, body, 0)

    @pl.when(tied)
    def _():
        need = kf - count(lo, strict=True)
        earlier = ind(qry_in_blk < key_in_blk).astype(BF16)

        def body(j, seen):
            blk = sc_ref[j]
            eq = ind(blk == lo)
            rank = seen + jnp.dot(earlier, eq.astype(BF16), preferred_element_type=F32)
            keep = (blk > lo) | ((blk == lo) & (rank < need))
            mask_ref[j] = jnp.where(keep, 0.0, -jnp.inf)
            return seen + total(fold(eq))
        over_blocks(body, jnp.zeros((1, tw), F32))

    m_ref[...] = jnp.full(m_ref.shape, -jnp.inf, F32)
    acc_ref[...] = jnp.zeros(acc_ref.shape, F32)

    def attend(j, nkb, bias):
        tk = nkb * tq
        k0 = pl.multiple_of(j * tq, tq)
        heads = lambda f: jnp.concatenate([f(h) for h in range(N_HEADS)], axis=1)
        for g in range(nbat):
            cols = slice(g * tq, (g + 1) * tq)
            kv = ckv_ref[g, pl.ds(k0, tk), :]
            kvt1 = jnp.concatenate([ckvt_ref[g, j + b] for b in range(nkb)], axis=1)
            mask = jnp.concatenate([mask_ref[j + b, :, cols] for b in range(nkb)], axis=0)
            qa = heads(lambda h: qat_ref[g, h * KV_LATENT:(h + 1) * KV_LATENT, :])
            s = jnp.dot(kv, qa, preferred_element_type=F32)
            if bias is not None:
                s = s + heads(bias)
            s = s + heads(lambda h: mask)
            m_prev = m_ref[g]
            m_new = jnp.maximum(m_prev, jnp.max(s, axis=0, keepdims=True))
            m_safe = jnp.where(m_new == -jnp.inf, 0.0, m_new)
            alpha = jnp.exp2(m_prev - m_safe)
            p = jnp.exp2(s - m_safe).astype(BF16)
            m_ref[g] = m_new
            pv = jnp.dot(kvt1, p, preferred_element_type=F32)
            acc_ref[g] = alpha * acc_ref[g] + pv

    nfar = jnp.maximum(i - 1, 0)

    def far_pair(pi, c):
        attend(2 * pi, 2, None)
        return c
    lax.fori_loop(0, lax.shift_right_logical(nfar, 1), far_pair, 0)

    @pl.when((nfar & 1) == 1)
    def _():
        attend(nfar - 1, 1, None)

    @pl.when(i >= 1)
    def _():
        attend(i - 1, 2, lambda h: bias_ref[h])

    @pl.when(i == 0)
    def _():
        attend(0, 1, lambda h: bias_ref[h, tq:, :])

    for g in range(nbat):
        outs = []
        for h in range(N_HEADS):
            a = acc_ref[g, :, h * tq:(h + 1) * tq]
            ctx = (a[:KV_LATENT, :] / a[KV_LATENT:KV_LATENT + 1, :]).astype(BF16)
            outs.append(jnp.dot(wuvt_ref[h], ctx, preferred_element_type=F32))
        out = jnp.concatenate(outs, axis=0).T
        o_ref[g] = _rms(out, gout_ref[...]).astype(BF16)


def _dsa(qit, wit, qat, ki, ckv, ckvt, bias, wuvt, gout, n_keep):
    b, s, _ = ckv.shape
    tq = Q_BLOCK
    nb = s // tq
    nq = N_HEADS_IDX * IDX_DIM
    nbat = DSA_BATCH if b % DSA_BATCH == 0 else 1
    colblk = lambda r: pl.BlockSpec((nbat, r, tq), lambda bi, i: (bi, 0, i))
    return pl.pallas_call(
        functools.partial(_dsa_kernel, n_keep=n_keep),
        grid=(b // nbat, nb),
        in_specs=[
            colblk(nq), colblk(N_HEADS_IDX), colblk(N_HEADS * KV_LATENT),
            pl.BlockSpec((nbat, s, IDX_DIM), lambda bi, i: (bi, 0, 0)),
            pl.BlockSpec((nbat, s, KV_LATENT), lambda bi, i: (bi, 0, 0)),
            pl.BlockSpec((nbat, nb, KVT_ROWS, tq), lambda bi, i: (bi, 0, 0, 0)),
            _const_spec(bias.shape), _const_spec(wuvt.shape), _const_spec(gout.shape),
        ],
        out_specs=pl.BlockSpec((nbat, tq, ATTN_WIDTH), lambda bi, i: (bi, i, 0)),
        out_shape=jax.ShapeDtypeStruct((b, s, ATTN_WIDTH), BF16),
        scratch_shapes=[
            pltpu.VMEM((nb, tq, nbat * tq), F32),
            pltpu.VMEM((nb, tq, nbat * tq), F32),
            pltpu.VMEM((nbat, KVT_ROWS, N_HEADS * tq), F32),
            pltpu.VMEM((nbat, 1, N_HEADS * tq), F32),
        ],
        compiler_params=pltpu.CompilerParams(dimension_semantics=("arbitrary", "arbitrary"),
                                             vmem_limit_bytes=VMEM_LIMIT),
        name="dsa",
    )(qit, wit, qat, ki, ckv, ckvt, bias, wuvt, gout)


def kernel(x, ffn1_norm, ffn1_w_gate, ffn1_w_up, ffn1_w_down, mix_norm, w_in, kv_norm, w_uk, w_uv, rel_bias, conv_w, attn_out_norm, conv_out_norm, w_out, ffn2_norm, ffn2_w_gate, ffn2_w_up, ffn2_w_down, final_norm):
    b, s, d = x.shape
    depth = ffn1_norm.shape[0]
    n_keep = min(TOPK_MAX, s // 4)
    assert s % Q_BLOCK == 0 and s % PROJ_TOKENS == 0 and (b * s) % FFN_TOKENS == 0
    assert Q_BLOCK >= MAX_DISTANCE and n_keep <= Q_BLOCK
    row = lambda v: v.reshape(1, -1).astype(F32)
    offs = [0]
    for w in IN_SIZES:
        offs.append(offs[-1] + w)

    bias = _bias_tiles(rel_bias.astype(F32), Q_BLOCK)
    h = x.astype(F32).reshape(b * s, d)
    for l in range(depth):
        h = _ffn1(h, row(ffn1_norm[l]), ffn1_w_gate[l].astype(BF16), ffn1_w_up[l].astype(BF16),
                  ffn1_w_down[l].astype(BF16))

        wi = w_in[l]
        col = lambda k: wi[:, offs[k]:offs[k + 1]]
        wcat = jnp.concatenate(
            [col(1), col(3), jnp.zeros((d, LANES - IDX_DIM), wi.dtype), col(5), col(6), col(7)],
            axis=1).astype(BF16)
        wt = jnp.concatenate(
            [col(0).T, col(1).T, col(2).T, col(4).T,
             jnp.zeros((_R_END - _R_WI - N_HEADS_IDX, d), wi.dtype)], axis=0).astype(BF16)
        wukt = jnp.transpose(w_uk[l], (1, 0, 2)).astype(BF16)
        wuvt = jnp.transpose(w_uv[l], (1, 2, 0)).astype(BF16)

        qat, ckv, ckvt, ki, qit, wit, convn = _proj(
            h.reshape(b, s, d), row(mix_norm[l]), wcat, wt, wukt, row(kv_norm[l]),
            conv_w[l].astype(F32), row(conv_out_norm[l]))
        attn = _dsa(qit, wit, qat, ki, ckv, ckvt, bias, wuvt, row(attn_out_norm[l]), n_keep)

        last = l == depth - 1
        h = _out_ffn2(h, attn.reshape(b * s, ATTN_WIDTH), convn.reshape(b * s, CONV_WIDTH),
                      w_out[l][:ATTN_WIDTH].astype(BF16), w_out[l][ATTN_WIDTH:].astype(BF16),
                      row(ffn2_norm[l]), ffn2_w_gate[l].astype(BF16), ffn2_w_up[l].astype(BF16),
                      ffn2_w_down[l].astype(BF16), row(final_norm), final_norm=last)
    return h.reshape(b, s, d).astype(x.dtype)
```

```python
import functools
import math

import jax
import jax.numpy as jnp
from jax import lax
from jax.experimental import pallas as pl
from jax.experimental.pallas import tpu as pltpu

F32 = jnp.float32
BF16 = jnp.bfloat16

EPS = 1e-6
N_HEADS = 8
HEAD_DIM = 64
ATTN_WIDTH = N_HEADS * HEAD_DIM
KV_LATENT = 128
N_HEADS_IDX = 8
IDX_DIM = 64
TOPK_MAX = 256
CONV_WIDTH = 512
CONV_K = 3
N_BUCKETS = 32
MAX_DISTANCE = 128
IN_SIZES = (ATTN_WIDTH, KV_LATENT, N_HEADS_IDX * IDX_DIM, IDX_DIM, N_HEADS_IDX,
            CONV_WIDTH, CONV_WIDTH, CONV_WIDTH)

SUBLANES = 8
LANES = 128

FFN_TOKENS = 512
FFN_CHUNK = 256
PROJ_TOKENS = 512
Q_BLOCK = 256
BISECT_CHUNK = 4
MAX_BISECT = 64
CAST_STEPS = 8
DSA_BATCH = 2
SWEEP_ACCS = 2
BRACKET_STOP = 0
MAX_PEEL = 4096
LOG2E = math.log2(math.e)
VMEM_LIMIT = 56 * 1024 * 1024


def _rms(x, g):
    return x * lax.rsqrt(jnp.mean(x * x, axis=-1, keepdims=True) + EPS) * g


def _const_spec(shape):
    nd = len(shape)
    return pl.BlockSpec(shape, lambda *_: (0,) * nd, pipeline_mode=pl.Buffered(1))


def _swiglu_into(acc_ref, xn, wg_ref, wu_ref, wd_ref):
    d_ff = wg_ref.shape[1]
    for c in range(d_ff // FFN_CHUNK):
        sl = slice(c * FFN_CHUNK, (c + 1) * FFN_CHUNK)
        g = jnp.dot(xn, wg_ref[:, sl], preferred_element_type=F32)
        u = jnp.dot(xn, wu_ref[:, sl], preferred_element_type=F32)
        a = (g * jax.nn.sigmoid(g) * u).astype(BF16)
        d = jnp.dot(a, wd_ref[sl, :], preferred_element_type=F32)
        if c == 0:
            acc_ref[...] = d
        else:
            acc_ref[...] += d


def _ffn1_kernel(x_ref, g_ref, wg_ref, wu_ref, wd_ref, o_ref, acc_ref):
    x = x_ref[...]
    xn = _rms(x, g_ref[...]).astype(BF16)
    _swiglu_into(acc_ref, xn, wg_ref, wu_ref, wd_ref)
    o_ref[...] = x + 0.5 * acc_ref[...]


def _out_ffn2_kernel(h_ref, a_ref, c_ref, woa_ref, woc_ref, g_ref, wg_ref, wu_ref, wd_ref,
                     gf_ref, o_ref, acc_ref, *, final_norm):
    h = (h_ref[...]
         + jnp.dot(a_ref[...], woa_ref[...], preferred_element_type=F32)
         + jnp.dot(c_ref[...], woc_ref[...], preferred_element_type=F32))
    xn = _rms(h, g_ref[...]).astype(BF16)
    _swiglu_into(acc_ref, xn, wg_ref, wu_ref, wd_ref)
    h = h + 0.5 * acc_ref[...]
    o_ref[...] = _rms(h, gf_ref[...]) if final_norm else h


def _ffn1(x2, g, wg, wu, wd):
    n, d = x2.shape
    tile = pl.BlockSpec((FFN_TOKENS, d), lambda i: (i, 0))
    return pl.pallas_call(
        _ffn1_kernel,
        grid=(n // FFN_TOKENS,),
        in_specs=[tile, _const_spec(g.shape), _const_spec(wg.shape), _const_spec(wu.shape),
                  _const_spec(wd.shape)],
        out_specs=tile,
        out_shape=jax.ShapeDtypeStruct((n, d), F32),
        scratch_shapes=[pltpu.VMEM((FFN_TOKENS, d), F32)],
        compiler_params=pltpu.CompilerParams(dimension_semantics=("arbitrary",),
                                             vmem_limit_bytes=VMEM_LIMIT),
        name="ffn1",
    )(x2, g, wg, wu, wd)


def _out_ffn2(h2, a2, c2, woa, woc, g, wg, wu, wd, gf, final_norm):
    n, d = h2.shape
    tile = pl.BlockSpec((FFN_TOKENS, d), lambda i: (i, 0))
    mix = pl.BlockSpec((FFN_TOKENS, a2.shape[1]), lambda i: (i, 0))
    return pl.pallas_call(
        functools.partial(_out_ffn2_kernel, final_norm=final_norm),
        grid=(n // FFN_TOKENS,),
        in_specs=[tile, mix, mix, _const_spec(woa.shape), _const_spec(woc.shape),
                  _const_spec(g.shape), _const_spec(wg.shape), _const_spec(wu.shape),
                  _const_spec(wd.shape), _const_spec(gf.shape)],
        out_specs=tile,
        out_shape=jax.ShapeDtypeStruct((n, d), F32),
        scratch_shapes=[pltpu.VMEM((FFN_TOKENS, d), F32)],
        compiler_params=pltpu.CompilerParams(dimension_semantics=("arbitrary",),
                                             vmem_limit_bytes=VMEM_LIMIT),
        name="out_ffn2",
    )(h2, a2, c2, woa, woc, g, wg, wu, wd, gf)


_C_KV = 0
_C_KI = _C_KV + KV_LATENT
_C_GB = _C_KI + LANES
_C_GC = _C_GB + CONV_WIDTH
_C_X = _C_GC + CONV_WIDTH
_C_END = _C_X + CONV_WIDTH
_R_Q = 0
_R_KV = _R_Q + ATTN_WIDTH
_R_QI = _R_KV + KV_LATENT
_R_WI = _R_QI + N_HEADS_IDX * IDX_DIM
_R_END = _R_WI + 2 * SUBLANES
KVT_ROWS = KV_LATENT + 2 * SUBLANES


def _proj_kernel(h_ref, gm_ref, wcat_ref, wt_ref, wukt_ref, gkv_ref, gkvc_ref, cw_ref, gco_ref,
                 qat_ref, ckv_ref, ckvt_ref, ki_ref, qit_ref, wit_ref, convn_ref, carry_ref):
    tp = h_ref.shape[0]

    @pl.when(pl.program_id(1) == 0)
    def _():
        carry_ref[...] = jnp.zeros_like(carry_ref)

    un = _rms(h_ref[...], gm_ref[...]).astype(BF16)
    p = jnp.dot(un, wcat_ref[...], preferred_element_type=F32)
    ckv_ref[...] = _rms(p[:, _C_KV:_C_KI], gkv_ref[...]).astype(BF16)
    ki_ref[...] = p[:, _C_KI:_C_KI + IDX_DIM].astype(BF16)

    v = p[:, _C_GC:_C_X] * p[:, _C_X:_C_END]
    row = lax.broadcasted_iota(jnp.int32, v.shape, 0)
    prev1 = jnp.broadcast_to(carry_ref[SUBLANES - 1:SUBLANES, :], v.shape)
    prev2 = jnp.broadcast_to(carry_ref[SUBLANES - 2:SUBLANES - 1, :], v.shape)
    v1 = jnp.where(row == 0, prev1, pltpu.roll(v, 1, 0))
    v2 = jnp.where(row == 0, prev2, jnp.where(row == 1, prev1, pltpu.roll(v, 2, 0)))
    carry_ref[...] = v[tp - SUBLANES:, :]
    y = cw_ref[0:1, :] * v2 + cw_ref[1:2, :] * v1 + cw_ref[2:3, :] * v
    conv = p[:, _C_GB:_C_GC] * y
    convn_ref[...] = _rms(conv, gco_ref[...]).astype(BF16)

    pt = lax.dot_general(wt_ref[...], un, (((1,), (1,)), ((), ())),
                         preferred_element_type=F32)
    qt = pt[_R_Q:_R_KV, :].astype(BF16)
    for h in range(N_HEADS):
        qah = jnp.dot(wukt_ref[h], qt[h * HEAD_DIM:(h + 1) * HEAD_DIM, :],
                      preferred_element_type=F32)
        qat_ref[h * KV_LATENT:(h + 1) * KV_LATENT, :] = (qah * (HEAD_DIM ** -0.5 * LOG2E)).astype(BF16)
    kvt = pt[_R_KV:_R_QI, :]
    kvt = kvt * lax.rsqrt(jnp.mean(kvt * kvt, axis=0, keepdims=True) + EPS) * gkvc_ref[...]
    tk = ckvt_ref.shape[2]
    tail = jnp.concatenate([jnp.ones((SUBLANES, tk), F32), jnp.zeros((SUBLANES, tk), F32)], axis=0)
    for kb in range(tp // tk):
        blk = jnp.concatenate([kvt[:, kb * tk:(kb + 1) * tk], tail], axis=0)
        ckvt_ref[kb] = blk.astype(BF16)
    qit_ref[...] = pt[_R_QI:_R_WI, :].astype(BF16)
    wit_ref[...] = pt[_R_WI:_R_WI + N_HEADS_IDX, :] * ((N_HEADS_IDX * IDX_DIM) ** -0.5)


def _proj(h1, gm, wcat, wt, wukt, gkv, cw, gco):
    b, s, d = h1.shape
    tp = PROJ_TOKENS
    nq = N_HEADS_IDX * IDX_DIM
    nqa = N_HEADS * KV_LATENT
    gkvc = gkv.reshape(KV_LATENT, 1)
    tok = lambda w: pl.BlockSpec((None, tp, w), lambda bi, ti: (bi, ti, 0))
    tokt = lambda r: pl.BlockSpec((None, r, tp), lambda bi, ti: (bi, 0, ti))
    out_shape = (
        jax.ShapeDtypeStruct((b, nqa, s), BF16),
        jax.ShapeDtypeStruct((b, s, KV_LATENT), BF16),
        jax.ShapeDtypeStruct((b, s // Q_BLOCK, KVT_ROWS, Q_BLOCK), BF16),
        jax.ShapeDtypeStruct((b, s, IDX_DIM), BF16),
        jax.ShapeDtypeStruct((b, nq, s), BF16),
        jax.ShapeDtypeStruct((b, N_HEADS_IDX, s), F32),
        jax.ShapeDtypeStruct((b, s, CONV_WIDTH), BF16),
    )
    out_specs = (
        tokt(nqa), tok(KV_LATENT),
        pl.BlockSpec((None, tp // Q_BLOCK, KVT_ROWS, Q_BLOCK), lambda bi, ti: (bi, ti, 0, 0)),
        tok(IDX_DIM), tokt(nq), tokt(N_HEADS_IDX), tok(CONV_WIDTH),
    )
    return pl.pallas_call(
        _proj_kernel,
        grid=(b, s // tp),
        in_specs=[tok(d), _const_spec(gm.shape), _const_spec(wcat.shape), _const_spec(wt.shape),
                  _const_spec(wukt.shape), _const_spec(gkv.shape), _const_spec(gkvc.shape),
                  _const_spec(cw.shape), _const_spec(gco.shape)],
        out_specs=out_specs,
        out_shape=out_shape,
        scratch_shapes=[pltpu.VMEM((SUBLANES, CONV_WIDTH), F32)],
        compiler_params=pltpu.CompilerParams(dimension_semantics=("arbitrary", "arbitrary"),
                                             vmem_limit_bytes=VMEM_LIMIT),
        name="proj",
    )(h1, gm, wcat, wt, wukt, gkv, gkvc, cw, gco)


def _bucket_starts():
    e = N_BUCKETS // 2
    n = N_BUCKETS - e
    assert MAX_DISTANCE % e == 0
    r = MAX_DISTANCE // e
    starts, d = [], e
    for k in range(1, n):
        while d ** n < e ** n * r ** k:
            d += 1
        starts.append(d)
    return starts


def _rel_bucket(dist):
    max_exact = N_BUCKETS // 2
    large = max_exact
    for start in _bucket_starts():
        large = large + (dist >= start).astype(jnp.int32)
    return jnp.where(dist < max_exact, dist, large)


def _bias_kernel(rb_ref, o_ref):
    h = pl.program_id(0)
    tk, tq = o_ref.shape
    s = lax.broadcasted_iota(jnp.int32, (tk, tq), 0)
    t = lax.broadcasted_iota(jnp.int32, (tk, tq), 1)
    bucket = _rel_bucket(jnp.maximum(t - s + tq, 0))
    val = jnp.zeros((tk, tq), F32)
    for b in range(N_BUCKETS):
        val = jnp.where(bucket == b, rb_ref[b, h], val)
    o_ref[...] = (val - rb_ref[N_BUCKETS - 1, h]) * LOG2E


def _bias_tiles(rel_bias, tq):
    return pl.pallas_call(
        _bias_kernel,
        grid=(N_HEADS,),
        in_specs=[pl.BlockSpec(memory_space=pltpu.SMEM)],
        out_specs=pl.BlockSpec((None, 2 * tq, tq), lambda h: (h, 0, 0)),
        out_shape=jax.ShapeDtypeStruct((N_HEADS, 2 * tq, tq), F32),
        compiler_params=pltpu.CompilerParams(dimension_semantics=("arbitrary",)),
        name="bias_tiles",
    )(rel_bias)


def _dsa_kernel(qit_ref, wit_ref, qat_ref, ki_ref, ckv_ref, ckvt_ref, bias_ref, wuvt_ref, gout_ref,
                o_ref, sc_ref, mask_ref, acc_ref, m_ref, *, n_keep):
    nbat, tq = o_ref.shape[0], o_ref.shape[1]
    tw = nbat * tq
    i = pl.program_id(1)
    nblk = i + 1
    q0 = i * tq
    kf = float(n_keep)
    key_in_blk = lax.broadcasted_iota(jnp.int32, (tq, tq), 0)
    qry_in_blk = lax.broadcasted_iota(jnp.int32, (tq, tq), 1)
    ind = lambda mask: jnp.where(mask, 1.0, 0.0)
    batches = lambda f: jnp.concatenate([f(g) for g in range(nbat)], axis=1)
    part = lambda a: a.reshape(tq // (SWEEP_ACCS * SUBLANES), SWEEP_ACCS, SUBLANES, tw)
    fold = lambda a: part(a).sum(axis=0)
    fold_max = lambda a: part(a).max(axis=0)
    total = lambda a: a.sum(axis=0).sum(axis=0, keepdims=True)
    total_max = lambda a: a.max(axis=0).max(axis=0, keepdims=True)
    zeros8 = jnp.zeros((SWEEP_ACCS, SUBLANES, tw), F32)
    ninf8 = jnp.full((SWEEP_ACCS, SUBLANES, tw), -jnp.inf, F32)

    def score_body(j, carry):
        mn, mx = carry
        k0 = pl.multiple_of(j * tq, tq)
        causal1 = (k0 + key_in_blk) <= (q0 + qry_in_blk)

        def one(g):
            kib = ki_ref[g, pl.ds(k0, tq), :]
            acc = jnp.zeros((tq, tq), F32)
            for h in range(N_HEADS_IDX):
                d = jnp.dot(kib, qit_ref[g, h * IDX_DIM:(h + 1) * IDX_DIM, :],
                            preferred_element_type=F32)
                acc = acc + jnp.maximum(d, 0.0) * wit_ref[g, h:h + 1, :]
            return acc
        acc = batches(one)
        causal = batches(lambda g: causal1)
        masked = jnp.where(causal, acc, -jnp.inf)
        sc_ref[j] = masked
        mx = jnp.maximum(mx, jnp.max(masked, axis=0, keepdims=True))
        mn = jnp.minimum(mn, jnp.min(jnp.where(causal, acc, jnp.inf), axis=0, keepdims=True))
        return mn, mx

    def over_blocks(body, init):
        def pair(pi, c):
            return body(2 * pi + 1, body(2 * pi, c))
        c = lax.fori_loop(0, lax.shift_right_logical(nblk, 1), pair, init)
        return lax.cond((nblk & 1) == 1, lambda c: body(nblk - 1, c), lambda c: c, c)

    mn, mx = over_blocks(score_body,
                         (jnp.full((1, tw), jnp.inf, F32), jnp.full((1, tw), -jnp.inf, F32)))

    def count(bound, strict=False):
        def one(j, g):
            cols = slice(g * tq, (g + 1) * tq)
            blk = sc_ref[j, :, cols]
            hit = blk > bound[:, cols] if strict else blk >= bound[:, cols]
            return ind(hit).reshape(tq // (SWEEP_ACCS * SUBLANES), SWEEP_ACCS, SUBLANES, tq).sum(axis=0)
        a8 = lax.fori_loop(
            0, nblk,
            lambda j, a: a + jnp.concatenate([one(j, g) for g in range(nbat)], axis=-1), zeros8)
        return total(a8)

    def any_set(flag):
        return jnp.max(flag) > 0.0

    c_ge0 = count(jnp.zeros((1, tw), F32))
    c_gt0 = count(jnp.zeros((1, tw), F32), strict=True)
    ncausal = batches(lambda g: (q0 + lax.broadcasted_iota(jnp.int32, (1, tq), 1) + 1).astype(F32))
    small = ncausal <= kf
    neg = c_ge0 < kf
    zero_tie = (c_gt0 < kf) & ~neg
    lo = jnp.where(small | neg, mn, 0.0)
    clo = jnp.where(small | neg, ncausal, c_ge0)
    above_mx = mx + jnp.abs(mx) * 2.0 ** -22 + 1e-37
    hi = jnp.where(neg, 0.0, above_mx)
    chi = jnp.where(neg, c_ge0, 0.0)
    searching = ~small & ~zero_tie

    def bis_cond(st):
        return (st[0] < MAX_BISECT) & st[1]

    def bis_body(st):
        it, _, lo, hi, clo, chi, active = st
        for _ in range(BISECT_CHUNK):
            mid = lo + (hi - lo) * 0.5
            live = (active > 0.0) & (mid > lo) & (mid < hi)
            c = count(mid)
            up = live & (c >= kf)
            dn = live & (c < kf)
            lo = jnp.where(up, mid, lo)
            clo = jnp.where(up, c, clo)
            hi = jnp.where(dn, mid, hi)
            chi = jnp.where(dn, c, chi)
            active = ind(live & (clo > kf) & (clo - chi > BRACKET_STOP))
        return it + 1, any_set(active), lo, hi, clo, chi, active

    active = ind(searching & (clo > kf) & (clo - chi > BRACKET_STOP))
    st = lax.while_loop(bis_cond, bis_body,
                        (jnp.int32(0), any_set(active), lo, hi, clo, chi, active))
    _, _, lo, hi, clo, chi, _ = st

    unres = ind(searching & (clo > kf))

    def peel_cond(st):
        return (st[0] < MAX_PEEL) & st[1]

    def peel_body(st):
        it, _, lo, hi, clo, chi, cand, unres = st

        def body(j, s2):
            a8, m8 = s2
            blk = sc_ref[j]
            ge = blk >= cand
            below = jnp.where((blk >= lo) & ~ge, blk, -jnp.inf)
            return a8 + fold(ind(ge)), jnp.maximum(m8, fold_max(below))
        a8, m8 = lax.fori_loop(0, nblk, body, (zeros8, ninf8))
        c = total(a8)
        nxt = total_max(m8)
        on = unres > 0.0
        found = on & (c >= kf)
        down = on & (c < kf)
        lo = jnp.where(found, cand, lo)
        clo = jnp.where(found, c, clo)
        hi = jnp.where(down, cand, hi)
        chi = jnp.where(down, c, chi)
        cand = jnp.where(down, nxt, cand)
        return it + 1, any_set(ind(down)), lo, hi, clo, chi, cand, ind(down)

    st = lax.while_loop(peel_cond, peel_body,
                        (jnp.int32(0), any_set(unres), lo, hi, clo, chi, hi, unres))
    _, _, lo, hi, clo, chi, _, _ = st

    tied = any_set(ind(~small & (clo > kf)))

    @pl.when(jnp.logical_not(tied))
    def _():
        def body(j, c):
            mask_ref[j] = jnp.where(sc_ref[j] >= lo, 0.0, -jnp.inf)
            return c
        lax.fori_loop(0, nblk, body, 0)

    @pl.when(tied)
    def _():
        need = kf - count(lo, strict=True)
        earlier = ind(qry_in_blk < key_in_blk).astype(BF16)

        def body(j, seen):
            blk = sc_ref[j]
            eq = ind(blk == lo)
            rank = seen + jnp.dot(earlier, eq.astype(BF16), preferred_element_type=F32)
            keep = (blk > lo) | ((blk == lo) & (rank < need))
            mask_ref[j] = jnp.where(keep, 0.0, -jnp.inf)
            return seen + total(fold(eq))
        over_blocks(body, jnp.zeros((1, tw), F32))

    m_ref[...] = jnp.full(m_ref.shape, -jnp.inf, F32)
    acc_ref[...] = jnp.zeros(acc_ref.shape, F32)

    def attend(j, nkb, bias):
        tk = nkb * tq
        k0 = pl.multiple_of(j * tq, tq)
        heads = lambda f: jnp.concatenate([f(h) for h in range(N_HEADS)], axis=1)
        for g in range(nbat):
            cols = slice(g * tq, (g + 1) * tq)
            kv = ckv_ref[g, pl.ds(k0, tk), :]
            kvt1 = jnp.concatenate([ckvt_ref[g, j + b] for b in range(nkb)], axis=1)
            mask = jnp.concatenate([mask_ref[j + b, :, cols] for b in range(nkb)], axis=0)
            qa = heads(lambda h: qat_ref[g, h * KV_LATENT:(h + 1) * KV_LATENT, :])
            s = jnp.dot(kv, qa, preferred_element_type=F32)
            if bias is not None:
                s = s + heads(bias)
            s = s + heads(lambda h: mask)
            m_prev = m_ref[g]
            m_new = jnp.maximum(m_prev, jnp.max(s, axis=0, keepdims=True))
            m_safe = jnp.where(m_new == -jnp.inf, 0.0, m_new)
            alpha = jnp.exp2(m_prev - m_safe)
            p = jnp.exp2(s - m_safe).astype(BF16)
            m_ref[g] = m_new
            pv = jnp.dot(kvt1, p, preferred_element_type=F32)
            acc_ref[g] = alpha * acc_ref[g] + pv

    nfar = jnp.maximum(i - 1, 0)

    def far_pair(pi, c):
        attend(2 * pi, 2, None)
        return c
    lax.fori_loop(0, lax.shift_right_logical(nfar, 1), far_pair, 0)

    @pl.when((nfar & 1) == 1)
    def _():
        attend(nfar - 1, 1, None)

    @pl.when(i >= 1)
    def _():
        attend(i - 1, 2, lambda h: bias_ref[h])

    @pl.when(i == 0)
    def _():
        attend(0, 1, lambda h: bias_ref[h, tq:, :])

    for g in range(nbat):
        outs = []
        for h in range(N_HEADS):
            a = acc_ref[g, :, h * tq:(h + 1) * tq]
            ctx = (a[:KV_LATENT, :] / a[KV_LATENT:KV_LATENT + 1, :]).astype(BF16)
            outs.append(jnp.dot(wuvt_ref[h], ctx, preferred_element_type=F32))
        out = jnp.concatenate(outs, axis=0).T
        o_ref[g] = _rms(out, gout_ref[...]).astype(BF16)


def _dsa(qit, wit, qat, ki, ckv, ckvt, bias, wuvt, gout, n_keep):
    b, s, _ = ckv.shape
    tq = Q_BLOCK
    nb = s // tq
    nq = N_HEADS_IDX * IDX_DIM
    nbat = DSA_BATCH if b % DSA_BATCH == 0 else 1
    colblk = lambda r: pl.BlockSpec((nbat, r, tq), lambda bi, i: (bi, 0, i))
    return pl.pallas_call(
        functools.partial(_dsa_kernel, n_keep=n_keep),
        grid=(b // nbat, nb),
        in_specs=[
            colblk(nq), colblk(N_HEADS_IDX), colblk(N_HEADS * KV_LATENT),
            pl.BlockSpec((nbat, s, IDX_DIM), lambda bi, i: (bi, 0, 0)),
            pl.BlockSpec((nbat, s, KV_LATENT), lambda bi, i: (bi, 0, 0)),
            pl.BlockSpec((nbat, nb, KVT_ROWS, tq), lambda bi, i: (bi, 0, 0, 0)),
            _const_spec(bias.shape), _const_spec(wuvt.shape), _const_spec(gout.shape),
        ],
        out_specs=pl.BlockSpec((nbat, tq, ATTN_WIDTH), lambda bi, i: (bi, i, 0)),
        out_shape=jax.ShapeDtypeStruct((b, s, ATTN_WIDTH), BF16),
        scratch_shapes=[
            pltpu.VMEM((nb, tq, nbat * tq), F32),
            pltpu.VMEM((nb, tq, nbat * tq), F32),
            pltpu.VMEM((nbat, KVT_ROWS, N_HEADS * tq), F32),
            pltpu.VMEM((nbat, 1, N_HEADS * tq), F32),
        ],
        compiler_params=pltpu.CompilerParams(dimension_semantics=("arbitrary", "arbitrary"),
                                             vmem_limit_bytes=VMEM_LIMIT),
        name="dsa",
    )(qit, wit, qat, ki, ckv, ckvt, bias, wuvt, gout)


def _cast_kernel(*refs):
    n = len(refs) // 2
    for src, dst in zip(refs[:n], refs[n:]):
        dst[...] = src[...].astype(BF16)


def _to_bf16(*ws):
    r, c = ws[0].shape
    spec = pl.BlockSpec((r // CAST_STEPS, c), lambda i: (i, 0))
    return pl.pallas_call(
        _cast_kernel,
        grid=(CAST_STEPS,),
        in_specs=[spec] * len(ws),
        out_specs=[spec] * len(ws),
        out_shape=[jax.ShapeDtypeStruct(w.shape, BF16) for w in ws],
        compiler_params=pltpu.CompilerParams(dimension_semantics=("arbitrary",),
                                             vmem_limit_bytes=VMEM_LIMIT),
        name="to_bf16",
    )(*ws)


def kernel(x, ffn1_norm, ffn1_w_gate, ffn1_w_up, ffn1_w_down, mix_norm, w_in, kv_norm, w_uk, w_uv, rel_bias, conv_w, attn_out_norm, conv_out_norm, w_out, ffn2_norm, ffn2_w_gate, ffn2_w_up, ffn2_w_down, final_norm):
    b, s, d = x.shape
    depth = ffn1_norm.shape[0]
    n_keep = min(TOPK_MAX, s // 4)
    assert s % Q_BLOCK == 0 and s % PROJ_TOKENS == 0 and (b * s) % FFN_TOKENS == 0
    assert Q_BLOCK >= MAX_DISTANCE and n_keep <= Q_BLOCK
    row = lambda v: v.reshape(1, -1).astype(F32)
    offs = [0]
    for w in IN_SIZES:
        offs.append(offs[-1] + w)

    bias = _bias_tiles(rel_bias.astype(F32), Q_BLOCK)
    h = x.astype(F32).reshape(b * s, d)
    for l in range(depth):
        wg1, wu1, wg2, wu2 = _to_bf16(ffn1_w_gate[l], ffn1_w_up[l], ffn2_w_gate[l], ffn2_w_up[l])
        wd1, wd2 = _to_bf16(ffn1_w_down[l], ffn2_w_down[l])
        h = _ffn1(h, row(ffn1_norm[l]), wg1, wu1, wd1)

        wi = w_in[l]
        col = lambda k: wi[:, offs[k]:offs[k + 1]]
        wcat = jnp.concatenate(
            [col(1), col(3), jnp.zeros((d, LANES - IDX_DIM), wi.dtype), col(5), col(6), col(7)],
            axis=1).astype(BF16)
        wt = jnp.concatenate(
            [col(0).T, col(1).T, col(2).T, col(4).T,
             jnp.zeros((_R_END - _R_WI - N_HEADS_IDX, d), wi.dtype)], axis=0).astype(BF16)
        wukt = jnp.transpose(w_uk[l], (1, 0, 2)).astype(BF16)
        wuvt = jnp.transpose(w_uv[l], (1, 2, 0)).astype(BF16)

        qat, ckv, ckvt, ki, qit, wit, convn = _proj(
            h.reshape(b, s, d), row(mix_norm[l]), wcat, wt, wukt, row(kv_norm[l]),
            conv_w[l].astype(F32), row(conv_out_norm[l]))
        attn = _dsa(qit, wit, qat, ki, ckv, ckvt, bias, wuvt, row(attn_out_norm[l]), n_keep)

        last = l == depth - 1
        h = _out_ffn2(h, attn.reshape(b * s, ATTN_WIDTH), convn.reshape(b * s, CONV_WIDTH),
                      w_out[l][:ATTN_WIDTH].astype(BF16), w_out[l][ATTN_WIDTH:].astype(BF16),
                      row(ffn2_norm[l]), wg2, wu2, wd2, row(final_norm), final_norm=last)
    return h.reshape(b, s, d).astype(x.dtype)
```

```python
import functools
import math

import jax
import jax.numpy as jnp
from jax import lax
from jax.experimental import pallas as pl
from jax.experimental.pallas import tpu as pltpu

F32 = jnp.float32
BF16 = jnp.bfloat16

EPS = 1e-6
N_HEADS = 8
HEAD_DIM = 64
ATTN_WIDTH = N_HEADS * HEAD_DIM
KV_LATENT = 128
N_HEADS_IDX = 8
IDX_DIM = 64
TOPK_MAX = 256
CONV_WIDTH = 512
CONV_K = 3
N_BUCKETS = 32
MAX_DISTANCE = 128
IN_SIZES = (ATTN_WIDTH, KV_LATENT, N_HEADS_IDX * IDX_DIM, IDX_DIM, N_HEADS_IDX,
            CONV_WIDTH, CONV_WIDTH, CONV_WIDTH)

SUBLANES = 8
LANES = 128

FFN_TOKENS = 512
FFN_CHUNK = 256
PROJ_TOKENS = 512
Q_BLOCK = 256
BISECT_CHUNK = 4
MAX_BISECT = 64
CAST_STEPS = 8
DSA_BATCH = 2
BF16_ROWS = 16
SWEEP_ACCS = 2
MAX_PEEL = 4096
LOG2E = math.log2(math.e)
VMEM_LIMIT = 56 * 1024 * 1024


def _rms(x, g):
    return x * lax.rsqrt(jnp.mean(x * x, axis=-1, keepdims=True) + EPS) * g


def _const_spec(shape):
    nd = len(shape)
    return pl.BlockSpec(shape, lambda *_: (0,) * nd, pipeline_mode=pl.Buffered(1))


def _swiglu_into(acc_ref, xn, wg_ref, wu_ref, wd_ref):
    d_ff = wg_ref.shape[1]
    for c in range(d_ff // FFN_CHUNK):
        sl = slice(c * FFN_CHUNK, (c + 1) * FFN_CHUNK)
        g = jnp.dot(xn, wg_ref[:, sl], preferred_element_type=F32)
        u = jnp.dot(xn, wu_ref[:, sl], preferred_element_type=F32)
        a = (g * jax.nn.sigmoid(g) * u).astype(BF16)
        d = jnp.dot(a, wd_ref[sl, :], preferred_element_type=F32)
        if c == 0:
            acc_ref[...] = d
        else:
            acc_ref[...] += d


def _ffn1_kernel(x_ref, g_ref, wg_ref, wu_ref, wd_ref, o_ref, acc_ref):
    x = x_ref[...]
    xn = _rms(x, g_ref[...]).astype(BF16)
    _swiglu_into(acc_ref, xn, wg_ref, wu_ref, wd_ref)
    o_ref[...] = x + 0.5 * acc_ref[...]


def _out_ffn2_kernel(h_ref, a_ref, c_ref, woa_ref, woc_ref, g_ref, wg_ref, wu_ref, wd_ref,
                     gf_ref, o_ref, acc_ref, *, final_norm):
    h = (h_ref[...]
         + jnp.dot(a_ref[...], woa_ref[...], preferred_element_type=F32)
         + jnp.dot(c_ref[...], woc_ref[...], preferred_element_type=F32))
    xn = _rms(h, g_ref[...]).astype(BF16)
    _swiglu_into(acc_ref, xn, wg_ref, wu_ref, wd_ref)
    h = h + 0.5 * acc_ref[...]
    o_ref[...] = _rms(h, gf_ref[...]) if final_norm else h


def _ffn1(x2, g, wg, wu, wd):
    n, d = x2.shape
    tile = pl.BlockSpec((FFN_TOKENS, d), lambda i: (i, 0))
    return pl.pallas_call(
        _ffn1_kernel,
        grid=(n // FFN_TOKENS,),
        in_specs=[tile, _const_spec(g.shape), _const_spec(wg.shape), _const_spec(wu.shape),
                  _const_spec(wd.shape)],
        out_specs=tile,
        out_shape=jax.ShapeDtypeStruct((n, d), F32),
        scratch_shapes=[pltpu.VMEM((FFN_TOKENS, d), F32)],
        compiler_params=pltpu.CompilerParams(dimension_semantics=("arbitrary",),
                                             vmem_limit_bytes=VMEM_LIMIT),
        name="ffn1",
    )(x2, g, wg, wu, wd)


def _out_ffn2(h2, a2, c2, woa, woc, g, wg, wu, wd, gf, final_norm):
    n, d = h2.shape
    tile = pl.BlockSpec((FFN_TOKENS, d), lambda i: (i, 0))
    mix = pl.BlockSpec((FFN_TOKENS, a2.shape[1]), lambda i: (i, 0))
    return pl.pallas_call(
        functools.partial(_out_ffn2_kernel, final_norm=final_norm),
        grid=(n // FFN_TOKENS,),
        in_specs=[tile, mix, mix, _const_spec(woa.shape), _const_spec(woc.shape),
                  _const_spec(g.shape), _const_spec(wg.shape), _const_spec(wu.shape),
                  _const_spec(wd.shape), _const_spec(gf.shape)],
        out_specs=tile,
        out_shape=jax.ShapeDtypeStruct((n, d), F32),
        scratch_shapes=[pltpu.VMEM((FFN_TOKENS, d), F32)],
        compiler_params=pltpu.CompilerParams(dimension_semantics=("arbitrary",),
                                             vmem_limit_bytes=VMEM_LIMIT),
        name="out_ffn2",
    )(h2, a2, c2, woa, woc, g, wg, wu, wd, gf)


_C_KV = 0
_C_KI = _C_KV + KV_LATENT
_C_GB = _C_KI + LANES
_C_GC = _C_GB + CONV_WIDTH
_C_X = _C_GC + CONV_WIDTH
_C_END = _C_X + CONV_WIDTH
_R_Q = 0
_R_KV = _R_Q + ATTN_WIDTH
_R_QI = _R_KV + KV_LATENT
_R_WI = _R_QI + N_HEADS_IDX * IDX_DIM
_R_END = _R_WI + 2 * SUBLANES
KVT_ROWS = KV_LATENT + 2 * SUBLANES


def _proj_kernel(h_ref, gm_ref, wcat_ref, wt_ref, wukt_ref, gkv_ref, gkvc_ref, cw_ref, gco_ref,
                 qat_ref, ckv_ref, ckvt_ref, ki_ref, qit_ref, wit_ref, convn_ref, carry_ref):
    tp = h_ref.shape[0]

    @pl.when(pl.program_id(1) == 0)
    def _():
        carry_ref[...] = jnp.zeros_like(carry_ref)

    un = _rms(h_ref[...], gm_ref[...]).astype(BF16)
    p = jnp.dot(un, wcat_ref[...], preferred_element_type=F32)
    ckv_ref[...] = _rms(p[:, _C_KV:_C_KI], gkv_ref[...]).astype(BF16)
    ki_ref[...] = p[:, _C_KI:_C_KI + IDX_DIM].astype(BF16)

    v = p[:, _C_GC:_C_X] * p[:, _C_X:_C_END]
    row = lax.broadcasted_iota(jnp.int32, v.shape, 0)
    prev1 = jnp.broadcast_to(carry_ref[SUBLANES - 1:SUBLANES, :], v.shape)
    prev2 = jnp.broadcast_to(carry_ref[SUBLANES - 2:SUBLANES - 1, :], v.shape)
    v1 = jnp.where(row == 0, prev1, pltpu.roll(v, 1, 0))
    v2 = jnp.where(row == 0, prev2, jnp.where(row == 1, prev1, pltpu.roll(v, 2, 0)))
    carry_ref[...] = v[tp - SUBLANES:, :]
    y = cw_ref[0:1, :] * v2 + cw_ref[1:2, :] * v1 + cw_ref[2:3, :] * v
    conv = p[:, _C_GB:_C_GC] * y
    convn_ref[...] = _rms(conv, gco_ref[...]).astype(BF16)

    pt = lax.dot_general(wt_ref[...], un, (((1,), (1,)), ((), ())),
                         preferred_element_type=F32)
    qt = pt[_R_Q:_R_KV, :].astype(BF16)
    for h in range(N_HEADS):
        qah = jnp.dot(wukt_ref[h], qt[h * HEAD_DIM:(h + 1) * HEAD_DIM, :],
                      preferred_element_type=F32)
        qat_ref[h * KV_LATENT:(h + 1) * KV_LATENT, :] = (qah * (HEAD_DIM ** -0.5 * LOG2E)).astype(BF16)
    kvt = pt[_R_KV:_R_QI, :]
    kvt = kvt * lax.rsqrt(jnp.mean(kvt * kvt, axis=0, keepdims=True) + EPS) * gkvc_ref[...]
    tk = ckvt_ref.shape[2]
    tail = jnp.concatenate([jnp.ones((SUBLANES, tk), F32), jnp.zeros((SUBLANES, tk), F32)], axis=0)
    for kb in range(tp // tk):
        blk = jnp.concatenate([kvt[:, kb * tk:(kb + 1) * tk], tail], axis=0)
        ckvt_ref[kb] = blk.astype(BF16)
    qit_ref[...] = pt[_R_QI:_R_WI, :].astype(BF16)
    wit_ref[...] = pt[_R_WI:_R_WI + N_HEADS_IDX, :] * ((N_HEADS_IDX * IDX_DIM) ** -0.5)


def _proj(h1, gm, wcat, wt, wukt, gkv, cw, gco):
    b, s, d = h1.shape
    tp = PROJ_TOKENS
    nq = N_HEADS_IDX * IDX_DIM
    nqa = N_HEADS * KV_LATENT
    gkvc = gkv.reshape(KV_LATENT, 1)
    tok = lambda w: pl.BlockSpec((None, tp, w), lambda bi, ti: (bi, ti, 0))
    tokt = lambda r: pl.BlockSpec((None, r, tp), lambda bi, ti: (bi, 0, ti))
    out_shape = (
        jax.ShapeDtypeStruct((b, nqa, s), BF16),
        jax.ShapeDtypeStruct((b, s, KV_LATENT), BF16),
        jax.ShapeDtypeStruct((b, s // Q_BLOCK, KVT_ROWS, Q_BLOCK), BF16),
        jax.ShapeDtypeStruct((b, s, IDX_DIM), BF16),
        jax.ShapeDtypeStruct((b, nq, s), BF16),
        jax.ShapeDtypeStruct((b, N_HEADS_IDX, s), F32),
        jax.ShapeDtypeStruct((b, s, CONV_WIDTH), BF16),
    )
    out_specs = (
        tokt(nqa), tok(KV_LATENT),
        pl.BlockSpec((None, tp // Q_BLOCK, KVT_ROWS, Q_BLOCK), lambda bi, ti: (bi, ti, 0, 0)),
        tok(IDX_DIM), tokt(nq), tokt(N_HEADS_IDX), tok(CONV_WIDTH),
    )
    return pl.pallas_call(
        _proj_kernel,
        grid=(b, s // tp),
        in_specs=[tok(d), _const_spec(gm.shape), _const_spec(wcat.shape), _const_spec(wt.shape),
                  _const_spec(wukt.shape), _const_spec(gkv.shape), _const_spec(gkvc.shape),
                  _const_spec(cw.shape), _const_spec(gco.shape)],
        out_specs=out_specs,
        out_shape=out_shape,
        scratch_shapes=[pltpu.VMEM((SUBLANES, CONV_WIDTH), F32)],
        compiler_params=pltpu.CompilerParams(dimension_semantics=("arbitrary", "arbitrary"),
                                             vmem_limit_bytes=VMEM_LIMIT),
        name="proj",
    )(h1, gm, wcat, wt, wukt, gkv, gkvc, cw, gco)


def _bucket_starts():
    e = N_BUCKETS // 2
    n = N_BUCKETS - e
    assert MAX_DISTANCE % e == 0
    r = MAX_DISTANCE // e
    starts, d = [], e
    for k in range(1, n):
        while d ** n < e ** n * r ** k:
            d += 1
        starts.append(d)
    return starts


def _rel_bucket(dist):
    max_exact = N_BUCKETS // 2
    large = max_exact
    for start in _bucket_starts():
        large = large + (dist >= start).astype(jnp.int32)
    return jnp.where(dist < max_exact, dist, large)


def _bias_kernel(rb_ref, o_ref):
    h = pl.program_id(0)
    tk, tq = o_ref.shape
    s = lax.broadcasted_iota(jnp.int32, (tk, tq), 0)
    t = lax.broadcasted_iota(jnp.int32, (tk, tq), 1)
    bucket = _rel_bucket(jnp.maximum(t - s + tq, 0))
    val = jnp.zeros((tk, tq), F32)
    for b in range(N_BUCKETS):
        val = jnp.where(bucket == b, rb_ref[b, h], val)
    o_ref[...] = (val - rb_ref[N_BUCKETS - 1, h]) * LOG2E


def _bias_tiles(rel_bias, tq):
    return pl.pallas_call(
        _bias_kernel,
        grid=(N_HEADS,),
        in_specs=[pl.BlockSpec(memory_space=pltpu.SMEM)],
        out_specs=pl.BlockSpec((None, 2 * tq, tq), lambda h: (h, 0, 0)),
        out_shape=jax.ShapeDtypeStruct((N_HEADS, 2 * tq, tq), F32),
        compiler_params=pltpu.CompilerParams(dimension_semantics=("arbitrary",)),
        name="bias_tiles",
    )(rel_bias)


def _dsa_kernel(qit_ref, wit_ref, qat_ref, ki_ref, ckv_ref, ckvt_ref, bias_ref, wuvt_ref, gout_ref,
                o_ref, sc_ref, sch_ref, mask_ref, acc_ref, m_ref, *, n_keep):
    nbat, tq = o_ref.shape[0], o_ref.shape[1]
    tw = nbat * tq
    i = pl.program_id(1)
    nblk = i + 1
    q0 = i * tq
    kf = float(n_keep)
    key_in_blk = lax.broadcasted_iota(jnp.int32, (tq, tq), 0)
    qry_in_blk = lax.broadcasted_iota(jnp.int32, (tq, tq), 1)
    ind = lambda mask: jnp.where(mask, 1.0, 0.0)
    batches = lambda f: jnp.concatenate([f(g) for g in range(nbat)], axis=1)
    part = lambda a: a.reshape(tq // (SWEEP_ACCS * SUBLANES), SWEEP_ACCS, SUBLANES, tw)
    fold = lambda a: part(a).sum(axis=0)
    fold_max = lambda a: part(a).max(axis=0)
    total = lambda a: a.sum(axis=0).sum(axis=0, keepdims=True)
    total_max = lambda a: a.max(axis=0).max(axis=0, keepdims=True)
    zeros8 = jnp.zeros((SWEEP_ACCS, SUBLANES, tw), F32)
    ninf8 = jnp.full((SWEEP_ACCS, SUBLANES, tw), -jnp.inf, F32)

    def score_body(j, carry):
        mn, mx = carry
        k0 = pl.multiple_of(j * tq, tq)
        causal1 = (k0 + key_in_blk) <= (q0 + qry_in_blk)

        def one(g):
            kib = ki_ref[g, pl.ds(k0, tq), :]
            acc = jnp.zeros((tq, tq), F32)
            for h in range(N_HEADS_IDX):
                d = jnp.dot(kib, qit_ref[g, h * IDX_DIM:(h + 1) * IDX_DIM, :],
                            preferred_element_type=F32)
                acc = acc + jnp.maximum(d, 0.0) * wit_ref[g, h:h + 1, :]
            return acc
        acc = batches(one)
        causal = batches(lambda g: causal1)
        masked = jnp.where(causal, acc, -jnp.inf)
        sc_ref[j] = masked
        sch_ref[j] = masked.astype(BF16)
        mx = jnp.maximum(mx, jnp.max(masked, axis=0, keepdims=True))
        mn = jnp.minimum(mn, jnp.min(jnp.where(causal, acc, jnp.inf), axis=0, keepdims=True))
        return mn, mx

    def over_blocks(body, init):
        def pair(pi, c):
            return body(2 * pi + 1, body(2 * pi, c))
        c = lax.fori_loop(0, lax.shift_right_logical(nblk, 1), pair, init)
        return lax.cond((nblk & 1) == 1, lambda c: body(nblk - 1, c), lambda c: c, c)

    mn, mx = over_blocks(score_body,
                         (jnp.full((1, tw), jnp.inf, F32), jnp.full((1, tw), -jnp.inf, F32)))

    def count(bound, strict=False):
        def one(j, g):
            cols = slice(g * tq, (g + 1) * tq)
            blk = sc_ref[j, :, cols]
            hit = blk > bound[:, cols] if strict else blk >= bound[:, cols]
            return ind(hit).reshape(tq // (SWEEP_ACCS * SUBLANES), SWEEP_ACCS, SUBLANES, tq).sum(axis=0)
        a8 = lax.fori_loop(
            0, nblk,
            lambda j, a: a + jnp.concatenate([one(j, g) for g in range(nbat)], axis=-1), zeros8)
        return total(a8)

    def any_set(flag):
        return jnp.max(flag) > 0.0

    c_ge0 = count(jnp.zeros((1, tw), F32))
    c_gt0 = count(jnp.zeros((1, tw), F32), strict=True)
    ncausal = batches(lambda g: (q0 + lax.broadcasted_iota(jnp.int32, (1, tq), 1) + 1).astype(F32))
    small = ncausal <= kf
    neg = c_ge0 < kf
    zero_tie = (c_gt0 < kf) & ~neg
    lo = jnp.where(small | neg, mn, 0.0)
    clo = jnp.where(small | neg, ncausal, c_ge0)
    above_mx = mx + jnp.abs(mx) * 2.0 ** -22 + 1e-37
    hi = jnp.where(neg, 0.0, above_mx)
    searching = ~small & ~zero_tie

    def bis_cond(st):
        return (st[0] < MAX_BISECT) & st[1]

    def bis_body(st):
        it, _, lo, hi, clo, active = st
        for _ in range(BISECT_CHUNK):
            mid = lo + (hi - lo) * 0.5
            live = (active > 0.0) & (mid > lo) & (mid < hi)
            c = count(mid)
            up = live & (c >= kf)
            lo = jnp.where(up, mid, lo)
            clo = jnp.where(up, c, clo)
            hi = jnp.where(live & (c < kf), mid, hi)
            active = ind(live & (clo > kf))
        return it + 1, any_set(active), lo, hi, clo, active

    rb = lambda v: v.astype(BF16).astype(F32)

    def count16(bound):
        bb = bound.astype(BF16)

        def tile(j, g):
            cols = slice(g * tq, (g + 1) * tq)
            hit = sch_ref[j, :, cols] >= bb[:, cols]
            v = jnp.where(hit, jnp.ones((tq, tq), BF16), jnp.zeros((tq, tq), BF16))
            v = v.reshape(tq // (SWEEP_ACCS * BF16_ROWS), SWEEP_ACCS, BF16_ROWS, tq)
            acc = v[0]
            for k in range(1, v.shape[0]):
                acc = acc + v[k]
            return acc
        a = lax.fori_loop(
            0, nblk,
            lambda j, a: a + jnp.concatenate([tile(j, g) for g in range(nbat)], axis=-1),
            jnp.zeros((SWEEP_ACCS, BF16_ROWS, tw), BF16))
        return a.astype(F32).sum(axis=0).sum(axis=0, keepdims=True)

    def coarse_body(st):
        it, _, lo_c, hi_c, active = st
        for _ in range(BISECT_CHUNK):
            mid = rb(lo_c + (hi_c - lo_c) * 0.5)
            live = (active > 0.0) & (mid > lo_c) & (mid < hi_c)
            c = count16(mid)
            lo_c = jnp.where(live & (c >= kf), mid, lo_c)
            hi_c = jnp.where(live & (c < kf), mid, hi_c)
            active = ind(live)
        return it + 1, any_set(active), lo_c, hi_c, active

    active = ind(searching & (clo > kf))
    st = lax.while_loop(bis_cond, coarse_body,
                        (jnp.int32(0), any_set(active), rb(lo), rb(hi * 1.02 + 1e-30), active))
    _, _, lo_c, hi_c, _ = st
    lo_f = lo_c - jnp.abs(lo_c) * 2.0 ** -6 - 1e-30
    hi_f = hi_c + jnp.abs(hi_c) * 2.0 ** -6 + 1e-30
    clo_f = count(lo_f)
    chi_f = count(hi_f)
    use_lo = searching & (clo_f >= kf) & (lo_f > lo)
    use_hi = searching & (chi_f < kf) & (hi_f < hi)
    lo = jnp.where(use_lo, lo_f, lo)
    clo = jnp.where(use_lo, clo_f, clo)
    hi = jnp.where(use_hi, hi_f, hi)

    active = ind(searching & (clo > kf))
    st = lax.while_loop(bis_cond, bis_body, (jnp.int32(0), any_set(active), lo, hi, clo, active))
    _, _, lo, hi, clo, _ = st

    unres = ind(searching & (clo > kf))

    def peel_cond(st):
        return (st[0] < MAX_PEEL) & st[1]

    def peel_body(st):
        it, _, lo, clo, cand, unres = st

        def body(j, s2):
            a8, m8 = s2
            blk = sc_ref[j]
            ge = blk >= cand
            below = jnp.where((blk >= lo) & ~ge, blk, -jnp.inf)
            return a8 + fold(ind(ge)), jnp.maximum(m8, fold_max(below))
        a8, m8 = lax.fori_loop(0, nblk, body, (zeros8, ninf8))
        c = total(a8)
        on = unres > 0.0
        found = on & (c >= kf)
        down = on & (c < kf)
        lo = jnp.where(found, cand, lo)
        clo = jnp.where(found, c, clo)
        cand = jnp.where(down, total_max(m8), cand)
        return it + 1, any_set(ind(down)), lo, clo, cand, ind(down)

    st = lax.while_loop(peel_cond, peel_body, (jnp.int32(0), any_set(unres), lo, clo, hi, unres))
    _, _, lo, clo, _, _ = st

    tied = any_set(ind(~small & (clo > kf)))

    @pl.when(jnp.logical_not(tied))
    def _():
        def body(j, c):
            mask_ref[j] = jnp.where(sc_ref[j] >= lo, 0.0, -jnp.inf)
            return c
        lax.fori_loop(0, nblk, body, 0)

    @pl.when(tied)
    def _():
        need = kf - count(lo, strict=True)
        earlier = ind(qry_in_blk < key_in_blk).astype(BF16)

        def body(j, seen):
            blk = sc_ref[j]
            eq = ind(blk == lo)
            rank = seen + jnp.dot(earlier, eq.astype(BF16), preferred_element_type=F32)
            keep = (blk > lo) | ((blk == lo) & (rank < need))
            mask_ref[j] = jnp.where(keep, 0.0, -jnp.inf)
            return seen + total(fold(eq))
        over_blocks(body, jnp.zeros((1, tw), F32))

    m_ref[...] = jnp.full(m_ref.shape, -jnp.inf, F32)
    acc_ref[...] = jnp.zeros(acc_ref.shape, F32)

    def attend(j, nkb, bias):
        tk = nkb * tq
        k0 = pl.multiple_of(j * tq, tq)
        heads = lambda f: jnp.concatenate([f(h) for h in range(N_HEADS)], axis=1)
        for g in range(nbat):
            cols = slice(g * tq, (g + 1) * tq)
            kv = ckv_ref[g, pl.ds(k0, tk), :]
            kvt1 = jnp.concatenate([ckvt_ref[g, j + b] for b in range(nkb)], axis=1)
            mask = jnp.concatenate([mask_ref[j + b, :, cols] for b in range(nkb)], axis=0)
            qa = heads(lambda h: qat_ref[g, h * KV_LATENT:(h + 1) * KV_LATENT, :])
            s = jnp.dot(kv, qa, preferred_element_type=F32)
            if bias is not None:
                s = s + heads(bias)
            s = s + heads(lambda h: mask)
            m_prev = m_ref[g]
            m_new = jnp.maximum(m_prev, jnp.max(s, axis=0, keepdims=True))
            m_safe = jnp.where(m_new == -jnp.inf, 0.0, m_new)
            alpha = jnp.exp2(m_prev - m_safe)
            p = jnp.exp2(s - m_safe).astype(BF16)
            m_ref[g] = m_new
            pv = jnp.dot(kvt1, p, preferred_element_type=F32)
            acc_ref[g] = alpha * acc_ref[g] + pv

    nfar = jnp.maximum(i - 1, 0)

    def far_pair(pi, c):
        attend(2 * pi, 2, None)
        return c
    lax.fori_loop(0, lax.shift_right_logical(nfar, 1), far_pair, 0)

    @pl.when((nfar & 1) == 1)
    def _():
        attend(nfar - 1, 1, None)

    @pl.when(i >= 1)
    def _():
        attend(i - 1, 2, lambda h: bias_ref[h])

    @pl.when(i == 0)
    def _():
        attend(0, 1, lambda h: bias_ref[h, tq:, :])

    for g in range(nbat):
        outs = []
        for h in range(N_HEADS):
            a = acc_ref[g, :, h * tq:(h + 1) * tq]
            ctx = (a[:KV_LATENT, :] / a[KV_LATENT:KV_LATENT + 1, :]).astype(BF16)
            outs.append(jnp.dot(wuvt_ref[h], ctx, preferred_element_type=F32))
        out = jnp.concatenate(outs, axis=0).T
        o_ref[g] = _rms(out, gout_ref[...]).astype(BF16)


def _dsa(qit, wit, qat, ki, ckv, ckvt, bias, wuvt, gout, n_keep):
    b, s, _ = ckv.shape
    tq = Q_BLOCK
    nb = s // tq
    nq = N_HEADS_IDX * IDX_DIM
    nbat = DSA_BATCH if b % DSA_BATCH == 0 else 1
    colblk = lambda r: pl.BlockSpec((nbat, r, tq), lambda bi, i: (bi, 0, i))
    return pl.pallas_call(
        functools.partial(_dsa_kernel, n_keep=n_keep),
        grid=(b // nbat, nb),
        in_specs=[
            colblk(nq), colblk(N_HEADS_IDX), colblk(N_HEADS * KV_LATENT),
            pl.BlockSpec((nbat, s, IDX_DIM), lambda bi, i: (bi, 0, 0)),
            pl.BlockSpec((nbat, s, KV_LATENT), lambda bi, i: (bi, 0, 0)),
            pl.BlockSpec((nbat, nb, KVT_ROWS, tq), lambda bi, i: (bi, 0, 0, 0)),
            _const_spec(bias.shape), _const_spec(wuvt.shape), _const_spec(gout.shape),
        ],
        out_specs=pl.BlockSpec((nbat, tq, ATTN_WIDTH), lambda bi, i: (bi, i, 0)),
        out_shape=jax.ShapeDtypeStruct((b, s, ATTN_WIDTH), BF16),
        scratch_shapes=[
            pltpu.VMEM((nb, tq, nbat * tq), F32),
            pltpu.VMEM((nb, tq, nbat * tq), BF16),
            pltpu.VMEM((nb, tq, nbat * tq), F32),
            pltpu.VMEM((nbat, KVT_ROWS, N_HEADS * tq), F32),
            pltpu.VMEM((nbat, 1, N_HEADS * tq), F32),
        ],
        compiler_params=pltpu.CompilerParams(dimension_semantics=("arbitrary", "arbitrary"),
                                             vmem_limit_bytes=VMEM_LIMIT),
        name="dsa",
    )(qit, wit, qat, ki, ckv, ckvt, bias, wuvt, gout)


def _cast_kernel(*refs):
    n = len(refs) // 2
    for src, dst in zip(refs[:n], refs[n:]):
        dst[...] = src[...].astype(BF16)


def _to_bf16(*ws):
    r, c = ws[0].shape
    spec = pl.BlockSpec((r // CAST_STEPS, c), lambda i: (i, 0))
    return pl.pallas_call(
        _cast_kernel,
        grid=(CAST_STEPS,),
        in_specs=[spec] * len(ws),
        out_specs=[spec] * len(ws),
        out_shape=[jax.ShapeDtypeStruct(w.shape, BF16) for w in ws],
        compiler_params=pltpu.CompilerParams(dimension_semantics=("arbitrary",),
                                             vmem_limit_bytes=VMEM_LIMIT),
        name="to_bf16",
    )(*ws)


def kernel(x, ffn1_norm, ffn1_w_gate, ffn1_w_up, ffn1_w_down, mix_norm, w_in, kv_norm, w_uk, w_uv, rel_bias, conv_w, attn_out_norm, conv_out_norm, w_out, ffn2_norm, ffn2_w_gate, ffn2_w_up, ffn2_w_down, final_norm):
    b, s, d = x.shape
    depth = ffn1_norm.shape[0]
    n_keep = min(TOPK_MAX, s // 4)
    assert s % Q_BLOCK == 0 and s % PROJ_TOKENS == 0 and (b * s) % FFN_TOKENS == 0
    assert Q_BLOCK >= MAX_DISTANCE and n_keep <= Q_BLOCK
    row = lambda v: v.reshape(1, -1).astype(F32)
    offs = [0]
    for w in IN_SIZES:
        offs.append(offs[-1] + w)

    bias = _bias_tiles(rel_bias.astype(F32), Q_BLOCK)
    h = x.astype(F32).reshape(b * s, d)
    for l in range(depth):
        wg1, wu1, wg2, wu2 = _to_bf16(ffn1_w_gate[l], ffn1_w_up[l], ffn2_w_gate[l], ffn2_w_up[l])
        wd1, wd2 = _to_bf16(ffn1_w_down[l], ffn2_w_down[l])
        h = _ffn1(h, row(ffn1_norm[l]), wg1, wu1, wd1)

        wi = w_in[l]
        col = lambda k: wi[:, offs[k]:offs[k + 1]]
        wcat = jnp.concatenate(
            [col(1), col(3), jnp.zeros((d, LANES - IDX_DIM), wi.dtype), col(5), col(6), col(7)],
            axis=1).astype(BF16)
        wt = jnp.concatenate(
            [col(0).T, col(1).T, col(2).T, col(4).T,
             jnp.zeros((_R_END - _R_WI - N_HEADS_IDX, d), wi.dtype)], axis=0).astype(BF16)
        wukt = jnp.transpose(w_uk[l], (1, 0, 2)).astype(BF16)
        wuvt = jnp.transpose(w_uv[l], (1, 2, 0)).astype(BF16)

        qat, ckv, ckvt, ki, qit, wit, convn = _proj(
            h.reshape(b, s, d), row(mix_norm[l]), wcat, wt, wukt, row(kv_norm[l]),
            conv_w[l].astype(F32), row(conv_out_norm[l]))
        attn = _dsa(qit, wit, qat, ki, ckv, ckvt, bias, wuvt, row(attn_out_norm[l]), n_keep)

        last = l == depth - 1
        h = _out_ffn2(h, attn.reshape(b * s, ATTN_WIDTH), convn.reshape(b * s, CONV_WIDTH),
                      w_out[l][:ATTN_WIDTH].astype(BF16), w_out[l][ATTN_WIDTH:].astype(BF16),
                      row(ffn2_norm[l]), wg2, wu2, wd2, row(final_norm), final_norm=last)
    return h.reshape(b, s, d).astype(x.dtype)
```

```python
import functools
import math

import jax
import jax.numpy as jnp
from jax import lax
from jax.experimental import pallas as pl
from jax.experimental.pallas import tpu as pltpu

F32 = jnp.float32
BF16 = jnp.bfloat16

EPS = 1e-6
N_HEADS = 8
HEAD_DIM = 64
ATTN_WIDTH = N_HEADS * HEAD_DIM
KV_LATENT = 128
N_HEADS_IDX = 8
IDX_DIM = 64
TOPK_MAX = 256
CONV_WIDTH = 512
CONV_K = 3
N_BUCKETS = 32
MAX_DISTANCE = 128
IN_SIZES = (ATTN_WIDTH, KV_LATENT, N_HEADS_IDX * IDX_DIM, IDX_DIM, N_HEADS_IDX,
            CONV_WIDTH, CONV_WIDTH, CONV_WIDTH)

SUBLANES = 8
LANES = 128

FFN_TOKENS = 512
FFN_CHUNK = 256
PROJ_TOKENS = 512
Q_BLOCK = 256
BISECT_CHUNK = 4
MAX_BISECT = 64
CAST_STEPS = 8
DSA_BATCH = 2
SWEEP_ACCS = 2
MAX_PEEL = 4096
LOG2E = math.log2(math.e)
VMEM_LIMIT = 56 * 1024 * 1024


def _rms(x, g):
    return x * lax.rsqrt(jnp.mean(x * x, axis=-1, keepdims=True) + EPS) * g


def _const_spec(shape):
    nd = len(shape)
    return pl.BlockSpec(shape, lambda *_: (0,) * nd, pipeline_mode=pl.Buffered(1))


def _swiglu_into(acc_ref, xn, wg_ref, wu_ref, wd_ref):
    d_ff = wg_ref.shape[1]
    for c in range(d_ff // FFN_CHUNK):
        sl = slice(c * FFN_CHUNK, (c + 1) * FFN_CHUNK)
        g = jnp.dot(xn, wg_ref[:, sl], preferred_element_type=F32)
        u = jnp.dot(xn, wu_ref[:, sl], preferred_element_type=F32)
        a = (g * jax.nn.sigmoid(g) * u).astype(BF16)
        d = jnp.dot(a, wd_ref[sl, :], preferred_element_type=F32)
        if c == 0:
            acc_ref[...] = d
        else:
            acc_ref[...] += d


def _ffn1_kernel(x_ref, g_ref, wg_ref, wu_ref, wd_ref, o_ref, acc_ref):
    x = x_ref[...]
    xn = _rms(x, g_ref[...]).astype(BF16)
    _swiglu_into(acc_ref, xn, wg_ref, wu_ref, wd_ref)
    o_ref[...] = x + 0.5 * acc_ref[...]


def _out_ffn2_kernel(h_ref, a_ref, c_ref, woa_ref, woc_ref, g_ref, wg_ref, wu_ref, wd_ref,
                     gf_ref, o_ref, acc_ref, *, final_norm):
    h = (h_ref[...]
         + jnp.dot(a_ref[...], woa_ref[...], preferred_element_type=F32)
         + jnp.dot(c_ref[...], woc_ref[...], preferred_element_type=F32))
    xn = _rms(h, g_ref[...]).astype(BF16)
    _swiglu_into(acc_ref, xn, wg_ref, wu_ref, wd_ref)
    h = h + 0.5 * acc_ref[...]
    o_ref[...] = _rms(h, gf_ref[...]) if final_norm else h


def _ffn1(x2, g, wg, wu, wd):
    n, d = x2.shape
    tile = pl.BlockSpec((FFN_TOKENS, d), lambda i: (i, 0))
    return pl.pallas_call(
        _ffn1_kernel,
        grid=(n // FFN_TOKENS,),
        in_specs=[tile, _const_spec(g.shape), _const_spec(wg.shape), _const_spec(wu.shape),
                  _const_spec(wd.shape)],
        out_specs=tile,
        out_shape=jax.ShapeDtypeStruct((n, d), F32),
        scratch_shapes=[pltpu.VMEM((FFN_TOKENS, d), F32)],
        compiler_params=pltpu.CompilerParams(dimension_semantics=("arbitrary",),
                                             vmem_limit_bytes=VMEM_LIMIT),
        name="ffn1",
    )(x2, g, wg, wu, wd)


def _out_ffn2(h2, a2, c2, woa, woc, g, wg, wu, wd, gf, final_norm):
    n, d = h2.shape
    tile = pl.BlockSpec((FFN_TOKENS, d), lambda i: (i, 0))
    mix = pl.BlockSpec((FFN_TOKENS, a2.shape[1]), lambda i: (i, 0))
    return pl.pallas_call(
        functools.partial(_out_ffn2_kernel, final_norm=final_norm),
        grid=(n // FFN_TOKENS,),
        in_specs=[tile, mix, mix, _const_spec(woa.shape), _const_spec(woc.shape),
                  _const_spec(g.shape), _const_spec(wg.shape), _const_spec(wu.shape),
                  _const_spec(wd.shape), _const_spec(gf.shape)],
        out_specs=tile,
        out_shape=jax.ShapeDtypeStruct((n, d), F32),
        scratch_shapes=[pltpu.VMEM((FFN_TOKENS, d), F32)],
        compiler_params=pltpu.CompilerParams(dimension_semantics=("arbitrary",),
                                             vmem_limit_bytes=VMEM_LIMIT),
        name="out_ffn2",
    )(h2, a2, c2, woa, woc, g, wg, wu, wd, gf)


_C_KV = 0
_C_KI = _C_KV + KV_LATENT
_C_GB = _C_KI + LANES
_C_GC = _C_GB + CONV_WIDTH
_C_X = _C_GC + CONV_WIDTH
_C_END = _C_X + CONV_WIDTH
_R_Q = 0
_R_KV = _R_Q + ATTN_WIDTH
_R_QI = _R_KV + KV_LATENT
_R_WI = _R_QI + N_HEADS_IDX * IDX_DIM
_R_END = _R_WI + 2 * SUBLANES
KVT_ROWS = KV_LATENT + 2 * SUBLANES


def _proj_kernel(h_ref, gm_ref, wcat_ref, wt_ref, wukt_ref, gkv_ref, gkvc_ref, cw_ref, gco_ref,
                 qat_ref, ckv_ref, ckvt_ref, ki_ref, qit_ref, wit_ref, convn_ref, carry_ref):
    tp = h_ref.shape[0]

    @pl.when(pl.program_id(1) == 0)
    def _():
        carry_ref[...] = jnp.zeros_like(carry_ref)

    un = _rms(h_ref[...], gm_ref[...]).astype(BF16)
    p = jnp.dot(un, wcat_ref[...], preferred_element_type=F32)
    ckv_ref[...] = _rms(p[:, _C_KV:_C_KI], gkv_ref[...]).astype(BF16)
    ki_ref[...] = p[:, _C_KI:_C_KI + IDX_DIM].astype(BF16)

    v = p[:, _C_GC:_C_X] * p[:, _C_X:_C_END]
    row = lax.broadcasted_iota(jnp.int32, v.shape, 0)
    prev1 = jnp.broadcast_to(carry_ref[SUBLANES - 1:SUBLANES, :], v.shape)
    prev2 = jnp.broadcast_to(carry_ref[SUBLANES - 2:SUBLANES - 1, :], v.shape)
    v1 = jnp.where(row == 0, prev1, pltpu.roll(v, 1, 0))
    v2 = jnp.where(row == 0, prev2, jnp.where(row == 1, prev1, pltpu.roll(v, 2, 0)))
    carry_ref[...] = v[tp - SUBLANES:, :]
    y = cw_ref[0:1, :] * v2 + cw_ref[1:2, :] * v1 + cw_ref[2:3, :] * v
    conv = p[:, _C_GB:_C_GC] * y
    convn_ref[...] = _rms(conv, gco_ref[...]).astype(BF16)

    pt = lax.dot_general(wt_ref[...], un, (((1,), (1,)), ((), ())),
                         preferred_element_type=F32)
    qt = pt[_R_Q:_R_KV, :].astype(BF16)
    for h in range(N_HEADS):
        qah = jnp.dot(wukt_ref[h], qt[h * HEAD_DIM:(h + 1) * HEAD_DIM, :],
                      preferred_element_type=F32)
        qat_ref[h * KV_LATENT:(h + 1) * KV_LATENT, :] = (qah * (HEAD_DIM ** -0.5 * LOG2E)).astype(BF16)
    kvt = pt[_R_KV:_R_QI, :]
    kvt = kvt * lax.rsqrt(jnp.mean(kvt * kvt, axis=0, keepdims=True) + EPS) * gkvc_ref[...]
    tk = ckvt_ref.shape[2]
    tail = jnp.concatenate([jnp.ones((SUBLANES, tk), F32), jnp.zeros((SUBLANES, tk), F32)], axis=0)
    for kb in range(tp // tk):
        blk = jnp.concatenate([kvt[:, kb * tk:(kb + 1) * tk], tail], axis=0)
        ckvt_ref[kb] = blk.astype(BF16)
    qit_ref[...] = pt[_R_QI:_R_WI, :].astype(BF16)
    wit_ref[...] = pt[_R_WI:_R_WI + N_HEADS_IDX, :] * ((N_HEADS_IDX * IDX_DIM) ** -0.5)


def _proj(h1, gm, wcat, wt, wukt, gkv, cw, gco):
    b, s, d = h1.shape
    tp = PROJ_TOKENS
    nq = N_HEADS_IDX * IDX_DIM
    nqa = N_HEADS * KV_LATENT
    gkvc = gkv.reshape(KV_LATENT, 1)
    tok = lambda w: pl.BlockSpec((None, tp, w), lambda bi, ti: (bi, ti, 0))
    tokt = lambda r: pl.BlockSpec((None, r, tp), lambda bi, ti: (bi, 0, ti))
    out_shape = (
        jax.ShapeDtypeStruct((b, nqa, s), BF16),
        jax.ShapeDtypeStruct((b, s, KV_LATENT), BF16),
        jax.ShapeDtypeStruct((b, s // Q_BLOCK, KVT_ROWS, Q_BLOCK), BF16),
        jax.ShapeDtypeStruct((b, s, IDX_DIM), BF16),
        jax.ShapeDtypeStruct((b, nq, s), BF16),
        jax.ShapeDtypeStruct((b, N_HEADS_IDX, s), F32),
        jax.ShapeDtypeStruct((b, s, CONV_WIDTH), BF16),
    )
    out_specs = (
        tokt(nqa), tok(KV_LATENT),
        pl.BlockSpec((None, tp // Q_BLOCK, KVT_ROWS, Q_BLOCK), lambda bi, ti: (bi, ti, 0, 0)),
        tok(IDX_DIM), tokt(nq), tokt(N_HEADS_IDX), tok(CONV_WIDTH),
    )
    return pl.pallas_call(
        _proj_kernel,
        grid=(b, s // tp),
        in_specs=[tok(d), _const_spec(gm.shape), _const_spec(wcat.shape), _const_spec(wt.shape),
                  _const_spec(wukt.shape), _const_spec(gkv.shape), _const_spec(gkvc.shape),
                  _const_spec(cw.shape), _const_spec(gco.shape)],
        out_specs=out_specs,
        out_shape=out_shape,
        scratch_shapes=[pltpu.VMEM((SUBLANES, CONV_WIDTH), F32)],
        compiler_params=pltpu.CompilerParams(dimension_semantics=("arbitrary", "arbitrary"),
                                             vmem_limit_bytes=VMEM_LIMIT),
        name="proj",
    )(h1, gm, wcat, wt, wukt, gkv, gkvc, cw, gco)


def _bucket_starts():
    e = N_BUCKETS // 2
    n = N_BUCKETS - e
    assert MAX_DISTANCE % e == 0
    r = MAX_DISTANCE // e
    starts, d = [], e
    for k in range(1, n):
        while d ** n < e ** n * r ** k:
            d += 1
        starts.append(d)
    return starts


def _rel_bucket(dist):
    max_exact = N_BUCKETS // 2
    large = max_exact
    for start in _bucket_starts():
        large = large + (dist >= start).astype(jnp.int32)
    return jnp.where(dist < max_exact, dist, large)


def _bias_kernel(rb_ref, o_ref):
    h = pl.program_id(0)
    tk, tq = o_ref.shape
    s = lax.broadcasted_iota(jnp.int32, (tk, tq), 0)
    t = lax.broadcasted_iota(jnp.int32, (tk, tq), 1)
    bucket = _rel_bucket(jnp.maximum(t - s + 2 * tq, 0))
    val = jnp.zeros((tk, tq), F32)
    for b in range(N_BUCKETS):
        val = jnp.where(bucket == b, rb_ref[b, h], val)
    o_ref[...] = (val - rb_ref[N_BUCKETS - 1, h]) * LOG2E


def _bias_tiles(rel_bias, tq):
    return pl.pallas_call(
        _bias_kernel,
        grid=(N_HEADS,),
        in_specs=[pl.BlockSpec(memory_space=pltpu.SMEM)],
        out_specs=pl.BlockSpec((None, 4 * tq, tq), lambda h: (h, 0, 0)),
        out_shape=jax.ShapeDtypeStruct((N_HEADS, 4 * tq, tq), F32),
        compiler_params=pltpu.CompilerParams(dimension_semantics=("arbitrary",)),
        name="bias_tiles",
    )(rel_bias)


def _dsa_kernel(qit_ref, wit_ref, qat_ref, ki_ref, ckv_ref, ckvt_ref, bias_ref, wuvt_ref, gout_ref,
                o_ref, sc_ref, mask_ref, acc_ref, m_ref, s_ref, cmax_ref, *, n_keep):
    nbat, tq = o_ref.shape[0], o_ref.shape[1]
    tw = nbat * tq
    i = pl.program_id(1)
    nblk = i + 1
    q0 = i * tq
    kf = float(n_keep)
    key_in_blk = lax.broadcasted_iota(jnp.int32, (tq, tq), 0)
    qry_in_blk = lax.broadcasted_iota(jnp.int32, (tq, tq), 1)
    ind = lambda mask: jnp.where(mask, 1.0, 0.0)
    batches = lambda f: jnp.concatenate([f(g) for g in range(nbat)], axis=1)
    part = lambda a: a.reshape(tq // (SWEEP_ACCS * SUBLANES), SWEEP_ACCS, SUBLANES, tw)
    fold = lambda a: part(a).sum(axis=0)
    fold_max = lambda a: part(a).max(axis=0)
    total = lambda a: a.sum(axis=0).sum(axis=0, keepdims=True)
    total_max = lambda a: a.max(axis=0).max(axis=0, keepdims=True)
    zeros8 = jnp.zeros((SWEEP_ACCS, SUBLANES, tw), F32)
    ninf8 = jnp.full((SWEEP_ACCS, SUBLANES, tw), -jnp.inf, F32)

    def score_body(j, carry):
        mn, mx = carry
        k0 = pl.multiple_of(j * tq, tq)
        causal1 = (k0 + key_in_blk) <= (q0 + qry_in_blk)

        def one(g):
            kib = ki_ref[g, pl.ds(k0, tq), :]
            acc = jnp.zeros((tq, tq), F32)
            for h in range(N_HEADS_IDX):
                d = jnp.dot(kib, qit_ref[g, h * IDX_DIM:(h + 1) * IDX_DIM, :],
                            preferred_element_type=F32)
                acc = acc + jnp.maximum(d, 0.0) * wit_ref[g, h:h + 1, :]
            return acc
        acc = batches(one)
        causal = batches(lambda g: causal1)
        masked = jnp.where(causal, acc, -jnp.inf)
        sc_ref[j] = masked
        mx = jnp.maximum(mx, jnp.max(masked, axis=0, keepdims=True))
        mn = jnp.minimum(mn, jnp.min(jnp.where(causal, acc, jnp.inf), axis=0, keepdims=True))
        return mn, mx

    def over_blocks(body, init):
        def pair(pi, c):
            return body(2 * pi + 1, body(2 * pi, c))
        c = lax.fori_loop(0, lax.shift_right_logical(nblk, 1), pair, init)
        return lax.cond((nblk & 1) == 1, lambda c: body(nblk - 1, c), lambda c: c, c)

    mn, mx = over_blocks(score_body,
                         (jnp.full((1, tw), jnp.inf, F32), jnp.full((1, tw), -jnp.inf, F32)))

    def count(bound, strict=False):
        def one(j, g):
            cols = slice(g * tq, (g + 1) * tq)
            blk = sc_ref[j, :, cols]
            hit = blk > bound[:, cols] if strict else blk >= bound[:, cols]
            return ind(hit).reshape(tq // (SWEEP_ACCS * SUBLANES), SWEEP_ACCS, SUBLANES, tq).sum(axis=0)
        a8 = lax.fori_loop(
            0, nblk,
            lambda j, a: a + jnp.concatenate([one(j, g) for g in range(nbat)], axis=-1), zeros8)
        return total(a8)

    def any_set(flag):
        return jnp.max(flag) > 0.0

    c_ge0 = count(jnp.zeros((1, tw), F32))
    c_gt0 = count(jnp.zeros((1, tw), F32), strict=True)
    ncausal = batches(lambda g: (q0 + lax.broadcasted_iota(jnp.int32, (1, tq), 1) + 1).astype(F32))
    small = ncausal <= kf
    neg = c_ge0 < kf
    zero_tie = (c_gt0 < kf) & ~neg
    lo = jnp.where(small | neg, mn, 0.0)
    clo = jnp.where(small | neg, ncausal, c_ge0)
    above_mx = mx + jnp.abs(mx) * 2.0 ** -22 + 1e-37
    hi = jnp.where(neg, 0.0, above_mx)
    searching = ~small & ~zero_tie

    def bis_cond(st):
        return (st[0] < MAX_BISECT) & st[1]

    def bis_body(st):
        it, _, lo, hi, clo, active = st
        for _ in range(BISECT_CHUNK):
            mid = lo + (hi - lo) * 0.5
            live = (active > 0.0) & (mid > lo) & (mid < hi)
            c = count(mid)
            up = live & (c >= kf)
            lo = jnp.where(up, mid, lo)
            clo = jnp.where(up, c, clo)
            hi = jnp.where(live & (c < kf), mid, hi)
            active = ind(live & (clo > kf))
        return it + 1, any_set(active), lo, hi, clo, active

    active = ind(searching & (clo > kf))
    st = lax.while_loop(bis_cond, bis_body, (jnp.int32(0), any_set(active), lo, hi, clo, active))
    _, _, lo, hi, clo, _ = st

    unres = ind(searching & (clo > kf))

    def peel_cond(st):
        return (st[0] < MAX_PEEL) & st[1]

    def peel_body(st):
        it, _, lo, clo, cand, unres = st

        def body(j, s2):
            a8, m8 = s2
            blk = sc_ref[j]
            ge = blk >= cand
            below = jnp.where((blk >= lo) & ~ge, blk, -jnp.inf)
            return a8 + fold(ind(ge)), jnp.maximum(m8, fold_max(below))
        a8, m8 = lax.fori_loop(0, nblk, body, (zeros8, ninf8))
        c = total(a8)
        on = unres > 0.0
        found = on & (c >= kf)
        down = on & (c < kf)
        lo = jnp.where(found, cand, lo)
        clo = jnp.where(found, c, clo)
        cand = jnp.where(down, total_max(m8), cand)
        return it + 1, any_set(ind(down)), lo, clo, cand, ind(down)

    st = lax.while_loop(peel_cond, peel_body, (jnp.int32(0), any_set(unres), lo, clo, hi, unres))
    _, _, lo, clo, _, _ = st

    tied = any_set(ind(~small & (clo > kf)))

    @pl.when(jnp.logical_not(tied))
    def _():
        def body(j, c):
            mask_ref[j] = jnp.where(sc_ref[j] >= lo, 0.0, -jnp.inf)
            return c
        lax.fori_loop(0, nblk, body, 0)

    @pl.when(tied)
    def _():
        need = kf - count(lo, strict=True)
        earlier = ind(qry_in_blk < key_in_blk).astype(BF16)

        def body(j, seen):
            blk = sc_ref[j]
            eq = ind(blk == lo)
            rank = seen + jnp.dot(earlier, eq.astype(BF16), preferred_element_type=F32)
            keep = (blk > lo) | ((blk == lo) & (rank < need))
            mask_ref[j] = jnp.where(keep, 0.0, -jnp.inf)
            return seen + total(fold(eq))
        over_blocks(body, jnp.zeros((1, tw), F32))

    m_ref[...] = jnp.full(m_ref.shape, -jnp.inf, F32)
    acc_ref[...] = jnp.zeros(acc_ref.shape, F32)

    tk = 2 * tq
    npc = lax.shift_right_logical(nblk + 1, 1)

    @pl.when((nblk & 1) == 1)
    def _():
        mask_ref[nblk] = jnp.full((tq, tw), -jnp.inf, F32)

    heads = lambda f: jnp.concatenate([f(h) for h in range(N_HEADS)], axis=1)
    bias_row = lambda c: pl.multiple_of((2 * c - i + 2) * tq, tq)
    nbias = jnp.where((nblk & 1) == 0, 1, jnp.minimum(2, npc))

    def stage_a(c, slot, biased):
        j = 2 * c
        k0 = pl.multiple_of(c * tk, tk)
        for g in range(nbat):
            cols = slice(g * tq, (g + 1) * tq)
            qa = heads(lambda h: qat_ref[g, h * KV_LATENT:(h + 1) * KV_LATENT, :])
            s = jnp.dot(ckv_ref[g, pl.ds(k0, tk), :], qa, preferred_element_type=F32)
            if biased:
                s = s + heads(lambda h: bias_ref[h, pl.ds(bias_row(c), tk), :])
            mask = jnp.concatenate([mask_ref[j, :, cols], mask_ref[j + 1, :, cols]], axis=0)
            s = s + heads(lambda h: mask)
            s_ref[slot, g] = s
            cmax_ref[slot, g] = jnp.max(s, axis=0, keepdims=True)

    def stage_b(c, slot):
        j = 2 * c
        for g in range(nbat):
            kvt1 = jnp.concatenate([ckvt_ref[g, j], ckvt_ref[g, j + 1]], axis=1)
            m_prev = m_ref[g]
            m_new = jnp.maximum(m_prev, cmax_ref[slot, g])
            m_safe = jnp.where(m_new == -jnp.inf, 0.0, m_new)
            alpha = jnp.exp2(m_prev - m_safe)
            p = jnp.exp2(s_ref[slot, g] - m_safe).astype(BF16)
            m_ref[g] = m_new
            pv = jnp.dot(kvt1, p, preferred_element_type=F32)
            acc_ref[g] = alpha * acc_ref[g] + pv

    def by_parity(c, fn):
        for par in range(2):
            @pl.when((c & 1) == par)
            def _():
                fn(par)

    last = npc - 1
    by_parity(last, lambda par: stage_a(last, par, True))

    @pl.when(nbias == 2)
    def _():
        def fn(par):
            stage_a(last - 1, 1 - par, True)
            stage_b(last, par)
        by_parity(last, fn)

    def plain_step(n, carry):
        c = npc - nbias - n

        def fn(par):
            stage_a(c - 1, 1 - par, False)
            stage_b(c, par)
        by_parity(c, fn)
        return carry
    lax.fori_loop(0, npc - nbias, plain_step, 0)
    stage_b(0, 0)


    for g in range(nbat):
        outs = []
        for h in range(N_HEADS):
            a = acc_ref[g, :, h * tq:(h + 1) * tq]
            ctx = (a[:KV_LATENT, :] / a[KV_LATENT:KV_LATENT + 1, :]).astype(BF16)
            outs.append(jnp.dot(wuvt_ref[h], ctx, preferred_element_type=F32))
        out = jnp.concatenate(outs, axis=0).T
        o_ref[g] = _rms(out, gout_ref[...]).astype(BF16)


def _dsa(qit, wit, qat, ki, ckv, ckvt, bias, wuvt, gout, n_keep):
    b, s, _ = ckv.shape
    tq = Q_BLOCK
    nb = s // tq
    nq = N_HEADS_IDX * IDX_DIM
    nbat = DSA_BATCH if b % DSA_BATCH == 0 else 1
    colblk = lambda r: pl.BlockSpec((nbat, r, tq), lambda bi, i: (bi, 0, i))
    return pl.pallas_call(
        functools.partial(_dsa_kernel, n_keep=n_keep),
        grid=(b // nbat, nb),
        in_specs=[
            colblk(nq), colblk(N_HEADS_IDX), colblk(N_HEADS * KV_LATENT),
            pl.BlockSpec((nbat, s, IDX_DIM), lambda bi, i: (bi, 0, 0)),
            pl.BlockSpec((nbat, s, KV_LATENT), lambda bi, i: (bi, 0, 0)),
            pl.BlockSpec((nbat, nb, KVT_ROWS, tq), lambda bi, i: (bi, 0, 0, 0)),
            _const_spec(bias.shape), _const_spec(wuvt.shape), _const_spec(gout.shape),
        ],
        out_specs=pl.BlockSpec((nbat, tq, ATTN_WIDTH), lambda bi, i: (bi, i, 0)),
        out_shape=jax.ShapeDtypeStruct((b, s, ATTN_WIDTH), BF16),
        scratch_shapes=[
            pltpu.VMEM((nb, tq, nbat * tq), F32),
            pltpu.VMEM((nb, tq, nbat * tq), F32),
            pltpu.VMEM((nbat, KVT_ROWS, N_HEADS * tq), F32),
            pltpu.VMEM((nbat, 1, N_HEADS * tq), F32),
            pltpu.VMEM((2, nbat, 2 * tq, N_HEADS * tq), F32),
            pltpu.VMEM((2, nbat, 1, N_HEADS * tq), F32),
        ],
        compiler_params=pltpu.CompilerParams(dimension_semantics=("arbitrary", "arbitrary"),
                                             vmem_limit_bytes=VMEM_LIMIT),
        name="dsa",
    )(qit, wit, qat, ki, ckv, ckvt, bias, wuvt, gout)


def _cast_kernel(*refs):
    n = len(refs) // 2
    for src, dst in zip(refs[:n], refs[n:]):
        dst[...] = src[...].astype(BF16)


def _to_bf16(*ws):
    r, c = ws[0].shape
    spec = pl.BlockSpec((r // CAST_STEPS, c), lambda i: (i, 0))
    return pl.pallas_call(
        _cast_kernel,
        grid=(CAST_STEPS,),
        in_specs=[spec] * len(ws),
        out_specs=[spec] * len(ws),
        out_shape=[jax.ShapeDtypeStruct(w.shape, BF16) for w in ws],
        compiler_params=pltpu.CompilerParams(dimension_semantics=("arbitrary",),
                                             vmem_limit_bytes=VMEM_LIMIT),
        name="to_bf16",
    )(*ws)


def kernel(x, ffn1_norm, ffn1_w_gate, ffn1_w_up, ffn1_w_down, mix_norm, w_in, kv_norm, w_uk, w_uv, rel_bias, conv_w, attn_out_norm, conv_out_norm, w_out, ffn2_norm, ffn2_w_gate, ffn2_w_up, ffn2_w_down, final_norm):
    b, s, d = x.shape
    depth = ffn1_norm.shape[0]
    n_keep = min(TOPK_MAX, s // 4)
    assert s % Q_BLOCK == 0 and s % PROJ_TOKENS == 0 and (b * s) % FFN_TOKENS == 0
    assert Q_BLOCK >= MAX_DISTANCE and n_keep <= Q_BLOCK and (s // Q_BLOCK) % 2 == 0
    row = lambda v: v.reshape(1, -1).astype(F32)
    offs = [0]
    for w in IN_SIZES:
        offs.append(offs[-1] + w)

    bias = _bias_tiles(rel_bias.astype(F32), Q_BLOCK)
    h = x.astype(F32).reshape(b * s, d)
    for l in range(depth):
        wg1, wu1, wg2, wu2 = _to_bf16(ffn1_w_gate[l], ffn1_w_up[l], ffn2_w_gate[l], ffn2_w_up[l])
        wd1, wd2 = _to_bf16(ffn1_w_down[l], ffn2_w_down[l])
        h = _ffn1(h, row(ffn1_norm[l]), wg1, wu1, wd1)

        wi = w_in[l]
        col = lambda k: wi[:, offs[k]:offs[k + 1]]
        wcat = jnp.concatenate(
            [col(1), col(3), jnp.zeros((d, LANES - IDX_DIM), wi.dtype), col(5), col(6), col(7)],
            axis=1).astype(BF16)
        wt = jnp.concatenate(
            [col(0).T, col(1).T, col(2).T, col(4).T,
             jnp.zeros((_R_END - _R_WI - N_HEADS_IDX, d), wi.dtype)], axis=0).astype(BF16)
        wukt = jnp.transpose(w_uk[l], (1, 0, 2)).astype(BF16)
        wuvt = jnp.transpose(w_uv[l], (1, 2, 0)).astype(BF16)

        qat, ckv, ckvt, ki, qit, wit, convn = _proj(
            h.reshape(b, s, d), row(mix_norm[l]), wcat, wt, wukt, row(kv_norm[l]),
            conv_w[l].astype(F32), row(conv_out_norm[l]))
        attn = _dsa(qit, wit, qat, ki, ckv, ckvt, bias, wuvt, row(attn_out_norm[l]), n_keep)

        last = l == depth - 1
        h = _out_ffn2(h, attn.reshape(b * s, ATTN_WIDTH), convn.reshape(b * s, CONV_WIDTH),
                      w_out[l][:ATTN_WIDTH].astype(BF16), w_out[l][ATTN_WIDTH:].astype(BF16),
                      row(ffn2_norm[l]), wg2, wu2, wd2, row(final_norm), final_norm=last)
    return h.reshape(b, s, d).astype(x.dtype)
```

```python
import functools
import math

import jax
import jax.numpy as jnp
from jax import lax
from jax.experimental import pallas as pl
from jax.experimental.pallas import tpu as pltpu

F32 = jnp.float32
BF16 = jnp.bfloat16

EPS = 1e-6
N_HEADS = 8
HEAD_DIM = 64
ATTN_WIDTH = N_HEADS * HEAD_DIM
KV_LATENT = 128
N_HEADS_IDX = 8
IDX_DIM = 64
TOPK_MAX = 256
CONV_WIDTH = 512
CONV_K = 3
N_BUCKETS = 32
MAX_DISTANCE = 128
IN_SIZES = (ATTN_WIDTH, KV_LATENT, N_HEADS_IDX * IDX_DIM, IDX_DIM, N_HEADS_IDX,
            CONV_WIDTH, CONV_WIDTH, CONV_WIDTH)

SUBLANES = 8
LANES = 128

FFN_TOKENS = 512
FFN_CHUNK = 256
PROJ_TOKENS = 512
Q_BLOCK = 256
BISECT_CHUNK = 4
BISECT_FIRST = 4
MAX_BISECT = 64
CAST_STEPS = 8
DSA_BATCH = 2
SWEEP_ACCS = 2
MAX_PEEL = 4096
LOG2E = math.log2(math.e)
VMEM_LIMIT = 56 * 1024 * 1024


def _rms(x, g):
    return x * lax.rsqrt(jnp.mean(x * x, axis=-1, keepdims=True) + EPS) * g


def _const_spec(shape):
    nd = len(shape)
    return pl.BlockSpec(shape, lambda *_: (0,) * nd, pipeline_mode=pl.Buffered(1))


def _swiglu_into(acc_ref, xn, wg_ref, wu_ref, wd_ref):
    d_ff = wg_ref.shape[1]
    for c in range(d_ff // FFN_CHUNK):
        sl = slice(c * FFN_CHUNK, (c + 1) * FFN_CHUNK)
        g = jnp.dot(xn, wg_ref[:, sl], preferred_element_type=F32)
        u = jnp.dot(xn, wu_ref[:, sl], preferred_element_type=F32)
        a = (g * jax.nn.sigmoid(g) * u).astype(BF16)
        d = jnp.dot(a, wd_ref[sl, :], preferred_element_type=F32)
        if c == 0:
            acc_ref[...] = d
        else:
            acc_ref[...] += d


def _ffn1_kernel(x_ref, g_ref, wg_ref, wu_ref, wd_ref, o_ref, acc_ref):
    x = x_ref[...]
    xn = _rms(x, g_ref[...]).astype(BF16)
    _swiglu_into(acc_ref, xn, wg_ref, wu_ref, wd_ref)
    o_ref[...] = x + 0.5 * acc_ref[...]


def _out_ffn2_kernel(h_ref, a_ref, c_ref, woa_ref, woc_ref, g_ref, wg_ref, wu_ref, wd_ref,
                     gf_ref, o_ref, acc_ref, *, final_norm):
    h = (h_ref[...]
         + jnp.dot(a_ref[...], woa_ref[...], preferred_element_type=F32)
         + jnp.dot(c_ref[...], woc_ref[...], preferred_element_type=F32))
    xn = _rms(h, g_ref[...]).astype(BF16)
    _swiglu_into(acc_ref, xn, wg_ref, wu_ref, wd_ref)
    h = h + 0.5 * acc_ref[...]
    o_ref[...] = _rms(h, gf_ref[...]) if final_norm else h


def _ffn1(x2, g, wg, wu, wd):
    n, d = x2.shape
    tile = pl.BlockSpec((FFN_TOKENS, d), lambda i: (i, 0))
    return pl.pallas_call(
        _ffn1_kernel,
        grid=(n // FFN_TOKENS,),
        in_specs=[tile, _const_spec(g.shape), _const_spec(wg.shape), _const_spec(wu.shape),
                  _const_spec(wd.shape)],
        out_specs=tile,
        out_shape=jax.ShapeDtypeStruct((n, d), F32),
        scratch_shapes=[pltpu.VMEM((FFN_TOKENS, d), F32)],
        compiler_params=pltpu.CompilerParams(dimension_semantics=("arbitrary",),
                                             vmem_limit_bytes=VMEM_LIMIT),
        name="ffn1",
    )(x2, g, wg, wu, wd)


def _out_ffn2(h2, a2, c2, woa, woc, g, wg, wu, wd, gf, final_norm):
    n, d = h2.shape
    tile = pl.BlockSpec((FFN_TOKENS, d), lambda i: (i, 0))
    mix = pl.BlockSpec((FFN_TOKENS, a2.shape[1]), lambda i: (i, 0))
    return pl.pallas_call(
        functools.partial(_out_ffn2_kernel, final_norm=final_norm),
        grid=(n // FFN_TOKENS,),
        in_specs=[tile, mix, mix, _const_spec(woa.shape), _const_spec(woc.shape),
                  _const_spec(g.shape), _const_spec(wg.shape), _const_spec(wu.shape),
                  _const_spec(wd.shape), _const_spec(gf.shape)],
        out_specs=tile,
        out_shape=jax.ShapeDtypeStruct((n, d), F32),
        scratch_shapes=[pltpu.VMEM((FFN_TOKENS, d), F32)],
        compiler_params=pltpu.CompilerParams(dimension_semantics=("arbitrary",),
                                             vmem_limit_bytes=VMEM_LIMIT),
        name="out_ffn2",
    )(h2, a2, c2, woa, woc, g, wg, wu, wd, gf)


_C_KV = 0
_C_KI = _C_KV + KV_LATENT
_C_GB = _C_KI + LANES
_C_GC = _C_GB + CONV_WIDTH
_C_X = _C_GC + CONV_WIDTH
_C_END = _C_X + CONV_WIDTH
_R_Q = 0
_R_KV = _R_Q + ATTN_WIDTH
_R_QI = _R_KV + KV_LATENT
_R_WI = _R_QI + N_HEADS_IDX * IDX_DIM
_R_END = _R_WI + 2 * SUBLANES
KVT_ROWS = KV_LATENT + 2 * SUBLANES


def _proj_kernel(h_ref, gm_ref, wcat_ref, wt_ref, wukt_ref, gkv_ref, gkvc_ref, cw_ref, gco_ref,
                 qat_ref, ckv_ref, ckvt_ref, ki_ref, qit_ref, wit_ref, convn_ref, carry_ref):
    tp = h_ref.shape[0]

    @pl.when(pl.program_id(1) == 0)
    def _():
        carry_ref[...] = jnp.zeros_like(carry_ref)

    un = _rms(h_ref[...], gm_ref[...]).astype(BF16)
    p = jnp.dot(un, wcat_ref[...], preferred_element_type=F32)
    ckv_ref[...] = _rms(p[:, _C_KV:_C_KI], gkv_ref[...]).astype(BF16)
    ki_ref[...] = p[:, _C_KI:_C_KI + IDX_DIM].astype(BF16)

    v = p[:, _C_GC:_C_X] * p[:, _C_X:_C_END]
    row = lax.broadcasted_iota(jnp.int32, v.shape, 0)
    prev1 = jnp.broadcast_to(carry_ref[SUBLANES - 1:SUBLANES, :], v.shape)
    prev2 = jnp.broadcast_to(carry_ref[SUBLANES - 2:SUBLANES - 1, :], v.shape)
    v1 = jnp.where(row == 0, prev1, pltpu.roll(v, 1, 0))
    v2 = jnp.where(row == 0, prev2, jnp.where(row == 1, prev1, pltpu.roll(v, 2, 0)))
    carry_ref[...] = v[tp - SUBLANES:, :]
    y = cw_ref[0:1, :] * v2 + cw_ref[1:2, :] * v1 + cw_ref[2:3, :] * v
    conv = p[:, _C_GB:_C_GC] * y
    convn_ref[...] = _rms(conv, gco_ref[...]).astype(BF16)

    pt = lax.dot_general(wt_ref[...], un, (((1,), (1,)), ((), ())),
                         preferred_element_type=F32)
    qt = pt[_R_Q:_R_KV, :].astype(BF16)
    for h in range(N_HEADS):
        qah = jnp.dot(wukt_ref[h], qt[h * HEAD_DIM:(h + 1) * HEAD_DIM, :],
                      preferred_element_type=F32)
        qat_ref[h * KV_LATENT:(h + 1) * KV_LATENT, :] = (qah * (HEAD_DIM ** -0.5 * LOG2E)).astype(BF16)
    kvt = pt[_R_KV:_R_QI, :]
    kvt = kvt * lax.rsqrt(jnp.mean(kvt * kvt, axis=0, keepdims=True) + EPS) * gkvc_ref[...]
    tk = ckvt_ref.shape[2]
    tail = jnp.concatenate([jnp.ones((SUBLANES, tk), F32), jnp.zeros((SUBLANES, tk), F32)], axis=0)
    for kb in range(tp // tk):
        blk = jnp.concatenate([kvt[:, kb * tk:(kb + 1) * tk], tail], axis=0)
        ckvt_ref[kb] = blk.astype(BF16)
    qit_ref[...] = pt[_R_QI:_R_WI, :].astype(BF16)
    wit_ref[...] = pt[_R_WI:_R_WI + N_HEADS_IDX, :] * ((N_HEADS_IDX * IDX_DIM) ** -0.5)


def _proj(h1, gm, wcat, wt, wukt, gkv, cw, gco):
    b, s, d = h1.shape
    tp = PROJ_TOKENS
    nq = N_HEADS_IDX * IDX_DIM
    nqa = N_HEADS * KV_LATENT
    gkvc = gkv.reshape(KV_LATENT, 1)
    tok = lambda w: pl.BlockSpec((None, tp, w), lambda bi, ti: (bi, ti, 0))
    tokt = lambda r: pl.BlockSpec((None, r, tp), lambda bi, ti: (bi, 0, ti))
    out_shape = (
        jax.ShapeDtypeStruct((b, nqa, s), BF16),
        jax.ShapeDtypeStruct((b, s, KV_LATENT), BF16),
        jax.ShapeDtypeStruct((b, s // Q_BLOCK, KVT_ROWS, Q_BLOCK), BF16),
        jax.ShapeDtypeStruct((b, s, IDX_DIM), BF16),
        jax.ShapeDtypeStruct((b, nq, s), BF16),
        jax.ShapeDtypeStruct((b, N_HEADS_IDX, s), F32),
        jax.ShapeDtypeStruct((b, s, CONV_WIDTH), BF16),
    )
    out_specs = (
        tokt(nqa), tok(KV_LATENT),
        pl.BlockSpec((None, tp // Q_BLOCK, KVT_ROWS, Q_BLOCK), lambda bi, ti: (bi, ti, 0, 0)),
        tok(IDX_DIM), tokt(nq), tokt(N_HEADS_IDX), tok(CONV_WIDTH),
    )
    return pl.pallas_call(
        _proj_kernel,
        grid=(b, s // tp),
        in_specs=[tok(d), _const_spec(gm.shape), _const_spec(wcat.shape), _const_spec(wt.shape),
                  _const_spec(wukt.shape), _const_spec(gkv.shape), _const_spec(gkvc.shape),
                  _const_spec(cw.shape), _const_spec(gco.shape)],
        out_specs=out_specs,
        out_shape=out_shape,
        scratch_shapes=[pltpu.VMEM((SUBLANES, CONV_WIDTH), F32)],
        compiler_params=pltpu.CompilerParams(dimension_semantics=("arbitrary", "arbitrary"),
                                             vmem_limit_bytes=VMEM_LIMIT),
        name="proj",
    )(h1, gm, wcat, wt, wukt, gkv, gkvc, cw, gco)


def _bucket_starts():
    e = N_BUCKETS // 2
    n = N_BUCKETS - e
    assert MAX_DISTANCE % e == 0
    r = MAX_DISTANCE // e
    starts, d = [], e
    for k in range(1, n):
        while d ** n < e ** n * r ** k:
            d += 1
        starts.append(d)
    return starts


def _rel_bucket(dist):
    max_exact = N_BUCKETS // 2
    large = max_exact
    for start in _bucket_starts():
        large = large + (dist >= start).astype(jnp.int32)
    return jnp.where(dist < max_exact, dist, large)


def _bias_kernel(rb_ref, o_ref):
    h = pl.program_id(0)
    tk, tq = o_ref.shape
    s = lax.broadcasted_iota(jnp.int32, (tk, tq), 0)
    t = lax.broadcasted_iota(jnp.int32, (tk, tq), 1)
    bucket = _rel_bucket(jnp.maximum(t - s + tq, 0))
    val = jnp.zeros((tk, tq), F32)
    for b in range(N_BUCKETS):
        val = jnp.where(bucket == b, rb_ref[b, h], val)
    o_ref[...] = (val - rb_ref[N_BUCKETS - 1, h]) * LOG2E


def _bias_tiles(rel_bias, tq):
    return pl.pallas_call(
        _bias_kernel,
        grid=(N_HEADS,),
        in_specs=[pl.BlockSpec(memory_space=pltpu.SMEM)],
        out_specs=pl.BlockSpec((None, 2 * tq, tq), lambda h: (h, 0, 0)),
        out_shape=jax.ShapeDtypeStruct((N_HEADS, 2 * tq, tq), F32),
        compiler_params=pltpu.CompilerParams(dimension_semantics=("arbitrary",)),
        name="bias_tiles",
    )(rel_bias)


def _dsa_kernel(qit_ref, wit_ref, qat_ref, ki_ref, ckv_ref, ckvt_ref, bias_ref, wuvt_ref, gout_ref,
                o_ref, sc_ref, mask_ref, acc_ref, m_ref, *, n_keep):
    nbat, tq = o_ref.shape[0], o_ref.shape[1]
    tw = nbat * tq
    i = pl.program_id(1)
    nblk = i + 1
    q0 = i * tq
    kf = float(n_keep)
    key_in_blk = lax.broadcasted_iota(jnp.int32, (tq, tq), 0)
    qry_in_blk = lax.broadcasted_iota(jnp.int32, (tq, tq), 1)
    ind = lambda mask: jnp.where(mask, 1.0, 0.0)
    batches = lambda f: jnp.concatenate([f(g) for g in range(nbat)], axis=1)
    part = lambda a: a.reshape(tq // (SWEEP_ACCS * SUBLANES), SWEEP_ACCS, SUBLANES, tw)
    fold = lambda a: part(a).sum(axis=0)
    fold_max = lambda a: part(a).max(axis=0)
    total = lambda a: a.sum(axis=0).sum(axis=0, keepdims=True)
    total_max = lambda a: a.max(axis=0).max(axis=0, keepdims=True)
    zeros8 = jnp.zeros((SWEEP_ACCS, SUBLANES, tw), F32)
    ninf8 = jnp.full((SWEEP_ACCS, SUBLANES, tw), -jnp.inf, F32)

    def score_body(j, carry):
        mn, mx = carry
        k0 = pl.multiple_of(j * tq, tq)
        causal1 = (k0 + key_in_blk) <= (q0 + qry_in_blk)

        def one(g):
            kib = ki_ref[g, pl.ds(k0, tq), :]
            acc = jnp.zeros((tq, tq), F32)
            for h in range(N_HEADS_IDX):
                d = jnp.dot(kib, qit_ref[g, h * IDX_DIM:(h + 1) * IDX_DIM, :],
                            preferred_element_type=F32)
                acc = acc + jnp.maximum(d, 0.0) * wit_ref[g, h:h + 1, :]
            return acc
        acc = batches(one)
        causal = batches(lambda g: causal1)
        masked = jnp.where(causal, acc, -jnp.inf)
        sc_ref[j] = masked
        mx = jnp.maximum(mx, jnp.max(masked, axis=0, keepdims=True))
        mn = jnp.minimum(mn, jnp.min(jnp.where(causal, acc, jnp.inf), axis=0, keepdims=True))
        return mn, mx

    def over_blocks(body, init):
        def pair(pi, c):
            return body(2 * pi + 1, body(2 * pi, c))
        c = lax.fori_loop(0, lax.shift_right_logical(nblk, 1), pair, init)
        return lax.cond((nblk & 1) == 1, lambda c: body(nblk - 1, c), lambda c: c, c)

    mn, mx = over_blocks(score_body,
                         (jnp.full((1, tw), jnp.inf, F32), jnp.full((1, tw), -jnp.inf, F32)))

    def count(bound, strict=False):
        def one(j, g):
            cols = slice(g * tq, (g + 1) * tq)
            blk = sc_ref[j, :, cols]
            hit = blk > bound[:, cols] if strict else blk >= bound[:, cols]
            return ind(hit).reshape(tq // (SWEEP_ACCS * SUBLANES), SWEEP_ACCS, SUBLANES, tq).sum(axis=0)
        a8 = lax.fori_loop(
            0, nblk,
            lambda j, a: a + jnp.concatenate([one(j, g) for g in range(nbat)], axis=-1), zeros8)
        return total(a8)

    def any_set(flag):
        return jnp.max(flag) > 0.0

    c_ge0 = count(jnp.zeros((1, tw), F32))
    c_gt0 = count(jnp.zeros((1, tw), F32), strict=True)
    ncausal = batches(lambda g: (q0 + lax.broadcasted_iota(jnp.int32, (1, tq), 1) + 1).astype(F32))
    small = ncausal <= kf
    neg = c_ge0 < kf
    zero_tie = (c_gt0 < kf) & ~neg
    lo = jnp.where(small | neg, mn, 0.0)
    clo = jnp.where(small | neg, ncausal, c_ge0)
    above_mx = mx + jnp.abs(mx) * 2.0 ** -22 + 1e-37
    hi = jnp.where(neg, 0.0, above_mx)
    searching = ~small & ~zero_tie

    def bis_cond(st):
        return (st[0] < MAX_BISECT) & st[1]

    def bis_passes(_, st):
        lo, hi, clo, active = st
        for _ in range(BISECT_CHUNK):
            mid = lo + (hi - lo) * 0.5
            live = (active > 0.0) & (mid > lo) & (mid < hi)
            c = count(mid)
            up = live & (c >= kf)
            lo = jnp.where(up, mid, lo)
            clo = jnp.where(up, c, clo)
            hi = jnp.where(live & (c < kf), mid, hi)
            active = ind(live & (clo > kf))
        return lo, hi, clo, active

    def bis_body(st):
        lo, hi, clo, active = bis_passes(0, st[2:])
        return st[0] + 1, any_set(active), lo, hi, clo, active

    active = ind(searching & (clo > kf))
    st = lax.cond(any_set(active),
                  lambda st: lax.fori_loop(0, BISECT_FIRST, bis_passes, st), lambda st: st,
                  (lo, hi, clo, active))
    st = lax.while_loop(bis_cond, bis_body, (jnp.int32(0), any_set(st[3])) + st)
    _, _, lo, hi, clo, _ = st

    unres = ind(searching & (clo > kf))

    def peel_cond(st):
        return (st[0] < MAX_PEEL) & st[1]

    def peel_body(st):
        it, _, lo, clo, cand, unres = st

        def body(j, s2):
            a8, m8 = s2
            blk = sc_ref[j]
            ge = blk >= cand
            below = jnp.where((blk >= lo) & ~ge, blk, -jnp.inf)
            return a8 + fold(ind(ge)), jnp.maximum(m8, fold_max(below))
        a8, m8 = lax.fori_loop(0, nblk, body, (zeros8, ninf8))
        c = total(a8)
        on = unres > 0.0
        found = on & (c >= kf)
        down = on & (c < kf)
        lo = jnp.where(found, cand, lo)
        clo = jnp.where(found, c, clo)
        cand = jnp.where(down, total_max(m8), cand)
        return it + 1, any_set(ind(down)), lo, clo, cand, ind(down)

    st = lax.while_loop(peel_cond, peel_body, (jnp.int32(0), any_set(unres), lo, clo, hi, unres))
    _, _, lo, clo, _, _ = st

    tied = any_set(ind(~small & (clo > kf)))

    @pl.when(jnp.logical_not(tied))
    def _():
        def body(j, c):
            mask_ref[j] = jnp.where(sc_ref[j] >= lo, 0.0, -jnp.inf)
            return c
        lax.fori_loop(0, nblk, body, 0)

    @pl.when(tied)
    def _():
        need = kf - count(lo, strict=True)
        earlier = ind(qry_in_blk < key_in_blk).astype(BF16)

        def body(j, seen):
            blk = sc_ref[j]
            eq = ind(blk == lo)
            rank = seen + jnp.dot(earlier, eq.astype(BF16), preferred_element_type=F32)
            keep = (blk > lo) | ((blk == lo) & (rank < need))
            mask_ref[j] = jnp.where(keep, 0.0, -jnp.inf)
            return seen + total(fold(eq))
        over_blocks(body, jnp.zeros((1, tw), F32))

    m_ref[...] = jnp.full(m_ref.shape, -jnp.inf, F32)
    acc_ref[...] = jnp.zeros(acc_ref.shape, F32)

    def attend(j, nkb, bias):
        tk = nkb * tq
        k0 = pl.multiple_of(j * tq, tq)
        heads = lambda f: jnp.concatenate([f(h) for h in range(N_HEADS)], axis=1)
        for g in range(nbat):
            cols = slice(g * tq, (g + 1) * tq)
            kv = ckv_ref[g, pl.ds(k0, tk), :]
            kvt1 = jnp.concatenate([ckvt_ref[g, j + b] for b in range(nkb)], axis=1)
            mask = jnp.concatenate([mask_ref[j + b, :, cols] for b in range(nkb)], axis=0)
            qa = heads(lambda h: qat_ref[g, h * KV_LATENT:(h + 1) * KV_LATENT, :])
            s = jnp.dot(kv, qa, preferred_element_type=F32)
            if bias is not None:
                s = s + heads(bias)
            s = s + heads(lambda h: mask)
            m_prev = m_ref[g]
            m_new = jnp.maximum(m_prev, jnp.max(s, axis=0, keepdims=True))
            m_safe = jnp.where(m_new == -jnp.inf, 0.0, m_new)
            alpha = jnp.exp2(m_prev - m_safe)
            p = jnp.exp2(s - m_safe).astype(BF16)
            m_ref[g] = m_new
            pv = jnp.dot(kvt1, p, preferred_element_type=F32)
            acc_ref[g] = alpha * acc_ref[g] + pv

    nfar = jnp.maximum(i - 1, 0)

    def far_pair(pi, c):
        attend(2 * pi, 2, None)
        return c
    lax.fori_loop(0, lax.shift_right_logical(nfar, 1), far_pair, 0)

    @pl.when((nfar & 1) == 1)
    def _():
        attend(nfar - 1, 1, None)

    @pl.when(i >= 1)
    def _():
        attend(i - 1, 2, lambda h: bias_ref[h])

    @pl.when(i == 0)
    def _():
        attend(0, 1, lambda h: bias_ref[h, tq:, :])

    for g in range(nbat):
        outs = []
        for h in range(N_HEADS):
            a = acc_ref[g, :, h * tq:(h + 1) * tq]
            ctx = (a[:KV_LATENT, :] / a[KV_LATENT:KV_LATENT + 1, :]).astype(BF16)
            outs.append(jnp.dot(wuvt_ref[h], ctx, preferred_element_type=F32))
        out = jnp.concatenate(outs, axis=0).T
        o_ref[g] = _rms(out, gout_ref[...]).astype(BF16)


def _dsa(qit, wit, qat, ki, ckv, ckvt, bias, wuvt, gout, n_keep):
    b, s, _ = ckv.shape
    tq = Q_BLOCK
    nb = s // tq
    nq = N_HEADS_IDX * IDX_DIM
    nbat = DSA_BATCH if b % DSA_BATCH == 0 else 1
    colblk = lambda r: pl.BlockSpec((nbat, r, tq), lambda bi, i: (bi, 0, i))
    return pl.pallas_call(
        functools.partial(_dsa_kernel, n_keep=n_keep),
        grid=(b // nbat, nb),
        in_specs=[
            colblk(nq), colblk(N_HEADS_IDX), colblk(N_HEADS * KV_LATENT),
            pl.BlockSpec((nbat, s, IDX_DIM), lambda bi, i: (bi, 0, 0)),
            pl.BlockSpec((nbat, s, KV_LATENT), lambda bi, i: (bi, 0, 0)),
            pl.BlockSpec((nbat, nb, KVT_ROWS, tq), lambda bi, i: (bi, 0, 0, 0)),
            _const_spec(bias.shape), _const_spec(wuvt.shape), _const_spec(gout.shape),
        ],
        out_specs=pl.BlockSpec((nbat, tq, ATTN_WIDTH), lambda bi, i: (bi, i, 0)),
        out_shape=jax.ShapeDtypeStruct((b, s, ATTN_WIDTH), BF16),
        scratch_shapes=[
            pltpu.VMEM((nb, tq, nbat * tq), F32),
            pltpu.VMEM((nb, tq, nbat * tq), F32),
            pltpu.VMEM((nbat, KVT_ROWS, N_HEADS * tq), F32),
            pltpu.VMEM((nbat, 1, N_HEADS * tq), F32),
        ],
        compiler_params=pltpu.CompilerParams(dimension_semantics=("arbitrary", "arbitrary"),
                                             vmem_limit_bytes=VMEM_LIMIT),
        name="dsa",
    )(qit, wit, qat, ki, ckv, ckvt, bias, wuvt, gout)


def _cast_kernel(*refs):
    n = len(refs) // 2
    for src, dst in zip(refs[:n], refs[n:]):
        dst[...] = src[...].astype(BF16)


def _to_bf16(*ws):
    r, c = ws[0].shape
    spec = pl.BlockSpec((r // CAST_STEPS, c), lambda i: (i, 0))
    return pl.pallas_call(
        _cast_kernel,
        grid=(CAST_STEPS,),
        in_specs=[spec] * len(ws),
        out_specs=[spec] * len(ws),
        out_shape=[jax.ShapeDtypeStruct(w.shape, BF16) for w in ws],
        compiler_params=pltpu.CompilerParams(dimension_semantics=("arbitrary",),
                                             vmem_limit_bytes=VMEM_LIMIT),
        name="to_bf16",
    )(*ws)


def kernel(x, ffn1_norm, ffn1_w_gate, ffn1_w_up, ffn1_w_down, mix_norm, w_in, kv_norm, w_uk, w_uv, rel_bias, conv_w, attn_out_norm, conv_out_norm, w_out, ffn2_norm, ffn2_w_gate, ffn2_w_up, ffn2_w_down, final_norm):
    b, s, d = x.shape
    depth = ffn1_norm.shape[0]
    n_keep = min(TOPK_MAX, s // 4)
    assert s % Q_BLOCK == 0 and s % PROJ_TOKENS == 0 and (b * s) % FFN_TOKENS == 0
    assert Q_BLOCK >= MAX_DISTANCE and n_keep <= Q_BLOCK
    row = lambda v: v.reshape(1, -1).astype(F32)
    offs = [0]
    for w in IN_SIZES:
        offs.append(offs[-1] + w)

    bias = _bias_tiles(rel_bias.astype(F32), Q_BLOCK)
    h = x.astype(F32).reshape(b * s, d)
    for l in range(depth):
        wg1, wu1, wg2, wu2 = _to_bf16(ffn1_w_gate[l], ffn1_w_up[l], ffn2_w_gate[l], ffn2_w_up[l])
        wd1, wd2 = _to_bf16(ffn1_w_down[l], ffn2_w_down[l])
        h = _ffn1(h, row(ffn1_norm[l]), wg1, wu1, wd1)

        wi = w_in[l]
        col = lambda k: wi[:, offs[k]:offs[k + 1]]
        wcat = jnp.concatenate(
            [col(1), col(3), jnp.zeros((d, LANES - IDX_DIM), wi.dtype), col(5), col(6), col(7)],
            axis=1).astype(BF16)
        wt = jnp.concatenate(
            [col(0).T, col(1).T, col(2).T, col(4).T,
             jnp.zeros((_R_END - _R_WI - N_HEADS_IDX, d), wi.dtype)], axis=0).astype(BF16)
        wukt = jnp.transpose(w_uk[l], (1, 0, 2)).astype(BF16)
        wuvt = jnp.transpose(w_uv[l], (1, 2, 0)).astype(BF16)

        qat, ckv, ckvt, ki, qit, wit, convn = _proj(
            h.reshape(b, s, d), row(mix_norm[l]), wcat, wt, wukt, row(kv_norm[l]),
            conv_w[l].astype(F32), row(conv_out_norm[l]))
        attn = _dsa(qit, wit, qat, ki, ckv, ckvt, bias, wuvt, row(attn_out_norm[l]), n_keep)

        last = l == depth - 1
        h = _out_ffn2(h, attn.reshape(b * s, ATTN_WIDTH), convn.reshape(b * s, CONV_WIDTH),
                      w_out[l][:ATTN_WIDTH].astype(BF16), w_out[l][ATTN_WIDTH:].astype(BF16),
                      row(ffn2_norm[l]), wg2, wu2, wd2, row(final_norm), final_norm=last)
    return h.reshape(b, s, d).astype(x.dtype)
```

```python
import functools
import math

import jax
import jax.numpy as jnp
from jax import lax
from jax.experimental import pallas as pl
from jax.experimental.pallas import tpu as pltpu

F32 = jnp.float32
BF16 = jnp.bfloat16

EPS = 1e-6
N_HEADS = 8
HEAD_DIM = 64
ATTN_WIDTH = N_HEADS * HEAD_DIM
KV_LATENT = 128
N_HEADS_IDX = 8
IDX_DIM = 64
TOPK_MAX = 256
CONV_WIDTH = 512
CONV_K = 3
N_BUCKETS = 32
MAX_DISTANCE = 128
IN_SIZES = (ATTN_WIDTH, KV_LATENT, N_HEADS_IDX * IDX_DIM, IDX_DIM, N_HEADS_IDX,
            CONV_WIDTH, CONV_WIDTH, CONV_WIDTH)

SUBLANES = 8
LANES = 128

FFN_TOKENS = 1024
FFN_CHUNK = 256
PROJ_TOKENS = 1024
Q_BLOCK = 256
BISECT_CHUNK = 2
BISECT_FIRST = 8
MAX_BISECT = 128
CAST_STEPS = 8
DSA_BATCH = 2
SWEEP_ACCS = 2
MAX_PEEL = 4096
LOG2E = math.log2(math.e)
VMEM_LIMIT = 56 * 1024 * 1024


def _rms(x, g):
    return x * lax.rsqrt(jnp.mean(x * x, axis=-1, keepdims=True) + EPS) * g


def _const_spec(shape):
    nd = len(shape)
    return pl.BlockSpec(shape, lambda *_: (0,) * nd, pipeline_mode=pl.Buffered(1))


def _swiglu_into(acc_ref, xn, wg_ref, wu_ref, wd_ref):
    d_ff = wg_ref.shape[1]
    for c in range(d_ff // FFN_CHUNK):
        sl = slice(c * FFN_CHUNK, (c + 1) * FFN_CHUNK)
        g = jnp.dot(xn, wg_ref[:, sl], preferred_element_type=F32)
        u = jnp.dot(xn, wu_ref[:, sl], preferred_element_type=F32)
        a = (g * jax.nn.sigmoid(g) * u).astype(BF16)
        d = jnp.dot(a, wd_ref[sl, :], preferred_element_type=F32)
        if c == 0:
            acc_ref[...] = d
        else:
            acc_ref[...] += d


def _ffn1_kernel(x_ref, g_ref, wg_ref, wu_ref, wd_ref, o_ref, acc_ref):
    x = x_ref[...]
    xn = _rms(x, g_ref[...]).astype(BF16)
    _swiglu_into(acc_ref, xn, wg_ref, wu_ref, wd_ref)
    o_ref[...] = x + 0.5 * acc_ref[...]


def _out_ffn2_kernel(h_ref, a_ref, c_ref, woa_ref, woc_ref, g_ref, wg_ref, wu_ref, wd_ref,
                     gf_ref, o_ref, acc_ref, *, final_norm):
    h = (h_ref[...]
         + jnp.dot(a_ref[...], woa_ref[...], preferred_element_type=F32)
         + jnp.dot(c_ref[...], woc_ref[...], preferred_element_type=F32))
    xn = _rms(h, g_ref[...]).astype(BF16)
    _swiglu_into(acc_ref, xn, wg_ref, wu_ref, wd_ref)
    h = h + 0.5 * acc_ref[...]
    o_ref[...] = _rms(h, gf_ref[...]) if final_norm else h


def _ffn1(x2, g, wg, wu, wd):
    n, d = x2.shape
    tile = pl.BlockSpec((FFN_TOKENS, d), lambda i: (i, 0))
    return pl.pallas_call(
        _ffn1_kernel,
        grid=(n // FFN_TOKENS,),
        in_specs=[tile, _const_spec(g.shape), _const_spec(wg.shape), _const_spec(wu.shape),
                  _const_spec(wd.shape)],
        out_specs=tile,
        out_shape=jax.ShapeDtypeStruct((n, d), F32),
        scratch_shapes=[pltpu.VMEM((FFN_TOKENS, d), F32)],
        compiler_params=pltpu.CompilerParams(dimension_semantics=("arbitrary",),
                                             vmem_limit_bytes=VMEM_LIMIT),
        name="ffn1",
    )(x2, g, wg, wu, wd)


def _out_ffn2(h2, a2, c2, woa, woc, g, wg, wu, wd, gf, final_norm):
    n, d = h2.shape
    tile = pl.BlockSpec((FFN_TOKENS, d), lambda i: (i, 0))
    mix = pl.BlockSpec((FFN_TOKENS, a2.shape[1]), lambda i: (i, 0))
    return pl.pallas_call(
        functools.partial(_out_ffn2_kernel, final_norm=final_norm),
        grid=(n // FFN_TOKENS,),
        in_specs=[tile, mix, mix, _const_spec(woa.shape), _const_spec(woc.shape),
                  _const_spec(g.shape), _const_spec(wg.shape), _const_spec(wu.shape),
                  _const_spec(wd.shape), _const_spec(gf.shape)],
        out_specs=tile,
        out_shape=jax.ShapeDtypeStruct((n, d), F32),
        scratch_shapes=[pltpu.VMEM((FFN_TOKENS, d), F32)],
        compiler_params=pltpu.CompilerParams(dimension_semantics=("arbitrary",),
                                             vmem_limit_bytes=VMEM_LIMIT),
        name="out_ffn2",
    )(h2, a2, c2, woa, woc, g, wg, wu, wd, gf)


_C_KV = 0
_C_KI = _C_KV + KV_LATENT
_C_GB = _C_KI + LANES
_C_GC = _C_GB + CONV_WIDTH
_C_X = _C_GC + CONV_WIDTH
_C_END = _C_X + CONV_WIDTH
_R_Q = 0
_R_KV = _R_Q + ATTN_WIDTH
_R_QI = _R_KV + KV_LATENT
_R_WI = _R_QI + N_HEADS_IDX * IDX_DIM
_R_END = _R_WI + 2 * SUBLANES
KVT_ROWS = KV_LATENT + 2 * SUBLANES


def _proj_kernel(h_ref, gm_ref, wcat_ref, wt_ref, wukt_ref, gkv_ref, gkvc_ref, cw_ref, gco_ref,
                 qat_ref, ckv_ref, ckvt_ref, ki_ref, qit_ref, wit_ref, convn_ref, carry_ref):
    tp = h_ref.shape[0]

    @pl.when(pl.program_id(1) == 0)
    def _():
        carry_ref[...] = jnp.zeros_like(carry_ref)

    un = _rms(h_ref[...], gm_ref[...]).astype(BF16)
    p = jnp.dot(un, wcat_ref[...], preferred_element_type=F32)
    ckv_ref[...] = _rms(p[:, _C_KV:_C_KI], gkv_ref[...]).astype(BF16)
    ki_ref[...] = p[:, _C_KI:_C_KI + IDX_DIM].astype(BF16)

    v = p[:, _C_GC:_C_X] * p[:, _C_X:_C_END]
    row = lax.broadcasted_iota(jnp.int32, v.shape, 0)
    prev1 = jnp.broadcast_to(carry_ref[SUBLANES - 1:SUBLANES, :], v.shape)
    prev2 = jnp.broadcast_to(carry_ref[SUBLANES - 2:SUBLANES - 1, :], v.shape)
    v1 = jnp.where(row == 0, prev1, pltpu.roll(v, 1, 0))
    v2 = jnp.where(row == 0, prev2, jnp.where(row == 1, prev1, pltpu.roll(v, 2, 0)))
    carry_ref[...] = v[tp - SUBLANES:, :]
    y = cw_ref[0:1, :] * v2 + cw_ref[1:2, :] * v1 + cw_ref[2:3, :] * v
    conv = p[:, _C_GB:_C_GC] * y
    convn_ref[...] = _rms(conv, gco_ref[...]).astype(BF16)

    pt = lax.dot_general(wt_ref[...], un, (((1,), (1,)), ((), ())),
                         preferred_element_type=F32)
    qt = pt[_R_Q:_R_KV, :].astype(BF16)
    for h in range(N_HEADS):
        qah = jnp.dot(wukt_ref[h], qt[h * HEAD_DIM:(h + 1) * HEAD_DIM, :],
                      preferred_element_type=F32)
        qat_ref[h * KV_LATENT:(h + 1) * KV_LATENT, :] = (qah * (HEAD_DIM ** -0.5 * LOG2E)).astype(BF16)
    kvt = pt[_R_KV:_R_QI, :]
    kvt = kvt * lax.rsqrt(jnp.mean(kvt * kvt, axis=0, keepdims=True) + EPS) * gkvc_ref[...]
    tk = ckvt_ref.shape[2]
    tail = jnp.concatenate([jnp.ones((SUBLANES, tk), F32), jnp.zeros((SUBLANES, tk), F32)], axis=0)
    for kb in range(tp // tk):
        blk = jnp.concatenate([kvt[:, kb * tk:(kb + 1) * tk], tail], axis=0)
        ckvt_ref[kb] = blk.astype(BF16)
    qit_ref[...] = pt[_R_QI:_R_WI, :].astype(BF16)
    wit_ref[...] = pt[_R_WI:_R_WI + N_HEADS_IDX, :] * ((N_HEADS_IDX * IDX_DIM) ** -0.5)


def _proj(h1, gm, wcat, wt, wukt, gkv, cw, gco):
    b, s, d = h1.shape
    tp = PROJ_TOKENS
    nq = N_HEADS_IDX * IDX_DIM
    nqa = N_HEADS * KV_LATENT
    gkvc = gkv.reshape(KV_LATENT, 1)
    tok = lambda w: pl.BlockSpec((None, tp, w), lambda bi, ti: (bi, ti, 0))
    tokt = lambda r: pl.BlockSpec((None, r, tp), lambda bi, ti: (bi, 0, ti))
    out_shape = (
        jax.ShapeDtypeStruct((b, nqa, s), BF16),
        jax.ShapeDtypeStruct((b, s, KV_LATENT), BF16),
        jax.ShapeDtypeStruct((b, s // Q_BLOCK, KVT_ROWS, Q_BLOCK), BF16),
        jax.ShapeDtypeStruct((b, s, IDX_DIM), BF16),
        jax.ShapeDtypeStruct((b, nq, s), BF16),
        jax.ShapeDtypeStruct((b, N_HEADS_IDX, s), F32),
        jax.ShapeDtypeStruct((b, s, CONV_WIDTH), BF16),
    )
    out_specs = (
        tokt(nqa), tok(KV_LATENT),
        pl.BlockSpec((None, tp // Q_BLOCK, KVT_ROWS, Q_BLOCK), lambda bi, ti: (bi, ti, 0, 0)),
        tok(IDX_DIM), tokt(nq), tokt(N_HEADS_IDX), tok(CONV_WIDTH),
    )
    return pl.pallas_call(
        _proj_kernel,
        grid=(b, s // tp),
        in_specs=[tok(d), _const_spec(gm.shape), _const_spec(wcat.shape), _const_spec(wt.shape),
                  _const_spec(wukt.shape), _const_spec(gkv.shape), _const_spec(gkvc.shape),
                  _const_spec(cw.shape), _const_spec(gco.shape)],
        out_specs=out_specs,
        out_shape=out_shape,
        scratch_shapes=[pltpu.VMEM((SUBLANES, CONV_WIDTH), F32)],
        compiler_params=pltpu.CompilerParams(dimension_semantics=("arbitrary", "arbitrary"),
                                             vmem_limit_bytes=VMEM_LIMIT),
        name="proj",
    )(h1, gm, wcat, wt, wukt, gkv, gkvc, cw, gco)


def _bucket_starts():
    e = N_BUCKETS // 2
    n = N_BUCKETS - e
    assert MAX_DISTANCE % e == 0
    r = MAX_DISTANCE // e
    starts, d = [], e
    for k in range(1, n):
        while d ** n < e ** n * r ** k:
            d += 1
        starts.append(d)
    return starts


def _rel_bucket(dist):
    max_exact = N_BUCKETS // 2
    large = max_exact
    for start in _bucket_starts():
        large = large + (dist >= start).astype(jnp.int32)
    return jnp.where(dist < max_exact, dist, large)


def _bias_kernel(rb_ref, o_ref):
    h = pl.program_id(0)
    tk, tq = o_ref.shape
    s = lax.broadcasted_iota(jnp.int32, (tk, tq), 0)
    t = lax.broadcasted_iota(jnp.int32, (tk, tq), 1)
    bucket = _rel_bucket(jnp.maximum(t - s + tq, 0))
    val = jnp.zeros((tk, tq), F32)
    for b in range(N_BUCKETS):
        val = jnp.where(bucket == b, rb_ref[b, h], val)
    o_ref[...] = (val - rb_ref[N_BUCKETS - 1, h]) * LOG2E


def _bias_tiles(rel_bias, tq):
    return pl.pallas_call(
        _bias_kernel,
        grid=(N_HEADS,),
        in_specs=[pl.BlockSpec(memory_space=pltpu.SMEM)],
        out_specs=pl.BlockSpec((None, 2 * tq, tq), lambda h: (h, 0, 0)),
        out_shape=jax.ShapeDtypeStruct((N_HEADS, 2 * tq, tq), F32),
        compiler_params=pltpu.CompilerParams(dimension_semantics=("arbitrary",)),
        name="bias_tiles",
    )(rel_bias)


def _dsa_kernel(qit_ref, wit_ref, qat_ref, ki_ref, ckv_ref, ckvt_ref, bias_ref, wuvt_ref, gout_ref,
                o_ref, sc_ref, mask_ref, acc_ref, m_ref, *, n_keep):
    nbat, tq = o_ref.shape[0], o_ref.shape[1]
    tw = nbat * tq
    i = pl.program_id(1)
    nblk = i + 1
    q0 = i * tq
    kf = float(n_keep)
    key_in_blk = lax.broadcasted_iota(jnp.int32, (tq, tq), 0)
    qry_in_blk = lax.broadcasted_iota(jnp.int32, (tq, tq), 1)
    ind = lambda mask: jnp.where(mask, 1.0, 0.0)
    batches = lambda f: jnp.concatenate([f(g) for g in range(nbat)], axis=1)
    part = lambda a: a.reshape(tq // (SWEEP_ACCS * SUBLANES), SWEEP_ACCS, SUBLANES, tw)
    fold = lambda a: part(a).sum(axis=0)
    fold_max = lambda a: part(a).max(axis=0)
    total = lambda a: a.sum(axis=0).sum(axis=0, keepdims=True)
    total_max = lambda a: a.max(axis=0).max(axis=0, keepdims=True)
    zeros8 = jnp.zeros((SWEEP_ACCS, SUBLANES, tw), F32)
    ninf8 = jnp.full((SWEEP_ACCS, SUBLANES, tw), -jnp.inf, F32)

    def score_body(j, carry):
        mn, mx = carry
        k0 = pl.multiple_of(j * tq, tq)
        causal1 = (k0 + key_in_blk) <= (q0 + qry_in_blk)

        def one(g):
            kib = ki_ref[g, pl.ds(k0, tq), :]
            acc = jnp.zeros((tq, tq), F32)
            for h in range(N_HEADS_IDX):
                d = jnp.dot(kib, qit_ref[g, h * IDX_DIM:(h + 1) * IDX_DIM, :],
                            preferred_element_type=F32)
                acc = acc + jnp.maximum(d, 0.0) * wit_ref[g, h:h + 1, :]
            return acc
        acc = batches(one)
        causal = batches(lambda g: causal1)
        masked = jnp.where(causal, acc, -jnp.inf)
        sc_ref[j] = masked
        mx = jnp.maximum(mx, jnp.max(masked, axis=0, keepdims=True))
        mn = jnp.minimum(mn, jnp.min(jnp.where(causal, acc, jnp.inf), axis=0, keepdims=True))
        return mn, mx

    def over_blocks(body, init):
        def pair(pi, c):
            return body(2 * pi + 1, body(2 * pi, c))
        c = lax.fori_loop(0, lax.shift_right_logical(nblk, 1), pair, init)
        return lax.cond((nblk & 1) == 1, lambda c: body(nblk - 1, c), lambda c: c, c)

    mn, mx = over_blocks(score_body,
                         (jnp.full((1, tw), jnp.inf, F32), jnp.full((1, tw), -jnp.inf, F32)))

    def count(bound, strict=False):
        def one(j, g):
            cols = slice(g * tq, (g + 1) * tq)
            blk = sc_ref[j, :, cols]
            hit = blk > bound[:, cols] if strict else blk >= bound[:, cols]
            return ind(hit).reshape(tq // (SWEEP_ACCS * SUBLANES), SWEEP_ACCS, SUBLANES, tq).sum(axis=0)
        a8 = lax.fori_loop(
            0, nblk,
            lambda j, a: a + jnp.concatenate([one(j, g) for g in range(nbat)], axis=-1), zeros8)
        return total(a8)

    def any_set(flag):
        return jnp.max(flag) > 0.0

    c_ge0 = count(jnp.zeros((1, tw), F32))
    c_gt0 = count(jnp.zeros((1, tw), F32), strict=True)
    ncausal = batches(lambda g: (q0 + lax.broadcasted_iota(jnp.int32, (1, tq), 1) + 1).astype(F32))
    small = ncausal <= kf
    neg = c_ge0 < kf
    zero_tie = (c_gt0 < kf) & ~neg
    lo = jnp.where(small | neg, mn, 0.0)
    clo = jnp.where(small | neg, ncausal, c_ge0)
    above_mx = mx + jnp.abs(mx) * 2.0 ** -22 + 1e-37
    hi = jnp.where(neg, 0.0, above_mx)
    searching = ~small & ~zero_tie

    def bis_cond(st):
        return (st[0] < MAX_BISECT) & st[1]

    def bis_passes(_, st):
        lo, hi, clo, active = st
        for _ in range(BISECT_CHUNK):
            mid = lo + (hi - lo) * 0.5
            live = (active > 0.0) & (mid > lo) & (mid < hi)
            c = count(mid)
            up = live & (c >= kf)
            lo = jnp.where(up, mid, lo)
            clo = jnp.where(up, c, clo)
            hi = jnp.where(live & (c < kf), mid, hi)
            active = ind(live & (clo > kf))
        return lo, hi, clo, active

    def bis_body(st):
        lo, hi, clo, active = bis_passes(0, st[2:])
        return st[0] + 1, any_set(active), lo, hi, clo, active

    active = ind(searching & (clo > kf))
    st = lax.cond(any_set(active),
                  lambda st: lax.fori_loop(0, BISECT_FIRST, bis_passes, st), lambda st: st,
                  (lo, hi, clo, active))
    st = lax.while_loop(bis_cond, bis_body, (jnp.int32(0), any_set(st[3])) + st)
    _, _, lo, hi, clo, _ = st

    unres = ind(searching & (clo > kf))

    def peel_cond(st):
        return (st[0] < MAX_PEEL) & st[1]

    def peel_body(st):
        it, _, lo, clo, cand, unres = st

        def body(j, s2):
            a8, m8 = s2
            blk = sc_ref[j]
            ge = blk >= cand
            below = jnp.where((blk >= lo) & ~ge, blk, -jnp.inf)
            return a8 + fold(ind(ge)), jnp.maximum(m8, fold_max(below))
        a8, m8 = lax.fori_loop(0, nblk, body, (zeros8, ninf8))
        c = total(a8)
        on = unres > 0.0
        found = on & (c >= kf)
        down = on & (c < kf)
        lo = jnp.where(found, cand, lo)
        clo = jnp.where(found, c, clo)
        cand = jnp.where(down, total_max(m8), cand)
        return it + 1, any_set(ind(down)), lo, clo, cand, ind(down)

    st = lax.while_loop(peel_cond, peel_body, (jnp.int32(0), any_set(unres), lo, clo, hi, unres))
    _, _, lo, clo, _, _ = st

    tied = any_set(ind(~small & (clo > kf)))

    @pl.when(jnp.logical_not(tied))
    def _():
        def body(j, c):
            mask_ref[j] = jnp.where(sc_ref[j] >= lo, 0.0, -jnp.inf)
            return c
        lax.fori_loop(0, nblk, body, 0)

    @pl.when(tied)
    def _():
        need = kf - count(lo, strict=True)
        earlier = ind(qry_in_blk < key_in_blk).astype(BF16)

        def body(j, seen):
            blk = sc_ref[j]
            eq = ind(blk == lo)
            rank = seen + jnp.dot(earlier, eq.astype(BF16), preferred_element_type=F32)
            keep = (blk > lo) | ((blk == lo) & (rank < need))
            mask_ref[j] = jnp.where(keep, 0.0, -jnp.inf)
            return seen + total(fold(eq))
        over_blocks(body, jnp.zeros((1, tw), F32))

    m_ref[...] = jnp.full(m_ref.shape, -jnp.inf, F32)
    acc_ref[...] = jnp.zeros(acc_ref.shape, F32)

    def attend(j, nkb, bias):
        tk = nkb * tq
        k0 = pl.multiple_of(j * tq, tq)
        heads = lambda f: jnp.concatenate([f(h) for h in range(N_HEADS)], axis=1)
        for g in range(nbat):
            cols = slice(g * tq, (g + 1) * tq)
            kv = ckv_ref[g, pl.ds(k0, tk), :]
            kvt1 = jnp.concatenate([ckvt_ref[g, j + b] for b in range(nkb)], axis=1)
            mask = jnp.concatenate([mask_ref[j + b, :, cols] for b in range(nkb)], axis=0)
            qa = heads(lambda h: qat_ref[g, h * KV_LATENT:(h + 1) * KV_LATENT, :])
            s = jnp.dot(kv, qa, preferred_element_type=F32)
            if bias is not None:
                s = s + heads(bias)
            s = s + heads(lambda h: mask)
            m_prev = m_ref[g]
            m_new = jnp.maximum(m_prev, jnp.max(s, axis=0, keepdims=True))
            m_safe = jnp.where(m_new == -jnp.inf, 0.0, m_new)
            alpha = jnp.exp2(m_prev - m_safe)
            p = jnp.exp2(s - m_safe).astype(BF16)
            m_ref[g] = m_new
            pv = jnp.dot(kvt1, p, preferred_element_type=F32)
            acc_ref[g] = alpha * acc_ref[g] + pv

    nfar = jnp.maximum(i - 1, 0)

    def far_pair(pi, c):
        attend(2 * pi, 2, None)
        return c
    lax.fori_loop(0, lax.shift_right_logical(nfar, 1), far_pair, 0)

    @pl.when((nfar & 1) == 1)
    def _():
        attend(nfar - 1, 1, None)

    @pl.when(i >= 1)
    def _():
        attend(i - 1, 2, lambda h: bias_ref[h])

    @pl.when(i == 0)
    def _():
        attend(0, 1, lambda h: bias_ref[h, tq:, :])

    for g in range(nbat):
        outs = []
        for h in range(N_HEADS):
            a = acc_ref[g, :, h * tq:(h + 1) * tq]
            ctx = (a[:KV_LATENT, :] / a[KV_LATENT:KV_LATENT + 1, :]).astype(BF16)
            outs.append(jnp.dot(wuvt_ref[h], ctx, preferred_element_type=F32))
        out = jnp.concatenate(outs, axis=0).T
        o_ref[g] = _rms(out, gout_ref[...]).astype(BF16)


def _dsa(qit, wit, qat, ki, ckv, ckvt, bias, wuvt, gout, n_keep):
    b, s, _ = ckv.shape
    tq = Q_BLOCK
    nb = s // tq
    nq = N_HEADS_IDX * IDX_DIM
    nbat = DSA_BATCH if b % DSA_BATCH == 0 else 1
    colblk = lambda r: pl.BlockSpec((nbat, r, tq), lambda bi, i: (bi, 0, i))
    return pl.pallas_call(
        functools.partial(_dsa_kernel, n_keep=n_keep),
        grid=(b // nbat, nb),
        in_specs=[
            colblk(nq), colblk(N_HEADS_IDX), colblk(N_HEADS * KV_LATENT),
            pl.BlockSpec((nbat, s, IDX_DIM), lambda bi, i: (bi, 0, 0)),
            pl.BlockSpec((nbat, s, KV_LATENT), lambda bi, i: (bi, 0, 0)),
            pl.BlockSpec((nbat, nb, KVT_ROWS, tq), lambda bi, i: (bi, 0, 0, 0)),
            _const_spec(bias.shape), _const_spec(wuvt.shape), _const_spec(gout.shape),
        ],
        out_specs=pl.BlockSpec((nbat, tq, ATTN_WIDTH), lambda bi, i: (bi, i, 0)),
        out_shape=jax.ShapeDtypeStruct((b, s, ATTN_WIDTH), BF16),
        scratch_shapes=[
            pltpu.VMEM((nb, tq, nbat * tq), F32),
            pltpu.VMEM((nb, tq, nbat * tq), F32),
            pltpu.VMEM((nbat, KVT_ROWS, N_HEADS * tq), F32),
            pltpu.VMEM((nbat, 1, N_HEADS * tq), F32),
        ],
        compiler_params=pltpu.CompilerParams(dimension_semantics=("arbitrary", "arbitrary"),
                                             vmem_limit_bytes=VMEM_LIMIT),
        name="dsa",
    )(qit, wit, qat, ki, ckv, ckvt, bias, wuvt, gout)


def _cast_kernel(*refs):
    n = len(refs) // 2
    for src, dst in zip(refs[:n], refs[n:]):
        dst[...] = src[...].astype(BF16)


def _to_bf16(*ws):
    r, c = ws[0].shape
    spec = pl.BlockSpec((r // CAST_STEPS, c), lambda i: (i, 0))
    return pl.pallas_call(
        _cast_kernel,
        grid=(CAST_STEPS,),
        in_specs=[spec] * len(ws),
        out_specs=[spec] * len(ws),
        out_shape=[jax.ShapeDtypeStruct(w.shape, BF16) for w in ws],
        compiler_params=pltpu.CompilerParams(dimension_semantics=("arbitrary",),
                                             vmem_limit_bytes=VMEM_LIMIT),
        name="to_bf16",
    )(*ws)


def kernel(x, ffn1_norm, ffn1_w_gate, ffn1_w_up, ffn1_w_down, mix_norm, w_in, kv_norm, w_uk, w_uv, rel_bias, conv_w, attn_out_norm, conv_out_norm, w_out, ffn2_norm, ffn2_w_gate, ffn2_w_up, ffn2_w_down, final_norm):
    b, s, d = x.shape
    depth = ffn1_norm.shape[0]
    n_keep = min(TOPK_MAX, s // 4)
    assert s % Q_BLOCK == 0 and s % PROJ_TOKENS == 0 and (b * s) % FFN_TOKENS == 0
    assert Q_BLOCK >= MAX_DISTANCE and n_keep <= Q_BLOCK
    row = lambda v: v.reshape(1, -1).astype(F32)
    offs = [0]
    for w in IN_SIZES:
        offs.append(offs[-1] + w)

    bias = _bias_tiles(rel_bias.astype(F32), Q_BLOCK)
    h = x.astype(F32).reshape(b * s, d)
    for l in range(depth):
        wg1, wu1, wg2, wu2 = _to_bf16(ffn1_w_gate[l], ffn1_w_up[l], ffn2_w_gate[l], ffn2_w_up[l])
        wd1, wd2 = _to_bf16(ffn1_w_down[l], ffn2_w_down[l])
        h = _ffn1(h, row(ffn1_norm[l]), wg1, wu1, wd1)

        wi = w_in[l]
        col = lambda k: wi[:, offs[k]:offs[k + 1]]
        wcat = jnp.concatenate(
            [col(1), col(3), jnp.zeros((d, LANES - IDX_DIM), wi.dtype), col(5), col(6), col(7)],
            axis=1).astype(BF16)
        wt = jnp.concatenate(
            [col(0).T, col(1).T, col(2).T, col(4).T,
             jnp.zeros((_R_END - _R_WI - N_HEADS_IDX, d), wi.dtype)], axis=0).astype(BF16)
        wukt = jnp.transpose(w_uk[l], (1, 0, 2)).astype(BF16)
        wuvt = jnp.transpose(w_uv[l], (1, 2, 0)).astype(BF16)

        qat, ckv, ckvt, ki, qit, wit, convn = _proj(
            h.reshape(b, s, d), row(mix_norm[l]), wcat, wt, wukt, row(kv_norm[l]),
            conv_w[l].astype(F32), row(conv_out_norm[l]))
        attn = _dsa(qit, wit, qat, ki, ckv, ckvt, bias, wuvt, row(attn_out_norm[l]), n_keep)

        last = l == depth - 1
        h = _out_ffn2(h, attn.reshape(b * s, ATTN_WIDTH), convn.reshape(b * s, CONV_WIDTH),
                      w_out[l][:ATTN_WIDTH].astype(BF16), w_out[l][ATTN_WIDTH:].astype(BF16),
                      row(ffn2_norm[l]), wg2, wu2, wd2, row(final_norm), final_norm=last)
    return h.reshape(b, s, d).astype(x.dtype)
```

```python
import functools
import math

import jax
import jax.numpy as jnp
from jax import lax
from jax.experimental import pallas as pl
from jax.experimental.pallas import tpu as pltpu

F32 = jnp.float32
BF16 = jnp.bfloat16

EPS = 1e-6
N_HEADS = 8
HEAD_DIM = 64
ATTN_WIDTH = N_HEADS * HEAD_DIM
KV_LATENT = 128
N_HEADS_IDX = 8
IDX_DIM = 64
TOPK_MAX = 256
CONV_WIDTH = 512
CONV_K = 3
N_BUCKETS = 32
MAX_DISTANCE = 128
IN_SIZES = (ATTN_WIDTH, KV_LATENT, N_HEADS_IDX * IDX_DIM, IDX_DIM, N_HEADS_IDX,
            CONV_WIDTH, CONV_WIDTH, CONV_WIDTH)

SUBLANES = 8
LANES = 128

FFN_TOKENS = 1024
FFN_CHUNK = 256
PROJ_TOKENS = 1024
Q_BLOCK = 256
BISECT_CHUNK = 2
BISECT_FIRST = 8
MAX_BISECT = 128
CAST_STEPS = 8
DSA_BATCH = 4
SWEEP_ACCS = 2
MAX_PEEL = 4096
LOG2E = math.log2(math.e)
VMEM_LIMIT = 56 * 1024 * 1024


def _rms(x, g):
    return x * lax.rsqrt(jnp.mean(x * x, axis=-1, keepdims=True) + EPS) * g


def _const_spec(shape):
    nd = len(shape)
    return pl.BlockSpec(shape, lambda *_: (0,) * nd, pipeline_mode=pl.Buffered(1))


def _swiglu_into(acc_ref, xn, wg_ref, wu_ref, wd_ref):
    d_ff = wg_ref.shape[1]
    for c in range(d_ff // FFN_CHUNK):
        sl = slice(c * FFN_CHUNK, (c + 1) * FFN_CHUNK)
        g = jnp.dot(xn, wg_ref[:, sl], preferred_element_type=F32)
        u = jnp.dot(xn, wu_ref[:, sl], preferred_element_type=F32)
        a = (g * jax.nn.sigmoid(g) * u).astype(BF16)
        d = jnp.dot(a, wd_ref[sl, :], preferred_element_type=F32)
        if c == 0:
            acc_ref[...] = d
        else:
            acc_ref[...] += d


def _ffn1_kernel(x_ref, g_ref, wg_ref, wu_ref, wd_ref, o_ref, acc_ref):
    x = x_ref[...]
    xn = _rms(x, g_ref[...]).astype(BF16)
    _swiglu_into(acc_ref, xn, wg_ref, wu_ref, wd_ref)
    o_ref[...] = x + 0.5 * acc_ref[...]


def _out_ffn2_kernel(h_ref, a_ref, c_ref, woa_ref, woc_ref, g_ref, wg_ref, wu_ref, wd_ref,
                     gf_ref, o_ref, acc_ref, *, final_norm):
    h = (h_ref[...]
         + jnp.dot(a_ref[...], woa_ref[...], preferred_element_type=F32)
         + jnp.dot(c_ref[...], woc_ref[...], preferred_element_type=F32))
    xn = _rms(h, g_ref[...]).astype(BF16)
    _swiglu_into(acc_ref, xn, wg_ref, wu_ref, wd_ref)
    h = h + 0.5 * acc_ref[...]
    o_ref[...] = _rms(h, gf_ref[...]) if final_norm else h


def _ffn1(x2, g, wg, wu, wd):
    n, d = x2.shape
    tile = pl.BlockSpec((FFN_TOKENS, d), lambda i: (i, 0))
    return pl.pallas_call(
        _ffn1_kernel,
        grid=(n // FFN_TOKENS,),
        in_specs=[tile, _const_spec(g.shape), _const_spec(wg.shape), _const_spec(wu.shape),
                  _const_spec(wd.shape)],
        out_specs=tile,
        out_shape=jax.ShapeDtypeStruct((n, d), F32),
        scratch_shapes=[pltpu.VMEM((FFN_TOKENS, d), F32)],
        compiler_params=pltpu.CompilerParams(dimension_semantics=("arbitrary",),
                                             vmem_limit_bytes=VMEM_LIMIT),
        name="ffn1",
    )(x2, g, wg, wu, wd)


def _out_ffn2(h2, a2, c2, woa, woc, g, wg, wu, wd, gf, final_norm):
    n, d = h2.shape
    tile = pl.BlockSpec((FFN_TOKENS, d), lambda i: (i, 0))
    mix = pl.BlockSpec((FFN_TOKENS, a2.shape[1]), lambda i: (i, 0))
    return pl.pallas_call(
        functools.partial(_out_ffn2_kernel, final_norm=final_norm),
        grid=(n // FFN_TOKENS,),
        in_specs=[tile, mix, mix, _const_spec(woa.shape), _const_spec(woc.shape),
                  _const_spec(g.shape), _const_spec(wg.shape), _const_spec(wu.shape),
                  _const_spec(wd.shape), _const_spec(gf.shape)],
        out_specs=tile,
        out_shape=jax.ShapeDtypeStruct((n, d), F32),
        scratch_shapes=[pltpu.VMEM((FFN_TOKENS, d), F32)],
        compiler_params=pltpu.CompilerParams(dimension_semantics=("arbitrary",),
                                             vmem_limit_bytes=VMEM_LIMIT),
        name="out_ffn2",
    )(h2, a2, c2, woa, woc, g, wg, wu, wd, gf)


_C_KV = 0
_C_KI = _C_KV + KV_LATENT
_C_GB = _C_KI + LANES
_C_GC = _C_GB + CONV_WIDTH
_C_X = _C_GC + CONV_WIDTH
_C_END = _C_X + CONV_WIDTH
_R_Q = 0
_R_KV = _R_Q + ATTN_WIDTH
_R_QI = _R_KV + KV_LATENT
_R_WI = _R_QI + N_HEADS_IDX * IDX_DIM
_R_END = _R_WI + 2 * SUBLANES
KVT_ROWS = KV_LATENT + 2 * SUBLANES


def _proj_kernel(h_ref, gm_ref, wcat_ref, wt_ref, wukt_ref, gkv_ref, gkvc_ref, cw_ref, gco_ref,
                 qat_ref, ckv_ref, ckvt_ref, ki_ref, qit_ref, wit_ref, convn_ref, carry_ref):
    tp = h_ref.shape[0]

    @pl.when(pl.program_id(1) == 0)
    def _():
        carry_ref[...] = jnp.zeros_like(carry_ref)

    un = _rms(h_ref[...], gm_ref[...]).astype(BF16)
    p = jnp.dot(un, wcat_ref[...], preferred_element_type=F32)
    ckv_ref[...] = _rms(p[:, _C_KV:_C_KI], gkv_ref[...]).astype(BF16)
    ki_ref[...] = p[:, _C_KI:_C_KI + IDX_DIM].astype(BF16)

    v = p[:, _C_GC:_C_X] * p[:, _C_X:_C_END]
    row = lax.broadcasted_iota(jnp.int32, v.shape, 0)
    prev1 = jnp.broadcast_to(carry_ref[SUBLANES - 1:SUBLANES, :], v.shape)
    prev2 = jnp.broadcast_to(carry_ref[SUBLANES - 2:SUBLANES - 1, :], v.shape)
    v1 = jnp.where(row == 0, prev1, pltpu.roll(v, 1, 0))
    v2 = jnp.where(row == 0, prev2, jnp.where(row == 1, prev1, pltpu.roll(v, 2, 0)))
    carry_ref[...] = v[tp - SUBLANES:, :]
    y = cw_ref[0:1, :] * v2 + cw_ref[1:2, :] * v1 + cw_ref[2:3, :] * v
    conv = p[:, _C_GB:_C_GC] * y
    convn_ref[...] = _rms(conv, gco_ref[...]).astype(BF16)

    pt = lax.dot_general(wt_ref[...], un, (((1,), (1,)), ((), ())),
                         preferred_element_type=F32)
    qt = pt[_R_Q:_R_KV, :].astype(BF16)
    for h in range(N_HEADS):
        qah = jnp.dot(wukt_ref[h], qt[h * HEAD_DIM:(h + 1) * HEAD_DIM, :],
                      preferred_element_type=F32)
        qat_ref[h * KV_LATENT:(h + 1) * KV_LATENT, :] = (qah * (HEAD_DIM ** -0.5 * LOG2E)).astype(BF16)
    kvt = pt[_R_KV:_R_QI, :]
    kvt = kvt * lax.rsqrt(jnp.mean(kvt * kvt, axis=0, keepdims=True) + EPS) * gkvc_ref[...]
    tk = ckvt_ref.shape[2]
    tail = jnp.concatenate([jnp.ones((SUBLANES, tk), F32), jnp.zeros((SUBLANES, tk), F32)], axis=0)
    for kb in range(tp // tk):
        blk = jnp.concatenate([kvt[:, kb * tk:(kb + 1) * tk], tail], axis=0)
        ckvt_ref[kb] = blk.astype(BF16)
    qit_ref[...] = pt[_R_QI:_R_WI, :].astype(BF16)
    wit_ref[...] = pt[_R_WI:_R_WI + N_HEADS_IDX, :] * ((N_HEADS_IDX * IDX_DIM) ** -0.5)


def _proj(h1, gm, wcat, wt, wukt, gkv, cw, gco):
    b, s, d = h1.shape
    tp = PROJ_TOKENS
    nq = N_HEADS_IDX * IDX_DIM
    nqa = N_HEADS * KV_LATENT
    gkvc = gkv.reshape(KV_LATENT, 1)
    tok = lambda w: pl.BlockSpec((None, tp, w), lambda bi, ti: (bi, ti, 0))
    tokt = lambda r: pl.BlockSpec((None, r, tp), lambda bi, ti: (bi, 0, ti))
    out_shape = (
        jax.ShapeDtypeStruct((b, nqa, s), BF16),
        jax.ShapeDtypeStruct((b, s, KV_LATENT), BF16),
        jax.ShapeDtypeStruct((b, s // Q_BLOCK, KVT_ROWS, Q_BLOCK), BF16),
        jax.ShapeDtypeStruct((b, s, IDX_DIM), BF16),
        jax.ShapeDtypeStruct((b, nq, s), BF16),
        jax.ShapeDtypeStruct((b, N_HEADS_IDX, s), F32),
        jax.ShapeDtypeStruct((b, s, CONV_WIDTH), BF16),
    )
    out_specs = (
        tokt(nqa), tok(KV_LATENT),
        pl.BlockSpec((None, tp // Q_BLOCK, KVT_ROWS, Q_BLOCK), lambda bi, ti: (bi, ti, 0, 0)),
        tok(IDX_DIM), tokt(nq), tokt(N_HEADS_IDX), tok(CONV_WIDTH),
    )
    return pl.pallas_call(
        _proj_kernel,
        grid=(b, s // tp),
        in_specs=[tok(d), _const_spec(gm.shape), _const_spec(wcat.shape), _const_spec(wt.shape),
                  _const_spec(wukt.shape), _const_spec(gkv.shape), _const_spec(gkvc.shape),
                  _const_spec(cw.shape), _const_spec(gco.shape)],
        out_specs=out_specs,
        out_shape=out_shape,
        scratch_shapes=[pltpu.VMEM((SUBLANES, CONV_WIDTH), F32)],
        compiler_params=pltpu.CompilerParams(dimension_semantics=("arbitrary", "arbitrary"),
                                             vmem_limit_bytes=VMEM_LIMIT),
        name="proj",
    )(h1, gm, wcat, wt, wukt, gkv, gkvc, cw, gco)


def _bucket_starts():
    e = N_BUCKETS // 2
    n = N_BUCKETS - e
    assert MAX_DISTANCE % e == 0
    r = MAX_DISTANCE // e
    starts, d = [], e
    for k in range(1, n):
        while d ** n < e ** n * r ** k:
            d += 1
        starts.append(d)
    return starts


def _rel_bucket(dist):
    max_exact = N_BUCKETS // 2
    large = max_exact
    for start in _bucket_starts():
        large = large + (dist >= start).astype(jnp.int32)
    return jnp.where(dist < max_exact, dist, large)


def _bias_kernel(rb_ref, o_ref):
    h = pl.program_id(0)
    tk, tq = o_ref.shape
    s = lax.broadcasted_iota(jnp.int32, (tk, tq), 0)
    t = lax.broadcasted_iota(jnp.int32, (tk, tq), 1)
    bucket = _rel_bucket(jnp.maximum(t - s + tq, 0))
    val = jnp.zeros((tk, tq), F32)
    for b in range(N_BUCKETS):
        val = jnp.where(bucket == b, rb_ref[b, h], val)
    o_ref[...] = (val - rb_ref[N_BUCKETS - 1, h]) * LOG2E


def _bias_tiles(rel_bias, tq):
    return pl.pallas_call(
        _bias_kernel,
        grid=(N_HEADS,),
        in_specs=[pl.BlockSpec(memory_space=pltpu.SMEM)],
        out_specs=pl.BlockSpec((None, 2 * tq, tq), lambda h: (h, 0, 0)),
        out_shape=jax.ShapeDtypeStruct((N_HEADS, 2 * tq, tq), F32),
        compiler_params=pltpu.CompilerParams(dimension_semantics=("arbitrary",)),
        name="bias_tiles",
    )(rel_bias)


def _dsa_kernel(qit_ref, wit_ref, qat_ref, ki_ref, ckv_ref, ckvt_ref, bias_ref, wuvt_ref, gout_ref,
                o_ref, sc_ref, mask_ref, acc_ref, m_ref, *, n_keep):
    nbat, tq = o_ref.shape[0], o_ref.shape[1]
    tw = nbat * tq
    i = pl.program_id(1)
    nblk = i + 1
    q0 = i * tq
    kf = float(n_keep)
    key_in_blk = lax.broadcasted_iota(jnp.int32, (tq, tq), 0)
    qry_in_blk = lax.broadcasted_iota(jnp.int32, (tq, tq), 1)
    ind = lambda mask: jnp.where(mask, 1.0, 0.0)
    batches = lambda f: jnp.concatenate([f(g) for g in range(nbat)], axis=1)
    part = lambda a: a.reshape(tq // (SWEEP_ACCS * SUBLANES), SWEEP_ACCS, SUBLANES, tw)
    fold = lambda a: part(a).sum(axis=0)
    fold_max = lambda a: part(a).max(axis=0)
    total = lambda a: a.sum(axis=0).sum(axis=0, keepdims=True)
    total_max = lambda a: a.max(axis=0).max(axis=0, keepdims=True)
    zeros8 = jnp.zeros((SWEEP_ACCS, SUBLANES, tw), F32)
    ninf8 = jnp.full((SWEEP_ACCS, SUBLANES, tw), -jnp.inf, F32)

    def score_body(j, carry):
        mn, mx = carry
        k0 = pl.multiple_of(j * tq, tq)
        causal1 = (k0 + key_in_blk) <= (q0 + qry_in_blk)

        def one(g):
            kib = ki_ref[g, pl.ds(k0, tq), :]
            acc = jnp.zeros((tq, tq), F32)
            for h in range(N_HEADS_IDX):
                d = jnp.dot(kib, qit_ref[g, h * IDX_DIM:(h + 1) * IDX_DIM, :],
                            preferred_element_type=F32)
                acc = acc + jnp.maximum(d, 0.0) * wit_ref[g, h:h + 1, :]
            return acc
        acc = batches(one)
        causal = batches(lambda g: causal1)
        masked = jnp.where(causal, acc, -jnp.inf)
        sc_ref[j] = masked
        mx = jnp.maximum(mx, jnp.max(masked, axis=0, keepdims=True))
        mn = jnp.minimum(mn, jnp.min(jnp.where(causal, acc, jnp.inf), axis=0, keepdims=True))
        return mn, mx

    def over_blocks(body, init):
        def pair(pi, c):
            return body(2 * pi + 1, body(2 * pi, c))
        c = lax.fori_loop(0, lax.shift_right_logical(nblk, 1), pair, init)
        return lax.cond((nblk & 1) == 1, lambda c: body(nblk - 1, c), lambda c: c, c)

    mn, mx = over_blocks(score_body,
                         (jnp.full((1, tw), jnp.inf, F32), jnp.full((1, tw), -jnp.inf, F32)))

    def count(bound, strict=False):
        def one(j, g):
            cols = slice(g * tq, (g + 1) * tq)
            blk = sc_ref[j, :, cols]
            hit = blk > bound[:, cols] if strict else blk >= bound[:, cols]
            return ind(hit).reshape(tq // (SWEEP_ACCS * SUBLANES), SWEEP_ACCS, SUBLANES, tq).sum(axis=0)
        a8 = lax.fori_loop(
            0, nblk,
            lambda j, a: a + jnp.concatenate([one(j, g) for g in range(nbat)], axis=-1), zeros8)
        return total(a8)

    def any_set(flag):
        return jnp.max(flag) > 0.0

    c_ge0 = count(jnp.zeros((1, tw), F32))
    c_gt0 = count(jnp.zeros((1, tw), F32), strict=True)
    ncausal = batches(lambda g: (q0 + lax.broadcasted_iota(jnp.int32, (1, tq), 1) + 1).astype(F32))
    small = ncausal <= kf
    neg = c_ge0 < kf
    zero_tie = (c_gt0 < kf) & ~neg
    lo = jnp.where(small | neg, mn, 0.0)
    clo = jnp.where(small | neg, ncausal, c_ge0)
    above_mx = mx + jnp.abs(mx) * 2.0 ** -22 + 1e-37
    hi = jnp.where(neg, 0.0, above_mx)
    searching = ~small & ~zero_tie

    def bis_cond(st):
        return (st[0] < MAX_BISECT) & st[1]

    def bis_passes(_, st):
        lo, hi, clo, active = st
        for _ in range(BISECT_CHUNK):
            mid = lo + (hi - lo) * 0.5
            live = (active > 0.0) & (mid > lo) & (mid < hi)
            c = count(mid)
            up = live & (c >= kf)
            lo = jnp.where(up, mid, lo)
            clo = jnp.where(up, c, clo)
            hi = jnp.where(live & (c < kf), mid, hi)
            active = ind(live & (clo > kf))
        return lo, hi, clo, active

    def bis_body(st):
        lo, hi, clo, active = bis_passes(0, st[2:])
        return st[0] + 1, any_set(active), lo, hi, clo, active

    active = ind(searching & (clo > kf))
    st = lax.cond(any_set(active),
                  lambda st: lax.fori_loop(0, BISECT_FIRST, bis_passes, st), lambda st: st,
                  (lo, hi, clo, active))
    st = lax.while_loop(bis_cond, bis_body, (jnp.int32(0), any_set(st[3])) + st)
    _, _, lo, hi, clo, _ = st

    unres = ind(searching & (clo > kf))

    def peel_cond(st):
        return (st[0] < MAX_PEEL) & st[1]

    def peel_body(st):
        it, _, lo, clo, cand, unres = st

        def body(j, s2):
            a8, m8 = s2
            blk = sc_ref[j]
            ge = blk >= cand
            below = jnp.where((blk >= lo) & ~ge, blk, -jnp.inf)
            return a8 + fold(ind(ge)), jnp.maximum(m8, fold_max(below))
        a8, m8 = lax.fori_loop(0, nblk, body, (zeros8, ninf8))
        c = total(a8)
        on = unres > 0.0
        found = on & (c >= kf)
        down = on & (c < kf)
        lo = jnp.where(found, cand, lo)
        clo = jnp.where(found, c, clo)
        cand = jnp.where(down, total_max(m8), cand)
        return it + 1, any_set(ind(down)), lo, clo, cand, ind(down)

    st = lax.while_loop(peel_cond, peel_body, (jnp.int32(0), any_set(unres), lo, clo, hi, unres))
    _, _, lo, clo, _, _ = st

    tied = any_set(ind(~small & (clo > kf)))

    @pl.when(jnp.logical_not(tied))
    def _():
        def body(j, c):
            mask_ref[j] = jnp.where(sc_ref[j] >= lo, 0.0, -jnp.inf)
            return c
        lax.fori_loop(0, nblk, body, 0)

    @pl.when(tied)
    def _():
        need = kf - count(lo, strict=True)
        earlier = ind(qry_in_blk < key_in_blk).astype(BF16)

        def body(j, seen):
            blk = sc_ref[j]
            eq = ind(blk == lo)
            rank = seen + jnp.dot(earlier, eq.astype(BF16), preferred_element_type=F32)
            keep = (blk > lo) | ((blk == lo) & (rank < need))
            mask_ref[j] = jnp.where(keep, 0.0, -jnp.inf)
            return seen + total(fold(eq))
        over_blocks(body, jnp.zeros((1, tw), F32))

    m_ref[...] = jnp.full(m_ref.shape, -jnp.inf, F32)
    acc_ref[...] = jnp.zeros(acc_ref.shape, F32)

    def attend(j, nkb, bias):
        tk = nkb * tq
        k0 = pl.multiple_of(j * tq, tq)
        heads = lambda f: jnp.concatenate([f(h) for h in range(N_HEADS)], axis=1)
        for g in range(nbat):
            cols = slice(g * tq, (g + 1) * tq)
            kv = ckv_ref[g, pl.ds(k0, tk), :]
            kvt1 = jnp.concatenate([ckvt_ref[g, j + b] for b in range(nkb)], axis=1)
            mask = jnp.concatenate([mask_ref[j + b, :, cols] for b in range(nkb)], axis=0)
            qa = heads(lambda h: qat_ref[g, h * KV_LATENT:(h + 1) * KV_LATENT, :])
            s = jnp.dot(kv, qa, preferred_element_type=F32)
            if bias is not None:
                s = s + heads(bias)
            s = s + heads(lambda h: mask)
            m_prev = m_ref[g]
            m_new = jnp.maximum(m_prev, jnp.max(s, axis=0, keepdims=True))
            m_safe = jnp.where(m_new == -jnp.inf, 0.0, m_new)
            alpha = jnp.exp2(m_prev - m_safe)
            p = jnp.exp2(s - m_safe).astype(BF16)
            m_ref[g] = m_new
            pv = jnp.dot(kvt1, p, preferred_element_type=F32)
            acc_ref[g] = alpha * acc_ref[g] + pv

    nfar = jnp.maximum(i - 1, 0)

    def far_pair(pi, c):
        attend(2 * pi, 2, None)
        return c
    lax.fori_loop(0, lax.shift_right_logical(nfar, 1), far_pair, 0)

    @pl.when((nfar & 1) == 1)
    def _():
        attend(nfar - 1, 1, None)

    @pl.when(i >= 1)
    def _():
        attend(i - 1, 2, lambda h: bias_ref[h])

    @pl.when(i == 0)
    def _():
        attend(0, 1, lambda h: bias_ref[h, tq:, :])

    for g in range(nbat):
        outs = []
        for h in range(N_HEADS):
            a = acc_ref[g, :, h * tq:(h + 1) * tq]
            ctx = (a[:KV_LATENT, :] / a[KV_LATENT:KV_LATENT + 1, :]).astype(BF16)
            outs.append(jnp.dot(wuvt_ref[h], ctx, preferred_element_type=F32))
        out = jnp.concatenate(outs, axis=0).T
        o_ref[g] = _rms(out, gout_ref[...]).astype(BF16)


def _dsa(qit, wit, qat, ki, ckv, ckvt, bias, wuvt, gout, n_keep):
    b, s, _ = ckv.shape
    tq = Q_BLOCK
    nb = s // tq
    nq = N_HEADS_IDX * IDX_DIM
    nbat = DSA_BATCH if b % DSA_BATCH == 0 else 1
    colblk = lambda r: pl.BlockSpec((nbat, r, tq), lambda bi, i: (bi, 0, i))
    return pl.pallas_call(
        functools.partial(_dsa_kernel, n_keep=n_keep),
        grid=(b // nbat, nb),
        in_specs=[
            colblk(nq), colblk(N_HEADS_IDX), colblk(N_HEADS * KV_LATENT),
            pl.BlockSpec((nbat, s, IDX_DIM), lambda bi, i: (bi, 0, 0)),
            pl.BlockSpec((nbat, s, KV_LATENT), lambda bi, i: (bi, 0, 0)),
            pl.BlockSpec((nbat, nb, KVT_ROWS, tq), lambda bi, i: (bi, 0, 0, 0)),
            _const_spec(bias.shape), _const_spec(wuvt.shape), _const_spec(gout.shape),
        ],
        out_specs=pl.BlockSpec((nbat, tq, ATTN_WIDTH), lambda bi, i: (bi, i, 0)),
        out_shape=jax.ShapeDtypeStruct((b, s, ATTN_WIDTH), BF16),
        scratch_shapes=[
            pltpu.VMEM((nb, tq, nbat * tq), F32),
            pltpu.VMEM((nb, tq, nbat * tq), F32),
            pltpu.VMEM((nbat, KVT_ROWS, N_HEADS * tq), F32),
            pltpu.VMEM((nbat, 1, N_HEADS * tq), F32),
        ],
        compiler_params=pltpu.CompilerParams(dimension_semantics=("arbitrary", "arbitrary"),
                                             vmem_limit_bytes=VMEM_LIMIT),
        name="dsa",
    )(qit, wit, qat, ki, ckv, ckvt, bias, wuvt, gout)


def _cast_kernel(*refs):
    n = len(refs) // 2
    for src, dst in zip(refs[:n], refs[n:]):
        dst[...] = src[...].astype(BF16)


def _to_bf16(*ws):
    r, c = ws[0].shape
    spec = pl.BlockSpec((r // CAST_STEPS, c), lambda i: (i, 0))
    return pl.pallas_call(
        _cast_kernel,
        grid=(CAST_STEPS,),
        in_specs=[spec] * len(ws),
        out_specs=[spec] * len(ws),
        out_shape=[jax.ShapeDtypeStruct(w.shape, BF16) for w in ws],
        compiler_params=pltpu.CompilerParams(dimension_semantics=("arbitrary",),
                                             vmem_limit_bytes=VMEM_LIMIT),
        name="to_bf16",
    )(*ws)


def kernel(x, ffn1_norm, ffn1_w_gate, ffn1_w_up, ffn1_w_down, mix_norm, w_in, kv_norm, w_uk, w_uv, rel_bias, conv_w, attn_out_norm, conv_out_norm, w_out, ffn2_norm, ffn2_w_gate, ffn2_w_up, ffn2_w_down, final_norm):
    b, s, d = x.shape
    depth = ffn1_norm.shape[0]
    n_keep = min(TOPK_MAX, s // 4)
    assert s % Q_BLOCK == 0 and s % PROJ_TOKENS == 0 and (b * s) % FFN_TOKENS == 0
    assert Q_BLOCK >= MAX_DISTANCE and n_keep <= Q_BLOCK
    row = lambda v: v.reshape(1, -1).astype(F32)
    offs = [0]
    for w in IN_SIZES:
        offs.append(offs[-1] + w)

    bias = _bias_tiles(rel_bias.astype(F32), Q_BLOCK)
    h = x.astype(F32).reshape(b * s, d)
    for l in range(depth):
        wg1, wu1, wg2, wu2 = _to_bf16(ffn1_w_gate[l], ffn1_w_up[l], ffn2_w_gate[l], ffn2_w_up[l])
        wd1, wd2 = _to_bf16(ffn1_w_down[l], ffn2_w_down[l])
        h = _ffn1(h, row(ffn1_norm[l]), wg1, wu1, wd1)

        wi = w_in[l]
        col = lambda k: wi[:, offs[k]:offs[k + 1]]
        wcat = jnp.concatenate(
            [col(1), col(3), jnp.zeros((d, LANES - IDX_DIM), wi.dtype), col(5), col(6), col(7)],
            axis=1).astype(BF16)
        wt = jnp.concatenate(
            [col(0).T, col(1).T, col(2).T, col(4).T,
             jnp.zeros((_R_END - _R_WI - N_HEADS_IDX, d), wi.dtype)], axis=0).astype(BF16)
        wukt = jnp.transpose(w_uk[l], (1, 0, 2)).astype(BF16)
        wuvt = jnp.transpose(w_uv[l], (1, 2, 0)).astype(BF16)

        qat, ckv, ckvt, ki, qit, wit, convn = _proj(
            h.reshape(b, s, d), row(mix_norm[l]), wcat, wt, wukt, row(kv_norm[l]),
            conv_w[l].astype(F32), row(conv_out_norm[l]))
        attn = _dsa(qit, wit, qat, ki, ckv, ckvt, bias, wuvt, row(attn_out_norm[l]), n_keep)

        last = l == depth - 1
        h = _out_ffn2(h, attn.reshape(b * s, ATTN_WIDTH), convn.reshape(b * s, CONV_WIDTH),
                      w_out[l][:ATTN_WIDTH].astype(BF16), w_out[l][ATTN_WIDTH:].astype(BF16),
                      row(ffn2_norm[l]), wg2, wu2, wd2, row(final_norm), final_norm=last)
    return h.reshape(b, s, d).astype(x.dtype)
```

```python
import functools
import math

import jax
import jax.numpy as jnp
from jax import lax
from jax.experimental import pallas as pl
from jax.experimental.pallas import tpu as pltpu

F32 = jnp.float32
BF16 = jnp.bfloat16

EPS = 1e-6
N_HEADS = 8
HEAD_DIM = 64
ATTN_WIDTH = N_HEADS * HEAD_DIM
KV_LATENT = 128
N_HEADS_IDX = 8
IDX_DIM = 64
TOPK_MAX = 256
CONV_WIDTH = 512
CONV_K = 3
N_BUCKETS = 32
MAX_DISTANCE = 128
IN_SIZES = (ATTN_WIDTH, KV_LATENT, N_HEADS_IDX * IDX_DIM, IDX_DIM, N_HEADS_IDX,
            CONV_WIDTH, CONV_WIDTH, CONV_WIDTH)

SUBLANES = 8
LANES = 128

FFN_TOKENS = 1024
FFN_CHUNK = 256
PROJ_TOKENS = 1024
Q_BLOCK = 256
BISECT_CHUNK = 2
BISECT_FIRST = 8
MAX_BISECT = 128
CAST_STEPS = 8
DSA_BATCH = 4
SWEEP_ACCS = 2
MAX_PEEL = 4096
LOG2E = math.log2(math.e)
VMEM_LIMIT = 56 * 1024 * 1024


def _rms(x, g):
    return x * lax.rsqrt(jnp.mean(x * x, axis=-1, keepdims=True) + EPS) * g


def _const_spec(shape):
    nd = len(shape)
    return pl.BlockSpec(shape, lambda *_: (0,) * nd, pipeline_mode=pl.Buffered(1))


def _swiglu_into(acc_ref, xn, wg_ref, wu_ref, wd_ref):
    d_ff = wg_ref.shape[1]
    for c in range(d_ff // FFN_CHUNK):
        sl = slice(c * FFN_CHUNK, (c + 1) * FFN_CHUNK)
        g = jnp.dot(xn, wg_ref[:, sl], preferred_element_type=F32)
        u = jnp.dot(xn, wu_ref[:, sl], preferred_element_type=F32)
        a = (g * jax.nn.sigmoid(g) * u).astype(BF16)
        d = jnp.dot(a, wd_ref[sl, :], preferred_element_type=F32)
        if c == 0:
            acc_ref[...] = d
        else:
            acc_ref[...] += d


def _ffn1_kernel(x_ref, g_ref, wg_ref, wu_ref, wd_ref, o_ref, acc_ref):
    x = x_ref[...]
    xn = _rms(x, g_ref[...]).astype(BF16)
    _swiglu_into(acc_ref, xn, wg_ref, wu_ref, wd_ref)
    o_ref[...] = x + 0.5 * acc_ref[...]


def _out_ffn2_kernel(h_ref, a_ref, c_ref, woa_ref, woc_ref, g_ref, wg_ref, wu_ref, wd_ref,
                     gf_ref, o_ref, acc_ref, *, final_norm):
    h = (h_ref[...]
         + jnp.dot(a_ref[...], woa_ref[...], preferred_element_type=F32)
         + jnp.dot(c_ref[...], woc_ref[...], preferred_element_type=F32))
    xn = _rms(h, g_ref[...]).astype(BF16)
    _swiglu_into(acc_ref, xn, wg_ref, wu_ref, wd_ref)
    h = h + 0.5 * acc_ref[...]
    o_ref[...] = _rms(h, gf_ref[...]) if final_norm else h


def _ffn1(x2, g, wg, wu, wd):
    n, d = x2.shape
    tile = pl.BlockSpec((FFN_TOKENS, d), lambda i: (i, 0))
    return pl.pallas_call(
        _ffn1_kernel,
        grid=(n // FFN_TOKENS,),
        in_specs=[tile, _const_spec(g.shape), _const_spec(wg.shape), _const_spec(wu.shape),
                  _const_spec(wd.shape)],
        out_specs=tile,
        out_shape=jax.ShapeDtypeStruct((n, d), F32),
        scratch_shapes=[pltpu.VMEM((FFN_TOKENS, d), F32)],
        compiler_params=pltpu.CompilerParams(dimension_semantics=("arbitrary",),
                                             vmem_limit_bytes=VMEM_LIMIT),
        name="ffn1",
    )(x2, g, wg, wu, wd)


def _out_ffn2(h2, a2, c2, woa, woc, g, wg, wu, wd, gf, final_norm):
    n, d = h2.shape
    tile = pl.BlockSpec((FFN_TOKENS, d), lambda i: (i, 0))
    mix = pl.BlockSpec((FFN_TOKENS, a2.shape[1]), lambda i: (i, 0))
    return pl.pallas_call(
        functools.partial(_out_ffn2_kernel, final_norm=final_norm),
        grid=(n // FFN_TOKENS,),
        in_specs=[tile, mix, mix, _const_spec(woa.shape), _const_spec(woc.shape),
                  _const_spec(g.shape), _const_spec(wg.shape), _const_spec(wu.shape),
                  _const_spec(wd.shape), _const_spec(gf.shape)],
        out_specs=tile,
        out_shape=jax.ShapeDtypeStruct((n, d), F32),
        scratch_shapes=[pltpu.VMEM((FFN_TOKENS, d), F32)],
        compiler_params=pltpu.CompilerParams(dimension_semantics=("arbitrary",),
                                             vmem_limit_bytes=VMEM_LIMIT),
        name="out_ffn2",
    )(h2, a2, c2, woa, woc, g, wg, wu, wd, gf)


_C_KV = 0
_C_KI = _C_KV + KV_LATENT
_C_GB = _C_KI + LANES
_C_GC = _C_GB + CONV_WIDTH
_C_X = _C_GC + CONV_WIDTH
_C_END = _C_X + CONV_WIDTH
_R_Q = 0
_R_KV = _R_Q + ATTN_WIDTH
_R_QI = _R_KV + KV_LATENT
_R_WI = _R_QI + N_HEADS_IDX * IDX_DIM
_R_END = _R_WI + 2 * SUBLANES
KVT_ROWS = KV_LATENT + 2 * SUBLANES


def _proj_kernel(h_ref, gm_ref, wcat_ref, wt_ref, wukt_ref, gkv_ref, gkvc_ref, cw_ref, gco_ref,
                 qat_ref, ckv_ref, ckvt_ref, ki_ref, qit_ref, wit_ref, convn_ref, carry_ref):
    tp = h_ref.shape[0]

    @pl.when(pl.program_id(1) == 0)
    def _():
        carry_ref[...] = jnp.zeros_like(carry_ref)

    un = _rms(h_ref[...], gm_ref[...]).astype(BF16)
    p = jnp.dot(un, wcat_ref[...], preferred_element_type=F32)
    ckv_ref[...] = _rms(p[:, _C_KV:_C_KI], gkv_ref[...]).astype(BF16)
    ki_ref[...] = p[:, _C_KI:_C_KI + IDX_DIM].astype(BF16)

    v = p[:, _C_GC:_C_X] * p[:, _C_X:_C_END]
    row = lax.broadcasted_iota(jnp.int32, v.shape, 0)
    prev1 = jnp.broadcast_to(carry_ref[SUBLANES - 1:SUBLANES, :], v.shape)
    prev2 = jnp.broadcast_to(carry_ref[SUBLANES - 2:SUBLANES - 1, :], v.shape)
    v1 = jnp.where(row == 0, prev1, pltpu.roll(v, 1, 0))
    v2 = jnp.where(row == 0, prev2, jnp.where(row == 1, prev1, pltpu.roll(v, 2, 0)))
    carry_ref[...] = v[tp - SUBLANES:, :]
    y = cw_ref[0:1, :] * v2 + cw_ref[1:2, :] * v1 + cw_ref[2:3, :] * v
    conv = p[:, _C_GB:_C_GC] * y
    convn_ref[...] = _rms(conv, gco_ref[...]).astype(BF16)

    pt = lax.dot_general(wt_ref[...], un, (((1,), (1,)), ((), ())),
                         preferred_element_type=F32)
    qt = pt[_R_Q:_R_KV, :].astype(BF16)
    for h in range(N_HEADS):
        qah = jnp.dot(wukt_ref[h], qt[h * HEAD_DIM:(h + 1) * HEAD_DIM, :],
                      preferred_element_type=F32)
        qat_ref[h * KV_LATENT:(h + 1) * KV_LATENT, :] = (qah * (HEAD_DIM ** -0.5 * LOG2E)).astype(BF16)
    kvt = pt[_R_KV:_R_QI, :]
    kvt = kvt * lax.rsqrt(jnp.mean(kvt * kvt, axis=0, keepdims=True) + EPS) * gkvc_ref[...]
    tk = ckvt_ref.shape[2]
    tail = jnp.concatenate([jnp.ones((SUBLANES, tk), F32), jnp.zeros((SUBLANES, tk), F32)], axis=0)
    for kb in range(tp // tk):
        blk = jnp.concatenate([kvt[:, kb * tk:(kb + 1) * tk], tail], axis=0)
        ckvt_ref[kb] = blk.astype(BF16)
    qit_ref[...] = pt[_R_QI:_R_WI, :].astype(BF16)
    wit_ref[...] = pt[_R_WI:_R_WI + N_HEADS_IDX, :] * ((N_HEADS_IDX * IDX_DIM) ** -0.5)


def _proj(h1, gm, wcat, wt, wukt, gkv, cw, gco):
    b, s, d = h1.shape
    tp = PROJ_TOKENS
    nq = N_HEADS_IDX * IDX_DIM
    nqa = N_HEADS * KV_LATENT
    gkvc = gkv.reshape(KV_LATENT, 1)
    tok = lambda w: pl.BlockSpec((None, tp, w), lambda bi, ti: (bi, ti, 0))
    tokt = lambda r: pl.BlockSpec((None, r, tp), lambda bi, ti: (bi, 0, ti))
    out_shape = (
        jax.ShapeDtypeStruct((b, nqa, s), BF16),
        jax.ShapeDtypeStruct((b, s, KV_LATENT), BF16),
        jax.ShapeDtypeStruct((b, s // Q_BLOCK, KVT_ROWS, Q_BLOCK), BF16),
        jax.ShapeDtypeStruct((b, s, IDX_DIM), BF16),
        jax.ShapeDtypeStruct((b, nq, s), BF16),
        jax.ShapeDtypeStruct((b, N_HEADS_IDX, s), F32),
        jax.ShapeDtypeStruct((b, s, CONV_WIDTH), BF16),
    )
    out_specs = (
        tokt(nqa), tok(KV_LATENT),
        pl.BlockSpec((None, tp // Q_BLOCK, KVT_ROWS, Q_BLOCK), lambda bi, ti: (bi, ti, 0, 0)),
        tok(IDX_DIM), tokt(nq), tokt(N_HEADS_IDX), tok(CONV_WIDTH),
    )
    return pl.pallas_call(
        _proj_kernel,
        grid=(b, s // tp),
        in_specs=[tok(d), _const_spec(gm.shape), _const_spec(wcat.shape), _const_spec(wt.shape),
                  _const_spec(wukt.shape), _const_spec(gkv.shape), _const_spec(gkvc.shape),
                  _const_spec(cw.shape), _const_spec(gco.shape)],
        out_specs=out_specs,
        out_shape=out_shape,
        scratch_shapes=[pltpu.VMEM((SUBLANES, CONV_WIDTH), F32)],
        compiler_params=pltpu.CompilerParams(dimension_semantics=("arbitrary", "arbitrary"),
                                             vmem_limit_bytes=VMEM_LIMIT),
        name="proj",
    )(h1, gm, wcat, wt, wukt, gkv, gkvc, cw, gco)


def _bucket_starts():
    e = N_BUCKETS // 2
    n = N_BUCKETS - e
    assert MAX_DISTANCE % e == 0
    r = MAX_DISTANCE // e
    starts, d = [], e
    for k in range(1, n):
        while d ** n < e ** n * r ** k:
            d += 1
        starts.append(d)
    return starts


def _rel_bucket(dist):
    max_exact = N_BUCKETS // 2
    large = max_exact
    for start in _bucket_starts():
        large = large + (dist >= start).astype(jnp.int32)
    return jnp.where(dist < max_exact, dist, large)


def _bias_kernel(rb_ref, o_ref):
    h = pl.program_id(0)
    tk, tq = o_ref.shape
    s = lax.broadcasted_iota(jnp.int32, (tk, tq), 0)
    t = lax.broadcasted_iota(jnp.int32, (tk, tq), 1)
    bucket = _rel_bucket(jnp.maximum(t - s + tq, 0))
    val = jnp.zeros((tk, tq), F32)
    for b in range(N_BUCKETS):
        val = jnp.where(bucket == b, rb_ref[b, h], val)
    o_ref[...] = (val - rb_ref[N_BUCKETS - 1, h]) * LOG2E


def _bias_tiles(rel_bias, tq):
    return pl.pallas_call(
        _bias_kernel,
        grid=(N_HEADS,),
        in_specs=[pl.BlockSpec(memory_space=pltpu.SMEM)],
        out_specs=pl.BlockSpec((None, 2 * tq, tq), lambda h: (h, 0, 0)),
        out_shape=jax.ShapeDtypeStruct((N_HEADS, 2 * tq, tq), F32),
        compiler_params=pltpu.CompilerParams(dimension_semantics=("arbitrary",)),
        name="bias_tiles",
    )(rel_bias)


def _dsa_kernel(qit_ref, wit_ref, qat_ref, ki_ref, ckv_ref, ckvt_ref, bias_ref, wuvt_ref, gout_ref,
                o_ref, sc_ref, mask_ref, acc_ref, m_ref, *, n_keep):
    nbat, tq = o_ref.shape[0], o_ref.shape[1]
    tw = nbat * tq
    i = pl.program_id(1)
    nblk = i + 1
    q0 = i * tq
    kf = float(n_keep)
    key_in_blk = lax.broadcasted_iota(jnp.int32, (tq, tq), 0)
    qry_in_blk = lax.broadcasted_iota(jnp.int32, (tq, tq), 1)
    ind = lambda mask: jnp.where(mask, 1.0, 0.0)
    batches = lambda f: jnp.concatenate([f(g) for g in range(nbat)], axis=1)
    part = lambda a: a.reshape(tq // (SWEEP_ACCS * SUBLANES), SWEEP_ACCS, SUBLANES, tw)
    fold = lambda a: part(a).sum(axis=0)
    fold_max = lambda a: part(a).max(axis=0)
    total = lambda a: a.sum(axis=0).sum(axis=0, keepdims=True)
    total_max = lambda a: a.max(axis=0).max(axis=0, keepdims=True)
    zeros8 = jnp.zeros((SWEEP_ACCS, SUBLANES, tw), F32)
    ninf8 = jnp.full((SWEEP_ACCS, SUBLANES, tw), -jnp.inf, F32)

    def score_body(j, carry):
        mn, mx = carry
        k0 = pl.multiple_of(j * tq, tq)
        causal1 = (k0 + key_in_blk) <= (q0 + qry_in_blk)

        def one(g):
            kib = ki_ref[g, pl.ds(k0, tq), :]
            acc = jnp.zeros((tq, tq), F32)
            for h in range(N_HEADS_IDX):
                d = jnp.dot(kib, qit_ref[g, h * IDX_DIM:(h + 1) * IDX_DIM, :],
                            preferred_element_type=F32)
                acc = acc + jnp.maximum(d, 0.0) * wit_ref[g, h:h + 1, :]
            return acc
        acc = batches(one)
        causal = batches(lambda g: causal1)
        masked = jnp.where(causal, acc, -jnp.inf)
        sc_ref[j] = masked
        mx = jnp.maximum(mx, jnp.max(masked, axis=0, keepdims=True))
        mn = jnp.minimum(mn, jnp.min(jnp.where(causal, acc, jnp.inf), axis=0, keepdims=True))
        return mn, mx

    def over_blocks(body, init):
        def pair(pi, c):
            return body(2 * pi + 1, body(2 * pi, c))
        c = lax.fori_loop(0, lax.shift_right_logical(nblk, 1), pair, init)
        return lax.cond((nblk & 1) == 1, lambda c: body(nblk - 1, c), lambda c: c, c)

    mn, mx = over_blocks(score_body,
                         (jnp.full((1, tw), jnp.inf, F32), jnp.full((1, tw), -jnp.inf, F32)))

    def count(bound, strict=False):
        def one(j, g):
            cols = slice(g * tq, (g + 1) * tq)
            blk = sc_ref[j, :, cols]
            hit = blk > bound[:, cols] if strict else blk >= bound[:, cols]
            return ind(hit).reshape(tq // (SWEEP_ACCS * SUBLANES), SWEEP_ACCS, SUBLANES, tq).sum(axis=0)
        a8 = lax.fori_loop(
            0, nblk,
            lambda j, a: a + jnp.concatenate([one(j, g) for g in range(nbat)], axis=-1), zeros8)
        return total(a8)

    def any_set(flag):
        return jnp.max(flag) > 0.0

    c_ge0 = count(jnp.zeros((1, tw), F32))
    c_gt0 = count(jnp.zeros((1, tw), F32), strict=True)
    ncausal = batches(lambda g: (q0 + lax.broadcasted_iota(jnp.int32, (1, tq), 1) + 1).astype(F32))
    small = ncausal <= kf
    neg = c_ge0 < kf
    zero_tie = (c_gt0 < kf) & ~neg
    lo = jnp.where(small | neg, mn, 0.0)
    clo = jnp.where(small | neg, ncausal, c_ge0)
    above_mx = mx + jnp.abs(mx) * 2.0 ** -22 + 1e-37
    hi = jnp.where(neg, 0.0, above_mx)
    searching = ~small & ~zero_tie

    def bis_cond(st):
        return (st[0] < MAX_BISECT) & st[1]

    def bis_passes(_, st):
        lo, hi, clo, active = st
        for _ in range(BISECT_CHUNK):
            mid = lo + (hi - lo) * 0.5
            live = (active > 0.0) & (mid > lo) & (mid < hi)
            c = count(mid)
            up = live & (c >= kf)
            lo = jnp.where(up, mid, lo)
            clo = jnp.where(up, c, clo)
            hi = jnp.where(live & (c < kf), mid, hi)
            active = ind(live & (clo > kf))
        return lo, hi, clo, active

    def bis_body(st):
        lo, hi, clo, active = bis_passes(0, st[2:])
        return st[0] + 1, any_set(active), lo, hi, clo, active

    active = ind(searching & (clo > kf))
    st = lax.cond(any_set(active),
                  lambda st: lax.fori_loop(0, BISECT_FIRST, bis_passes, st), lambda st: st,
                  (lo, hi, clo, active))
    st = lax.while_loop(bis_cond, bis_body, (jnp.int32(0), any_set(st[3])) + st)
    _, _, lo, hi, clo, _ = st

    unres = ind(searching & (clo > kf))

    def peel_cond(st):
        return (st[0] < MAX_PEEL) & st[1]

    def peel_body(st):
        it, _, lo, clo, cand, unres = st

        def body(j, s2):
            a8, m8 = s2
            blk = sc_ref[j]
            ge = blk >= cand
            below = jnp.where((blk >= lo) & ~ge, blk, -jnp.inf)
            return a8 + fold(ind(ge)), jnp.maximum(m8, fold_max(below))
        a8, m8 = lax.fori_loop(0, nblk, body, (zeros8, ninf8))
        c = total(a8)
        on = unres > 0.0
        found = on & (c >= kf)
        down = on & (c < kf)
        lo = jnp.where(found, cand, lo)
        clo = jnp.where(found, c, clo)
        cand = jnp.where(down, total_max(m8), cand)
        return it + 1, any_set(ind(down)), lo, clo, cand, ind(down)

    st = lax.while_loop(peel_cond, peel_body, (jnp.int32(0), any_set(unres), lo, clo, hi, unres))
    _, _, lo, clo, _, _ = st

    tie_cols = ind(~small & (clo > kf))
    tied = any_set(tie_cols)

    def plain_masks(cols):
        def body(j, c):
            mask_ref[j, :, cols] = jnp.where(sc_ref[j, :, cols] >= lo[:, cols], 0.0, -jnp.inf)
            return c
        lax.fori_loop(0, nblk, body, 0)

    @pl.when(jnp.logical_not(tied))
    def _():
        plain_masks(slice(0, tw))

    @pl.when(tied)
    def _():
        need = kf - count(lo, strict=True)
        earlier = ind(qry_in_blk < key_in_blk).astype(BF16)
        for g in range(nbat):
            cols = slice(g * tq, (g + 1) * tq)
            row_tied = any_set(tie_cols[:, cols])
            lo_g, need_g = lo[:, cols], need[:, cols]

            @pl.when(jnp.logical_not(row_tied))
            def _():
                plain_masks(cols)

            @pl.when(row_tied)
            def _():
                def body(j, seen):
                    blk = sc_ref[j, :, cols]
                    eq = ind(blk == lo_g)
                    rank = seen + jnp.dot(earlier, eq.astype(BF16), preferred_element_type=F32)
                    keep = (blk > lo_g) | ((blk == lo_g) & (rank < need_g))
                    mask_ref[j, :, cols] = jnp.where(keep, 0.0, -jnp.inf)
                    return seen + eq.sum(axis=0, keepdims=True)
                over_blocks(body, jnp.zeros((1, tq), F32))

    m_ref[...] = jnp.full(m_ref.shape, -jnp.inf, F32)
    acc_ref[...] = jnp.zeros(acc_ref.shape, F32)

    def attend(j, nkb, bias):
        tk = nkb * tq
        k0 = pl.multiple_of(j * tq, tq)
        heads = lambda f: jnp.concatenate([f(h) for h in range(N_HEADS)], axis=1)
        for g in range(nbat):
            cols = slice(g * tq, (g + 1) * tq)
            kv = ckv_ref[g, pl.ds(k0, tk), :]
            kvt1 = jnp.concatenate([ckvt_ref[g, j + b] for b in range(nkb)], axis=1)
            mask = jnp.concatenate([mask_ref[j + b, :, cols] for b in range(nkb)], axis=0)
            qa = heads(lambda h: qat_ref[g, h * KV_LATENT:(h + 1) * KV_LATENT, :])
            s = jnp.dot(kv, qa, preferred_element_type=F32)
            if bias is not None:
                s = s + heads(bias)
            s = s + heads(lambda h: mask)
            m_prev = m_ref[g]
            m_new = jnp.maximum(m_prev, jnp.max(s, axis=0, keepdims=True))
            m_safe = jnp.where(m_new == -jnp.inf, 0.0, m_new)
            alpha = jnp.exp2(m_prev - m_safe)
            p = jnp.exp2(s - m_safe).astype(BF16)
            m_ref[g] = m_new
            pv = jnp.dot(kvt1, p, preferred_element_type=F32)
            acc_ref[g] = alpha * acc_ref[g] + pv

    nfar = jnp.maximum(i - 1, 0)

    def far_pair(pi, c):
        attend(2 * pi, 2, None)
        return c
    lax.fori_loop(0, lax.shift_right_logical(nfar, 1), far_pair, 0)

    @pl.when((nfar & 1) == 1)
    def _():
        attend(nfar - 1, 1, None)

    @pl.when(i >= 1)
    def _():
        attend(i - 1, 2, lambda h: bias_ref[h])

    @pl.when(i == 0)
    def _():
        attend(0, 1, lambda h: bias_ref[h, tq:, :])

    for g in range(nbat):
        outs = []
        for h in range(N_HEADS):
            a = acc_ref[g, :, h * tq:(h + 1) * tq]
            ctx = (a[:KV_LATENT, :] / a[KV_LATENT:KV_LATENT + 1, :]).astype(BF16)
            outs.append(jnp.dot(wuvt_ref[h], ctx, preferred_element_type=F32))
        out = jnp.concatenate(outs, axis=0).T
        o_ref[g] = _rms(out, gout_ref[...]).astype(BF16)


def _dsa(qit, wit, qat, ki, ckv, ckvt, bias, wuvt, gout, n_keep):
    b, s, _ = ckv.shape
    tq = Q_BLOCK
    nb = s // tq
    nq = N_HEADS_IDX * IDX_DIM
    nbat = DSA_BATCH if b % DSA_BATCH == 0 else 1
    colblk = lambda r: pl.BlockSpec((nbat, r, tq), lambda bi, i: (bi, 0, i))
    return pl.pallas_call(
        functools.partial(_dsa_kernel, n_keep=n_keep),
        grid=(b // nbat, nb),
        in_specs=[
            colblk(nq), colblk(N_HEADS_IDX), colblk(N_HEADS * KV_LATENT),
            pl.BlockSpec((nbat, s, IDX_DIM), lambda bi, i: (bi, 0, 0)),
            pl.BlockSpec((nbat, s, KV_LATENT), lambda bi, i: (bi, 0, 0)),
            pl.BlockSpec((nbat, nb, KVT_ROWS, tq), lambda bi, i: (bi, 0, 0, 0)),
            _const_spec(bias.shape), _const_spec(wuvt.shape), _const_spec(gout.shape),
        ],
        out_specs=pl.BlockSpec((nbat, tq, ATTN_WIDTH), lambda bi, i: (bi, i, 0)),
        out_shape=jax.ShapeDtypeStruct((b, s, ATTN_WIDTH), BF16),
        scratch_shapes=[
            pltpu.VMEM((nb, tq, nbat * tq), F32),
            pltpu.VMEM((nb, tq, nbat * tq), F32),
            pltpu.VMEM((nbat, KVT_ROWS, N_HEADS * tq), F32),
            pltpu.VMEM((nbat, 1, N_HEADS * tq), F32),
        ],
        compiler_params=pltpu.CompilerParams(dimension_semantics=("arbitrary", "arbitrary"),
                                             vmem_limit_bytes=VMEM_LIMIT),
        name="dsa",
    )(qit, wit, qat, ki, ckv, ckvt, bias, wuvt, gout)


def _cast_kernel(*refs):
    n = len(refs) // 2
    for src, dst in zip(refs[:n], refs[n:]):
        dst[...] = src[...].astype(BF16)


def _to_bf16(*ws):
    r, c = ws[0].shape
    spec = pl.BlockSpec((r // CAST_STEPS, c), lambda i: (i, 0))
    return pl.pallas_call(
        _cast_kernel,
        grid=(CAST_STEPS,),
        in_specs=[spec] * len(ws),
        out_specs=[spec] * len(ws),
        out_shape=[jax.ShapeDtypeStruct(w.shape, BF16) for w in ws],
        compiler_params=pltpu.CompilerParams(dimension_semantics=("arbitrary",),
                                             vmem_limit_bytes=VMEM_LIMIT),
        name="to_bf16",
    )(*ws)


def kernel(x, ffn1_norm, ffn1_w_gate, ffn1_w_up, ffn1_w_down, mix_norm, w_in, kv_norm, w_uk, w_uv, rel_bias, conv_w, attn_out_norm, conv_out_norm, w_out, ffn2_norm, ffn2_w_gate, ffn2_w_up, ffn2_w_down, final_norm):
    b, s, d = x.shape
    depth = ffn1_norm.shape[0]
    n_keep = min(TOPK_MAX, s // 4)
    assert s % Q_BLOCK == 0 and s % PROJ_TOKENS == 0 and (b * s) % FFN_TOKENS == 0
    assert Q_BLOCK >= MAX_DISTANCE and n_keep <= Q_BLOCK
    row = lambda v: v.reshape(1, -1).astype(F32)
    offs = [0]
    for w in IN_SIZES:
        offs.append(offs[-1] + w)

    bias = _bias_tiles(rel_bias.astype(F32), Q_BLOCK)
    h = x.astype(F32).reshape(b * s, d)
    for l in range(depth):
        wg1, wu1, wg2, wu2 = _to_bf16(ffn1_w_gate[l], ffn1_w_up[l], ffn2_w_gate[l], ffn2_w_up[l])
        wd1, wd2 = _to_bf16(ffn1_w_down[l], ffn2_w_down[l])
        h = _ffn1(h, row(ffn1_norm[l]), wg1, wu1, wd1)

        wi = w_in[l]
        col = lambda k: wi[:, offs[k]:offs[k + 1]]
        wcat = jnp.concatenate(
            [col(1), col(3), jnp.zeros((d, LANES - IDX_DIM), wi.dtype), col(5), col(6), col(7)],
            axis=1).astype(BF16)
        wt = jnp.concatenate(
            [col(0).T, col(1).T, col(2).T, col(4).T,
             jnp.zeros((_R_END - _R_WI - N_HEADS_IDX, d), wi.dtype)], axis=0).astype(BF16)
        wukt = jnp.transpose(w_uk[l], (1, 0, 2)).astype(BF16)
        wuvt = jnp.transpose(w_uv[l], (1, 2, 0)).astype(BF16)

        qat, ckv, ckvt, ki, qit, wit, convn = _proj(
            h.reshape(b, s, d), row(mix_norm[l]), wcat, wt, wukt, row(kv_norm[l]),
            conv_w[l].astype(F32), row(conv_out_norm[l]))
        attn = _dsa(qit, wit, qat, ki, ckv, ckvt, bias, wuvt, row(attn_out_norm[l]), n_keep)

        last = l == depth - 1
        h = _out_ffn2(h, attn.reshape(b * s, ATTN_WIDTH), convn.reshape(b * s, CONV_WIDTH),
                      w_out[l][:ATTN_WIDTH].astype(BF16), w_out[l][ATTN_WIDTH:].astype(BF16),
                      row(ffn2_norm[l]), wg2, wu2, wd2, row(final_norm), final_norm=last)
    return h.reshape(b, s, d).astype(x.dtype)
```

```python
import functools
import math

import jax
import jax.numpy as jnp
from jax import lax
from jax.experimental import pallas as pl
from jax.experimental.pallas import tpu as pltpu

F32 = jnp.float32
BF16 = jnp.bfloat16

EPS = 1e-6
N_HEADS = 8
HEAD_DIM = 64
ATTN_WIDTH = N_HEADS * HEAD_DIM
KV_LATENT = 128
N_HEADS_IDX = 8
IDX_DIM = 64
TOPK_MAX = 256
CONV_WIDTH = 512
CONV_K = 3
N_BUCKETS = 32
MAX_DISTANCE = 128
IN_SIZES = (ATTN_WIDTH, KV_LATENT, N_HEADS_IDX * IDX_DIM, IDX_DIM, N_HEADS_IDX,
            CONV_WIDTH, CONV_WIDTH, CONV_WIDTH)

SUBLANES = 8
LANES = 128

FFN_TOKENS = 1024
FFN_CHUNK = 256
PROJ_TOKENS = 1024
Q_BLOCK = 256
BISECT_CHUNK = 2
BISECT_FIRST = 8
MAX_BISECT = 128
CAST_STEPS = 8
DSA_BATCH = 4
SWEEP_ACCS = 2
MAX_PEEL = 4096
LOG2E = math.log2(math.e)
VMEM_LIMIT = 56 * 1024 * 1024


def _rms(x, g):
    return x * lax.rsqrt(jnp.mean(x * x, axis=-1, keepdims=True) + EPS) * g


def _const_spec(shape):
    nd = len(shape)
    return pl.BlockSpec(shape, lambda *_: (0,) * nd, pipeline_mode=pl.Buffered(1))


def _swiglu_into(acc_ref, xn, wg_ref, wu_ref, wd_ref):
    d_ff = wg_ref.shape[1]
    for c in range(d_ff // FFN_CHUNK):
        sl = slice(c * FFN_CHUNK, (c + 1) * FFN_CHUNK)
        g = jnp.dot(xn, wg_ref[:, sl], preferred_element_type=F32)
        u = jnp.dot(xn, wu_ref[:, sl], preferred_element_type=F32)
        a = (g * jax.nn.sigmoid(g) * u).astype(BF16)
        d = jnp.dot(a, wd_ref[sl, :], preferred_element_type=F32)
        if c == 0:
            acc_ref[...] = d
        else:
            acc_ref[...] += d


def _ffn1_kernel(x_ref, g_ref, wg_ref, wu_ref, wd_ref, o_ref, acc_ref):
    x = x_ref[...]
    xn = _rms(x, g_ref[...]).astype(BF16)
    _swiglu_into(acc_ref, xn, wg_ref, wu_ref, wd_ref)
    o_ref[...] = x + 0.5 * acc_ref[...]


def _out_ffn2_kernel(h_ref, a_ref, c_ref, woa_ref, woc_ref, g_ref, wg_ref, wu_ref, wd_ref,
                     gf_ref, o_ref, acc_ref, *, final_norm):
    h = (h_ref[...]
         + jnp.dot(a_ref[...], woa_ref[...], preferred_element_type=F32)
         + jnp.dot(c_ref[...], woc_ref[...], preferred_element_type=F32))
    xn = _rms(h, g_ref[...]).astype(BF16)
    _swiglu_into(acc_ref, xn, wg_ref, wu_ref, wd_ref)
    h = h + 0.5 * acc_ref[...]
    o_ref[...] = _rms(h, gf_ref[...]) if final_norm else h


def _ffn1(x2, g, wg, wu, wd):
    n, d = x2.shape
    tile = pl.BlockSpec((FFN_TOKENS, d), lambda i: (i, 0))
    return pl.pallas_call(
        _ffn1_kernel,
        grid=(n // FFN_TOKENS,),
        in_specs=[tile, _const_spec(g.shape), _const_spec(wg.shape), _const_spec(wu.shape),
                  _const_spec(wd.shape)],
        out_specs=tile,
        out_shape=jax.ShapeDtypeStruct((n, d), F32),
        scratch_shapes=[pltpu.VMEM((FFN_TOKENS, d), F32)],
        compiler_params=pltpu.CompilerParams(dimension_semantics=("parallel",),
                                             vmem_limit_bytes=VMEM_LIMIT),
        name="ffn1",
    )(x2, g, wg, wu, wd)


def _out_ffn2(h2, a2, c2, woa, woc, g, wg, wu, wd, gf, final_norm):
    n, d = h2.shape
    tile = pl.BlockSpec((FFN_TOKENS, d), lambda i: (i, 0))
    mix = pl.BlockSpec((FFN_TOKENS, a2.shape[1]), lambda i: (i, 0))
    return pl.pallas_call(
        functools.partial(_out_ffn2_kernel, final_norm=final_norm),
        grid=(n // FFN_TOKENS,),
        in_specs=[tile, mix, mix, _const_spec(woa.shape), _const_spec(woc.shape),
                  _const_spec(g.shape), _const_spec(wg.shape), _const_spec(wu.shape),
                  _const_spec(wd.shape), _const_spec(gf.shape)],
        out_specs=tile,
        out_shape=jax.ShapeDtypeStruct((n, d), F32),
        scratch_shapes=[pltpu.VMEM((FFN_TOKENS, d), F32)],
        compiler_params=pltpu.CompilerParams(dimension_semantics=("parallel",),
                                             vmem_limit_bytes=VMEM_LIMIT),
        name="out_ffn2",
    )(h2, a2, c2, woa, woc, g, wg, wu, wd, gf)


_C_KV = 0
_C_KI = _C_KV + KV_LATENT
_C_GB = _C_KI + LANES
_C_GC = _C_GB + CONV_WIDTH
_C_X = _C_GC + CONV_WIDTH
_C_END = _C_X + CONV_WIDTH
_R_Q = 0
_R_KV = _R_Q + ATTN_WIDTH
_R_QI = _R_KV + KV_LATENT
_R_WI = _R_QI + N_HEADS_IDX * IDX_DIM
_R_END = _R_WI + 2 * SUBLANES
KVT_ROWS = KV_LATENT + 2 * SUBLANES


def _proj_kernel(h_ref, gm_ref, wcat_ref, wt_ref, wukt_ref, gkv_ref, gkvc_ref, cw_ref, gco_ref,
                 qat_ref, ckv_ref, ckvt_ref, ki_ref, qit_ref, wit_ref, convn_ref, carry_ref):
    tp = h_ref.shape[0]

    @pl.when(pl.program_id(1) == 0)
    def _():
        carry_ref[...] = jnp.zeros_like(carry_ref)

    un = _rms(h_ref[...], gm_ref[...]).astype(BF16)
    p = jnp.dot(un, wcat_ref[...], preferred_element_type=F32)
    ckv_ref[...] = _rms(p[:, _C_KV:_C_KI], gkv_ref[...]).astype(BF16)
    ki_ref[...] = p[:, _C_KI:_C_KI + IDX_DIM].astype(BF16)

    v = p[:, _C_GC:_C_X] * p[:, _C_X:_C_END]
    row = lax.broadcasted_iota(jnp.int32, v.shape, 0)
    prev1 = jnp.broadcast_to(carry_ref[SUBLANES - 1:SUBLANES, :], v.shape)
    prev2 = jnp.broadcast_to(carry_ref[SUBLANES - 2:SUBLANES - 1, :], v.shape)
    v1 = jnp.where(row == 0, prev1, pltpu.roll(v, 1, 0))
    v2 = jnp.where(row == 0, prev2, jnp.where(row == 1, prev1, pltpu.roll(v, 2, 0)))
    carry_ref[...] = v[tp - SUBLANES:, :]
    y = cw_ref[0:1, :] * v2 + cw_ref[1:2, :] * v1 + cw_ref[2:3, :] * v
    conv = p[:, _C_GB:_C_GC] * y
    convn_ref[...] = _rms(conv, gco_ref[...]).astype(BF16)

    pt = lax.dot_general(wt_ref[...], un, (((1,), (1,)), ((), ())),
                         preferred_element_type=F32)
    qt = pt[_R_Q:_R_KV, :].astype(BF16)
    for h in range(N_HEADS):
        qah = jnp.dot(wukt_ref[h], qt[h * HEAD_DIM:(h + 1) * HEAD_DIM, :],
                      preferred_element_type=F32)
        qat_ref[h * KV_LATENT:(h + 1) * KV_LATENT, :] = (qah * (HEAD_DIM ** -0.5 * LOG2E)).astype(BF16)
    kvt = pt[_R_KV:_R_QI, :]
    kvt = kvt * lax.rsqrt(jnp.mean(kvt * kvt, axis=0, keepdims=True) + EPS) * gkvc_ref[...]
    tk = ckvt_ref.shape[2]
    tail = jnp.concatenate([jnp.ones((SUBLANES, tk), F32), jnp.zeros((SUBLANES, tk), F32)], axis=0)
    for kb in range(tp // tk):
        blk = jnp.concatenate([kvt[:, kb * tk:(kb + 1) * tk], tail], axis=0)
        ckvt_ref[kb] = blk.astype(BF16)
    qit_ref[...] = pt[_R_QI:_R_WI, :].astype(BF16)
    wit_ref[...] = pt[_R_WI:_R_WI + N_HEADS_IDX, :] * ((N_HEADS_IDX * IDX_DIM) ** -0.5)


def _proj(h1, gm, wcat, wt, wukt, gkv, cw, gco):
    b, s, d = h1.shape
    tp = PROJ_TOKENS
    nq = N_HEADS_IDX * IDX_DIM
    nqa = N_HEADS * KV_LATENT
    gkvc = gkv.reshape(KV_LATENT, 1)
    tok = lambda w: pl.BlockSpec((None, tp, w), lambda bi, ti: (bi, ti, 0))
    tokt = lambda r: pl.BlockSpec((None, r, tp), lambda bi, ti: (bi, 0, ti))
    out_shape = (
        jax.ShapeDtypeStruct((b, nqa, s), BF16),
        jax.ShapeDtypeStruct((b, s, KV_LATENT), BF16),
        jax.ShapeDtypeStruct((b, s // Q_BLOCK, KVT_ROWS, Q_BLOCK), BF16),
        jax.ShapeDtypeStruct((b, s, IDX_DIM), BF16),
        jax.ShapeDtypeStruct((b, nq, s), BF16),
        jax.ShapeDtypeStruct((b, N_HEADS_IDX, s), F32),
        jax.ShapeDtypeStruct((b, s, CONV_WIDTH), BF16),
    )
    out_specs = (
        tokt(nqa), tok(KV_LATENT),
        pl.BlockSpec((None, tp // Q_BLOCK, KVT_ROWS, Q_BLOCK), lambda bi, ti: (bi, ti, 0, 0)),
        tok(IDX_DIM), tokt(nq), tokt(N_HEADS_IDX), tok(CONV_WIDTH),
    )
    return pl.pallas_call(
        _proj_kernel,
        grid=(b, s // tp),
        in_specs=[tok(d), _const_spec(gm.shape), _const_spec(wcat.shape), _const_spec(wt.shape),
                  _const_spec(wukt.shape), _const_spec(gkv.shape), _const_spec(gkvc.shape),
                  _const_spec(cw.shape), _const_spec(gco.shape)],
        out_specs=out_specs,
        out_shape=out_shape,
        scratch_shapes=[pltpu.VMEM((SUBLANES, CONV_WIDTH), F32)],
        compiler_params=pltpu.CompilerParams(dimension_semantics=("parallel", "arbitrary"),
                                             vmem_limit_bytes=VMEM_LIMIT),
        name="proj",
    )(h1, gm, wcat, wt, wukt, gkv, gkvc, cw, gco)


def _bucket_starts():
    e = N_BUCKETS // 2
    n = N_BUCKETS - e
    assert MAX_DISTANCE % e == 0
    r = MAX_DISTANCE // e
    starts, d = [], e
    for k in range(1, n):
        while d ** n < e ** n * r ** k:
            d += 1
        starts.append(d)
    return starts


def _rel_bucket(dist):
    max_exact = N_BUCKETS // 2
    large = max_exact
    for start in _bucket_starts():
        large = large + (dist >= start).astype(jnp.int32)
    return jnp.where(dist < max_exact, dist, large)


def _bias_kernel(rb_ref, o_ref):
    h = pl.program_id(0)
    tk, tq = o_ref.shape
    s = lax.broadcasted_iota(jnp.int32, (tk, tq), 0)
    t = lax.broadcasted_iota(jnp.int32, (tk, tq), 1)
    bucket = _rel_bucket(jnp.maximum(t - s + tq, 0))
    val = jnp.zeros((tk, tq), F32)
    for b in range(N_BUCKETS):
        val = jnp.where(bucket == b, rb_ref[b, h], val)
    o_ref[...] = (val - rb_ref[N_BUCKETS - 1, h]) * LOG2E


def _bias_tiles(rel_bias, tq):
    return pl.pallas_call(
        _bias_kernel,
        grid=(N_HEADS,),
        in_specs=[pl.BlockSpec(memory_space=pltpu.SMEM)],
        out_specs=pl.BlockSpec((None, 2 * tq, tq), lambda h: (h, 0, 0)),
        out_shape=jax.ShapeDtypeStruct((N_HEADS, 2 * tq, tq), F32),
        compiler_params=pltpu.CompilerParams(dimension_semantics=("parallel",)),
        name="bias_tiles",
    )(rel_bias)


def _dsa_kernel(qit_ref, wit_ref, qat_ref, ki_ref, ckv_ref, ckvt_ref, bias_ref, wuvt_ref, gout_ref,
                o_ref, sc_ref, mask_ref, acc_ref, m_ref, *, n_keep):
    nbat, tq = o_ref.shape[0], o_ref.shape[1]
    tw = nbat * tq
    i = pl.program_id(1)
    nblk = i + 1
    q0 = i * tq
    kf = float(n_keep)
    key_in_blk = lax.broadcasted_iota(jnp.int32, (tq, tq), 0)
    qry_in_blk = lax.broadcasted_iota(jnp.int32, (tq, tq), 1)
    ind = lambda mask: jnp.where(mask, 1.0, 0.0)
    batches = lambda f: jnp.concatenate([f(g) for g in range(nbat)], axis=1)
    part = lambda a: a.reshape(tq // (SWEEP_ACCS * SUBLANES), SWEEP_ACCS, SUBLANES, tw)
    fold = lambda a: part(a).sum(axis=0)
    fold_max = lambda a: part(a).max(axis=0)
    total = lambda a: a.sum(axis=0).sum(axis=0, keepdims=True)
    total_max = lambda a: a.max(axis=0).max(axis=0, keepdims=True)
    zeros8 = jnp.zeros((SWEEP_ACCS, SUBLANES, tw), F32)
    ninf8 = jnp.full((SWEEP_ACCS, SUBLANES, tw), -jnp.inf, F32)

    def score_body(j, carry):
        mn, mx = carry
        k0 = pl.multiple_of(j * tq, tq)
        causal1 = (k0 + key_in_blk) <= (q0 + qry_in_blk)

        def one(g):
            kib = ki_ref[g, pl.ds(k0, tq), :]
            acc = jnp.zeros((tq, tq), F32)
            for h in range(N_HEADS_IDX):
                d = jnp.dot(kib, qit_ref[g, h * IDX_DIM:(h + 1) * IDX_DIM, :],
                            preferred_element_type=F32)
                acc = acc + jnp.maximum(d, 0.0) * wit_ref[g, h:h + 1, :]
            return acc
        acc = batches(one)
        causal = batches(lambda g: causal1)
        masked = jnp.where(causal, acc, -jnp.inf)
        sc_ref[j] = masked
        mx = jnp.maximum(mx, jnp.max(masked, axis=0, keepdims=True))
        mn = jnp.minimum(mn, jnp.min(jnp.where(causal, acc, jnp.inf), axis=0, keepdims=True))
        return mn, mx

    def over_blocks(body, init):
        def pair(pi, c):
            return body(2 * pi + 1, body(2 * pi, c))
        c = lax.fori_loop(0, lax.shift_right_logical(nblk, 1), pair, init)
        return lax.cond((nblk & 1) == 1, lambda c: body(nblk - 1, c), lambda c: c, c)

    mn, mx = over_blocks(score_body,
                         (jnp.full((1, tw), jnp.inf, F32), jnp.full((1, tw), -jnp.inf, F32)))

    def count(bound, strict=False):
        def one(j, g):
            cols = slice(g * tq, (g + 1) * tq)
            blk = sc_ref[j, :, cols]
            hit = blk > bound[:, cols] if strict else blk >= bound[:, cols]
            return ind(hit).reshape(tq // (SWEEP_ACCS * SUBLANES), SWEEP_ACCS, SUBLANES, tq).sum(axis=0)
        a8 = lax.fori_loop(
            0, nblk,
            lambda j, a: a + jnp.concatenate([one(j, g) for g in range(nbat)], axis=-1), zeros8)
        return total(a8)

    def any_set(flag):
        return jnp.max(flag) > 0.0

    c_ge0 = count(jnp.zeros((1, tw), F32))
    c_gt0 = count(jnp.zeros((1, tw), F32), strict=True)
    ncausal = batches(lambda g: (q0 + lax.broadcasted_iota(jnp.int32, (1, tq), 1) + 1).astype(F32))
    small = ncausal <= kf
    neg = c_ge0 < kf
    zero_tie = (c_gt0 < kf) & ~neg
    lo = jnp.where(small | neg, mn, 0.0)
    clo = jnp.where(small | neg, ncausal, c_ge0)
    above_mx = mx + jnp.abs(mx) * 2.0 ** -22 + 1e-37
    hi = jnp.where(neg, 0.0, above_mx)
    searching = ~small & ~zero_tie

    def bis_cond(st):
        return (st[0] < MAX_BISECT) & st[1]

    def bis_passes(_, st):
        lo, hi, clo, active = st
        for _ in range(BISECT_CHUNK):
            mid = lo + (hi - lo) * 0.5
            live = (active > 0.0) & (mid > lo) & (mid < hi)
            c = count(mid)
            up = live & (c >= kf)
            lo = jnp.where(up, mid, lo)
            clo = jnp.where(up, c, clo)
            hi = jnp.where(live & (c < kf), mid, hi)
            active = ind(live & (clo > kf))
        return lo, hi, clo, active

    def bis_body(st):
        lo, hi, clo, active = bis_passes(0, st[2:])
        return st[0] + 1, any_set(active), lo, hi, clo, active

    active = ind(searching & (clo > kf))
    st = lax.cond(any_set(active),
                  lambda st: lax.fori_loop(0, BISECT_FIRST, bis_passes, st), lambda st: st,
                  (lo, hi, clo, active))
    st = lax.while_loop(bis_cond, bis_body, (jnp.int32(0), any_set(st[3])) + st)
    _, _, lo, hi, clo, _ = st

    unres = ind(searching & (clo > kf))

    def peel_cond(st):
        return (st[0] < MAX_PEEL) & st[1]

    def peel_body(st):
        it, _, lo, clo, cand, unres = st

        def body(j, s2):
            a8, m8 = s2
            blk = sc_ref[j]
            ge = blk >= cand
            below = jnp.where((blk >= lo) & ~ge, blk, -jnp.inf)
            return a8 + fold(ind(ge)), jnp.maximum(m8, fold_max(below))
        a8, m8 = lax.fori_loop(0, nblk, body, (zeros8, ninf8))
        c = total(a8)
        on = unres > 0.0
        found = on & (c >= kf)
        down = on & (c < kf)
        lo = jnp.where(found, cand, lo)
        clo = jnp.where(found, c, clo)
        cand = jnp.where(down, total_max(m8), cand)
        return it + 1, any_set(ind(down)), lo, clo, cand, ind(down)

    st = lax.while_loop(peel_cond, peel_body, (jnp.int32(0), any_set(unres), lo, clo, hi, unres))
    _, _, lo, clo, _, _ = st

    tied = any_set(ind(~small & (clo > kf)))

    @pl.when(jnp.logical_not(tied))
    def _():
        def body(j, c):
            mask_ref[j] = jnp.where(sc_ref[j] >= lo, 0.0, -jnp.inf)
            return c
        lax.fori_loop(0, nblk, body, 0)

    @pl.when(tied)
    def _():
        need = kf - count(lo, strict=True)
        earlier = ind(qry_in_blk < key_in_blk).astype(BF16)

        def body(j, seen):
            blk = sc_ref[j]
            eq = ind(blk == lo)
            rank = seen + jnp.dot(earlier, eq.astype(BF16), preferred_element_type=F32)
            keep = (blk > lo) | ((blk == lo) & (rank < need))
            mask_ref[j] = jnp.where(keep, 0.0, -jnp.inf)
            return seen + total(fold(eq))
        over_blocks(body, jnp.zeros((1, tw), F32))

    m_ref[...] = jnp.full(m_ref.shape, -jnp.inf, F32)
    acc_ref[...] = jnp.zeros(acc_ref.shape, F32)

    def attend(j, nkb, bias):
        tk = nkb * tq
        k0 = pl.multiple_of(j * tq, tq)
        heads = lambda f: jnp.concatenate([f(h) for h in range(N_HEADS)], axis=1)
        for g in range(nbat):
            cols = slice(g * tq, (g + 1) * tq)
            kv = ckv_ref[g, pl.ds(k0, tk), :]
            kvt1 = jnp.concatenate([ckvt_ref[g, j + b] for b in range(nkb)], axis=1)
            mask = jnp.concatenate([mask_ref[j + b, :, cols] for b in range(nkb)], axis=0)
            qa = heads(lambda h: qat_ref[g, h * KV_LATENT:(h + 1) * KV_LATENT, :])
            s = jnp.dot(kv, qa, preferred_element_type=F32)
            if bias is not None:
                s = s + heads(bias)
            s = s + heads(lambda h: mask)
            m_prev = m_ref[g]
            m_new = jnp.maximum(m_prev, jnp.max(s, axis=0, keepdims=True))
            m_safe = jnp.where(m_new == -jnp.inf, 0.0, m_new)
            alpha = jnp.exp2(m_prev - m_safe)
            p = jnp.exp2(s - m_safe).astype(BF16)
            m_ref[g] = m_new
            pv = jnp.dot(kvt1, p, preferred_element_type=F32)
            acc_ref[g] = alpha * acc_ref[g] + pv

    nfar = jnp.maximum(i - 1, 0)

    def far_pair(pi, c):
        attend(2 * pi, 2, None)
        return c
    lax.fori_loop(0, lax.shift_right_logical(nfar, 1), far_pair, 0)

    @pl.when((nfar & 1) == 1)
    def _():
        attend(nfar - 1, 1, None)

    @pl.when(i >= 1)
    def _():
        attend(i - 1, 2, lambda h: bias_ref[h])

    @pl.when(i == 0)
    def _():
        attend(0, 1, lambda h: bias_ref[h, tq:, :])

    for g in range(nbat):
        outs = []
        for h in range(N_HEADS):
            a = acc_ref[g, :, h * tq:(h + 1) * tq]
            ctx = (a[:KV_LATENT, :] / a[KV_LATENT:KV_LATENT + 1, :]).astype(BF16)
            outs.append(jnp.dot(wuvt_ref[h], ctx, preferred_element_type=F32))
        out = jnp.concatenate(outs, axis=0).T
        o_ref[g] = _rms(out, gout_ref[...]).astype(BF16)


def _dsa(qit, wit, qat, ki, ckv, ckvt, bias, wuvt, gout, n_keep):
    b, s, _ = ckv.shape
    tq = Q_BLOCK
    nb = s // tq
    nq = N_HEADS_IDX * IDX_DIM
    nbat = DSA_BATCH if b % DSA_BATCH == 0 else 1
    colblk = lambda r: pl.BlockSpec((nbat, r, tq), lambda bi, i: (bi, 0, i))
    return pl.pallas_call(
        functools.partial(_dsa_kernel, n_keep=n_keep),
        grid=(b // nbat, nb),
        in_specs=[
            colblk(nq), colblk(N_HEADS_IDX), colblk(N_HEADS * KV_LATENT),
            pl.BlockSpec((nbat, s, IDX_DIM), lambda bi, i: (bi, 0, 0)),
            pl.BlockSpec((nbat, s, KV_LATENT), lambda bi, i: (bi, 0, 0)),
            pl.BlockSpec((nbat, nb, KVT_ROWS, tq), lambda bi, i: (bi, 0, 0, 0)),
            _const_spec(bias.shape), _const_spec(wuvt.shape), _const_spec(gout.shape),
        ],
        out_specs=pl.BlockSpec((nbat, tq, ATTN_WIDTH), lambda bi, i: (bi, i, 0)),
        out_shape=jax.ShapeDtypeStruct((b, s, ATTN_WIDTH), BF16),
        scratch_shapes=[
            pltpu.VMEM((nb, tq, nbat * tq), F32),
            pltpu.VMEM((nb, tq, nbat * tq), F32),
            pltpu.VMEM((nbat, KVT_ROWS, N_HEADS * tq), F32),
            pltpu.VMEM((nbat, 1, N_HEADS * tq), F32),
        ],
        compiler_params=pltpu.CompilerParams(dimension_semantics=("parallel", "parallel"),
                                             vmem_limit_bytes=VMEM_LIMIT),
        name="dsa",
    )(qit, wit, qat, ki, ckv, ckvt, bias, wuvt, gout)


def _cast_kernel(*refs):
    n = len(refs) // 2
    for src, dst in zip(refs[:n], refs[n:]):
        dst[...] = src[...].astype(BF16)


def _to_bf16(*ws):
    r, c = ws[0].shape
    spec = pl.BlockSpec((r // CAST_STEPS, c), lambda i: (i, 0))
    return pl.pallas_call(
        _cast_kernel,
        grid=(CAST_STEPS,),
        in_specs=[spec] * len(ws),
        out_specs=[spec] * len(ws),
        out_shape=[jax.ShapeDtypeStruct(w.shape, BF16) for w in ws],
        compiler_params=pltpu.CompilerParams(dimension_semantics=("parallel",),
                                             vmem_limit_bytes=VMEM_LIMIT),
        name="to_bf16",
    )(*ws)


def kernel(x, ffn1_norm, ffn1_w_gate, ffn1_w_up, ffn1_w_down, mix_norm, w_in, kv_norm, w_uk, w_uv, rel_bias, conv_w, attn_out_norm, conv_out_norm, w_out, ffn2_norm, ffn2_w_gate, ffn2_w_up, ffn2_w_down, final_norm):
    b, s, d = x.shape
    depth = ffn1_norm.shape[0]
    n_keep = min(TOPK_MAX, s // 4)
    assert s % Q_BLOCK == 0 and s % PROJ_TOKENS == 0 and (b * s) % FFN_TOKENS == 0
    assert Q_BLOCK >= MAX_DISTANCE and n_keep <= Q_BLOCK
    row = lambda v: v.reshape(1, -1).astype(F32)
    offs = [0]
    for w in IN_SIZES:
        offs.append(offs[-1] + w)

    bias = _bias_tiles(rel_bias.astype(F32), Q_BLOCK)
    h = x.astype(F32).reshape(b * s, d)
    for l in range(depth):
        wg1, wu1, wg2, wu2 = _to_bf16(ffn1_w_gate[l], ffn1_w_up[l], ffn2_w_gate[l], ffn2_w_up[l])
        wd1, wd2 = _to_bf16(ffn1_w_down[l], ffn2_w_down[l])
        h = _ffn1(h, row(ffn1_norm[l]), wg1, wu1, wd1)

        wi = w_in[l]
        col = lambda k: wi[:, offs[k]:offs[k + 1]]
        wcat = jnp.concatenate(
            [col(1), col(3), jnp.zeros((d, LANES - IDX_DIM), wi.dtype), col(5), col(6), col(7)],
            axis=1).astype(BF16)
        wt = jnp.concatenate(
            [col(0).T, col(1).T, col(2).T, col(4).T,
             jnp.zeros((_R_END - _R_WI - N_HEADS_IDX, d), wi.dtype)], axis=0).astype(BF16)
        wukt = jnp.transpose(w_uk[l], (1, 0, 2)).astype(BF16)
        wuvt = jnp.transpose(w_uv[l], (1, 2, 0)).astype(BF16)

        qat, ckv, ckvt, ki, qit, wit, convn = _proj(
            h.reshape(b, s, d), row(mix_norm[l]), wcat, wt, wukt, row(kv_norm[l]),
            conv_w[l].astype(F32), row(conv_out_norm[l]))
        attn = _dsa(qit, wit, qat, ki, ckv, ckvt, bias, wuvt, row(attn_out_norm[l]), n_keep)

        last = l == depth - 1
        h = _out_ffn2(h, attn.reshape(b * s, ATTN_WIDTH), convn.reshape(b * s, CONV_WIDTH),
                      w_out[l][:ATTN_WIDTH].astype(BF16), w_out[l][ATTN_WIDTH:].astype(BF16),
                      row(ffn2_norm[l]), wg2, wu2, wd2, row(final_norm), final_norm=last)
    return h.reshape(b, s, d).astype(x.dtype)
```

```python
import functools
import math

import jax
import jax.numpy as jnp
from jax import lax
from jax.experimental import pallas as pl
from jax.experimental.pallas import tpu as pltpu

F32 = jnp.float32
BF16 = jnp.bfloat16

EPS = 1e-6
N_HEADS = 8
HEAD_DIM = 64
ATTN_WIDTH = N_HEADS * HEAD_DIM
KV_LATENT = 128
N_HEADS_IDX = 8
IDX_DIM = 64
TOPK_MAX = 256
CONV_WIDTH = 512
CONV_K = 3
N_BUCKETS = 32
MAX_DISTANCE = 128
IN_SIZES = (ATTN_WIDTH, KV_LATENT, N_HEADS_IDX * IDX_DIM, IDX_DIM, N_HEADS_IDX,
            CONV_WIDTH, CONV_WIDTH, CONV_WIDTH)

SUBLANES = 8
LANES = 128

FFN_TOKENS = 1024
FFN_CHUNK = 256
PROJ_TOKENS = 1024
Q_BLOCK = 256
BISECT_CHUNK = 2
BISECT_FIRST = 8
MAX_BISECT = 128
CAST_STEPS = 8
DSA_BATCH = 4
SWEEP_ACCS = 2
MAX_PEEL = 4096
LOG2E = math.log2(math.e)
VMEM_LIMIT = 56 * 1024 * 1024


def _rms(x, g):
    return x * lax.rsqrt(jnp.mean(x * x, axis=-1, keepdims=True) + EPS) * g


def _const_spec(shape):
    nd = len(shape)
    return pl.BlockSpec(shape, lambda *_: (0,) * nd, pipeline_mode=pl.Buffered(1))


def _swiglu_into(acc_ref, xn, wg_ref, wu_ref, wd_ref):
    d_ff = wg_ref.shape[1]
    for c in range(d_ff // FFN_CHUNK):
        sl = slice(c * FFN_CHUNK, (c + 1) * FFN_CHUNK)
        g = jnp.dot(xn, wg_ref[:, sl], preferred_element_type=F32)
        u = jnp.dot(xn, wu_ref[:, sl], preferred_element_type=F32)
        a = (g * jax.nn.sigmoid(g) * u).astype(BF16)
        d = jnp.dot(a, wd_ref[sl, :], preferred_element_type=F32)
        if c == 0:
            acc_ref[...] = d
        else:
            acc_ref[...] += d


def _ffn1_kernel(x_ref, g_ref, wg_ref, wu_ref, wd_ref, o_ref, acc_ref):
    x = x_ref[...]
    xn = _rms(x, g_ref[...]).astype(BF16)
    _swiglu_into(acc_ref, xn, wg_ref, wu_ref, wd_ref)
    o_ref[...] = x + 0.5 * acc_ref[...]


def _out_ffn2_kernel(h_ref, a_ref, c_ref, woa_ref, woc_ref, g_ref, wg_ref, wu_ref, wd_ref,
                     gf_ref, o_ref, acc_ref, *, final_norm):
    h = (h_ref[...]
         + lax.dot_general(a_ref[...], woa_ref[...], (((0,), (0,)), ((), ())),
                           preferred_element_type=F32)
         + jnp.dot(c_ref[...], woc_ref[...], preferred_element_type=F32))
    xn = _rms(h, g_ref[...]).astype(BF16)
    _swiglu_into(acc_ref, xn, wg_ref, wu_ref, wd_ref)
    h = h + 0.5 * acc_ref[...]
    o_ref[...] = _rms(h, gf_ref[...]) if final_norm else h


def _ffn1(x2, g, wg, wu, wd):
    n, d = x2.shape
    tile = pl.BlockSpec((FFN_TOKENS, d), lambda i: (i, 0))
    return pl.pallas_call(
        _ffn1_kernel,
        grid=(n // FFN_TOKENS,),
        in_specs=[tile, _const_spec(g.shape), _const_spec(wg.shape), _const_spec(wu.shape),
                  _const_spec(wd.shape)],
        out_specs=tile,
        out_shape=jax.ShapeDtypeStruct((n, d), F32),
        scratch_shapes=[pltpu.VMEM((FFN_TOKENS, d), F32)],
        compiler_params=pltpu.CompilerParams(dimension_semantics=("arbitrary",),
                                             vmem_limit_bytes=VMEM_LIMIT),
        name="ffn1",
    )(x2, g, wg, wu, wd)


def _out_ffn2(h2, a2, c2, woa, woc, g, wg, wu, wd, gf, final_norm):
    n, d = h2.shape
    tile = pl.BlockSpec((FFN_TOKENS, d), lambda i: (i, 0))
    mix = pl.BlockSpec((FFN_TOKENS, c2.shape[1]), lambda i: (i, 0))
    per_row = a2.shape[2] // FFN_TOKENS
    mixt = pl.BlockSpec((None, a2.shape[1], FFN_TOKENS), lambda i: (i // per_row, 0, i % per_row))
    return pl.pallas_call(
        functools.partial(_out_ffn2_kernel, final_norm=final_norm),
        grid=(n // FFN_TOKENS,),
        in_specs=[tile, mixt, mix, _const_spec(woa.shape), _const_spec(woc.shape),
                  _const_spec(g.shape), _const_spec(wg.shape), _const_spec(wu.shape),
                  _const_spec(wd.shape), _const_spec(gf.shape)],
        out_specs=tile,
        out_shape=jax.ShapeDtypeStruct((n, d), F32),
        scratch_shapes=[pltpu.VMEM((FFN_TOKENS, d), F32)],
        compiler_params=pltpu.CompilerParams(dimension_semantics=("arbitrary",),
                                             vmem_limit_bytes=VMEM_LIMIT),
        name="out_ffn2",
    )(h2, a2, c2, woa, woc, g, wg, wu, wd, gf)


_C_KV = 0
_C_KI = _C_KV + KV_LATENT
_C_GB = _C_KI + LANES
_C_GC = _C_GB + CONV_WIDTH
_C_X = _C_GC + CONV_WIDTH
_C_END = _C_X + CONV_WIDTH
_R_Q = 0
_R_KV = _R_Q + ATTN_WIDTH
_R_QI = _R_KV + KV_LATENT
_R_WI = _R_QI + N_HEADS_IDX * IDX_DIM
_R_END = _R_WI + 2 * SUBLANES
KVT_ROWS = KV_LATENT + 2 * SUBLANES


def _proj_kernel(h_ref, gm_ref, wcat_ref, wt_ref, wukt_ref, gkv_ref, gkvc_ref, cw_ref, gco_ref,
                 qat_ref, ckv_ref, ckvt_ref, ki_ref, qit_ref, wit_ref, convn_ref, carry_ref):
    tp = h_ref.shape[0]

    @pl.when(pl.program_id(1) == 0)
    def _():
        carry_ref[...] = jnp.zeros_like(carry_ref)

    un = _rms(h_ref[...], gm_ref[...]).astype(BF16)
    p = jnp.dot(un, wcat_ref[...], preferred_element_type=F32)
    ckv_ref[...] = _rms(p[:, _C_KV:_C_KI], gkv_ref[...]).astype(BF16)
    ki_ref[...] = p[:, _C_KI:_C_KI + IDX_DIM].astype(BF16)

    v = p[:, _C_GC:_C_X] * p[:, _C_X:_C_END]
    row = lax.broadcasted_iota(jnp.int32, v.shape, 0)
    prev1 = jnp.broadcast_to(carry_ref[SUBLANES - 1:SUBLANES, :], v.shape)
    prev2 = jnp.broadcast_to(carry_ref[SUBLANES - 2:SUBLANES - 1, :], v.shape)
    v1 = jnp.where(row == 0, prev1, pltpu.roll(v, 1, 0))
    v2 = jnp.where(row == 0, prev2, jnp.where(row == 1, prev1, pltpu.roll(v, 2, 0)))
    carry_ref[...] = v[tp - SUBLANES:, :]
    y = cw_ref[0:1, :] * v2 + cw_ref[1:2, :] * v1 + cw_ref[2:3, :] * v
    conv = p[:, _C_GB:_C_GC] * y
    convn_ref[...] = _rms(conv, gco_ref[...]).astype(BF16)

    pt = lax.dot_general(wt_ref[...], un, (((1,), (1,)), ((), ())),
                         preferred_element_type=F32)
    qt = pt[_R_Q:_R_KV, :].astype(BF16)
    for h in range(N_HEADS):
        qah = jnp.dot(wukt_ref[h], qt[h * HEAD_DIM:(h + 1) * HEAD_DIM, :],
                      preferred_element_type=F32)
        qat_ref[h * KV_LATENT:(h + 1) * KV_LATENT, :] = (qah * (HEAD_DIM ** -0.5 * LOG2E)).astype(BF16)
    kvt = pt[_R_KV:_R_QI, :]
    kvt = kvt * lax.rsqrt(jnp.mean(kvt * kvt, axis=0, keepdims=True) + EPS) * gkvc_ref[...]
    tk = ckvt_ref.shape[2]
    tail = jnp.concatenate([jnp.ones((SUBLANES, tk), F32), jnp.zeros((SUBLANES, tk), F32)], axis=0)
    for kb in range(tp // tk):
        blk = jnp.concatenate([kvt[:, kb * tk:(kb + 1) * tk], tail], axis=0)
        ckvt_ref[kb] = blk.astype(BF16)
    qit_ref[...] = pt[_R_QI:_R_WI, :].astype(BF16)
    wit_ref[...] = pt[_R_WI:_R_WI + N_HEADS_IDX, :] * ((N_HEADS_IDX * IDX_DIM) ** -0.5)


def _proj(h1, gm, wcat, wt, wukt, gkv, cw, gco):
    b, s, d = h1.shape
    tp = PROJ_TOKENS
    nq = N_HEADS_IDX * IDX_DIM
    nqa = N_HEADS * KV_LATENT
    gkvc = gkv.reshape(KV_LATENT, 1)
    tok = lambda w: pl.BlockSpec((None, tp, w), lambda bi, ti: (bi, ti, 0))
    tokt = lambda r: pl.BlockSpec((None, r, tp), lambda bi, ti: (bi, 0, ti))
    out_shape = (
        jax.ShapeDtypeStruct((b, nqa, s), BF16),
        jax.ShapeDtypeStruct((b, s, KV_LATENT), BF16),
        jax.ShapeDtypeStruct((b, s // Q_BLOCK, KVT_ROWS, Q_BLOCK), BF16),
        jax.ShapeDtypeStruct((b, s, IDX_DIM), BF16),
        jax.ShapeDtypeStruct((b, nq, s), BF16),
        jax.ShapeDtypeStruct((b, N_HEADS_IDX, s), F32),
        jax.ShapeDtypeStruct((b, s, CONV_WIDTH), BF16),
    )
    out_specs = (
        tokt(nqa), tok(KV_LATENT),
        pl.BlockSpec((None, tp // Q_BLOCK, KVT_ROWS, Q_BLOCK), lambda bi, ti: (bi, ti, 0, 0)),
        tok(IDX_DIM), tokt(nq), tokt(N_HEADS_IDX), tok(CONV_WIDTH),
    )
    return pl.pallas_call(
        _proj_kernel,
        grid=(b, s // tp),
        in_specs=[tok(d), _const_spec(gm.shape), _const_spec(wcat.shape), _const_spec(wt.shape),
                  _const_spec(wukt.shape), _const_spec(gkv.shape), _const_spec(gkvc.shape),
                  _const_spec(cw.shape), _const_spec(gco.shape)],
        out_specs=out_specs,
        out_shape=out_shape,
        scratch_shapes=[pltpu.VMEM((SUBLANES, CONV_WIDTH), F32)],
        compiler_params=pltpu.CompilerParams(dimension_semantics=("arbitrary", "arbitrary"),
                                             vmem_limit_bytes=VMEM_LIMIT),
        name="proj",
    )(h1, gm, wcat, wt, wukt, gkv, gkvc, cw, gco)


def _bucket_starts():
    e = N_BUCKETS // 2
    n = N_BUCKETS - e
    assert MAX_DISTANCE % e == 0
    r = MAX_DISTANCE // e
    starts, d = [], e
    for k in range(1, n):
        while d ** n < e ** n * r ** k:
            d += 1
        starts.append(d)
    return starts


def _rel_bucket(dist):
    max_exact = N_BUCKETS // 2
    large = max_exact
    for start in _bucket_starts():
        large = large + (dist >= start).astype(jnp.int32)
    return jnp.where(dist < max_exact, dist, large)


def _bias_kernel(rb_ref, o_ref):
    h = pl.program_id(0)
    tk, tq = o_ref.shape
    s = lax.broadcasted_iota(jnp.int32, (tk, tq), 0)
    t = lax.broadcasted_iota(jnp.int32, (tk, tq), 1)
    bucket = _rel_bucket(jnp.maximum(t - s + tq, 0))
    val = jnp.zeros((tk, tq), F32)
    for b in range(N_BUCKETS):
        val = jnp.where(bucket == b, rb_ref[b, h], val)
    o_ref[...] = (val - rb_ref[N_BUCKETS - 1, h]) * LOG2E


def _bias_tiles(rel_bias, tq):
    return pl.pallas_call(
        _bias_kernel,
        grid=(N_HEADS,),
        in_specs=[pl.BlockSpec(memory_space=pltpu.SMEM)],
        out_specs=pl.BlockSpec((None, 2 * tq, tq), lambda h: (h, 0, 0)),
        out_shape=jax.ShapeDtypeStruct((N_HEADS, 2 * tq, tq), F32),
        compiler_params=pltpu.CompilerParams(dimension_semantics=("arbitrary",)),
        name="bias_tiles",
    )(rel_bias)


def _dsa_kernel(qit_ref, wit_ref, qat_ref, ki_ref, ckv_ref, ckvt_ref, bias_ref, wuvt_ref, gout_ref,
                o_ref, sc_ref, mask_ref, acc_ref, m_ref, *, n_keep):
    nbat, tq = o_ref.shape[0], o_ref.shape[2]
    tw = nbat * tq
    i = pl.program_id(1)
    nblk = i + 1
    q0 = i * tq
    kf = float(n_keep)
    key_in_blk = lax.broadcasted_iota(jnp.int32, (tq, tq), 0)
    qry_in_blk = lax.broadcasted_iota(jnp.int32, (tq, tq), 1)
    ind = lambda mask: jnp.where(mask, 1.0, 0.0)
    batches = lambda f: jnp.concatenate([f(g) for g in range(nbat)], axis=1)
    part = lambda a: a.reshape(tq // (SWEEP_ACCS * SUBLANES), SWEEP_ACCS, SUBLANES, tw)
    fold = lambda a: part(a).sum(axis=0)
    fold_max = lambda a: part(a).max(axis=0)
    total = lambda a: a.sum(axis=0).sum(axis=0, keepdims=True)
    total_max = lambda a: a.max(axis=0).max(axis=0, keepdims=True)
    zeros8 = jnp.zeros((SWEEP_ACCS, SUBLANES, tw), F32)
    ninf8 = jnp.full((SWEEP_ACCS, SUBLANES, tw), -jnp.inf, F32)

    def score_body(j, carry):
        mn, mx = carry
        k0 = pl.multiple_of(j * tq, tq)
        causal1 = (k0 + key_in_blk) <= (q0 + qry_in_blk)

        def one(g):
            kib = ki_ref[g, pl.ds(k0, tq), :]
            acc = jnp.zeros((tq, tq), F32)
            for h in range(N_HEADS_IDX):
                d = jnp.dot(kib, qit_ref[g, h * IDX_DIM:(h + 1) * IDX_DIM, :],
                            preferred_element_type=F32)
                acc = acc + jnp.maximum(d, 0.0) * wit_ref[g, h:h + 1, :]
            return acc
        acc = batches(one)
        causal = batches(lambda g: causal1)
        masked = jnp.where(causal, acc, -jnp.inf)
        sc_ref[j] = masked
        mx = jnp.maximum(mx, jnp.max(masked, axis=0, keepdims=True))
        mn = jnp.minimum(mn, jnp.min(jnp.where(causal, acc, jnp.inf), axis=0, keepdims=True))
        return mn, mx

    def over_blocks(body, init):
        def pair(pi, c):
            return body(2 * pi + 1, body(2 * pi, c))
        c = lax.fori_loop(0, lax.shift_right_logical(nblk, 1), pair, init)
        return lax.cond((nblk & 1) == 1, lambda c: body(nblk - 1, c), lambda c: c, c)

    mn, mx = over_blocks(score_body,
                         (jnp.full((1, tw), jnp.inf, F32), jnp.full((1, tw), -jnp.inf, F32)))

    def count(bound, strict=False):
        def one(j, g):
            cols = slice(g * tq, (g + 1) * tq)
            blk = sc_ref[j, :, cols]
            hit = blk > bound[:, cols] if strict else blk >= bound[:, cols]
            return ind(hit).reshape(tq // (SWEEP_ACCS * SUBLANES), SWEEP_ACCS, SUBLANES, tq).sum(axis=0)
        a8 = lax.fori_loop(
            0, nblk,
            lambda j, a: a + jnp.concatenate([one(j, g) for g in range(nbat)], axis=-1), zeros8)
        return total(a8)

    def any_set(flag):
        return jnp.max(flag) > 0.0

    c_ge0 = count(jnp.zeros((1, tw), F32))
    c_gt0 = count(jnp.zeros((1, tw), F32), strict=True)
    ncausal = batches(lambda g: (q0 + lax.broadcasted_iota(jnp.int32, (1, tq), 1) + 1).astype(F32))
    small = ncausal <= kf
    neg = c_ge0 < kf
    zero_tie = (c_gt0 < kf) & ~neg
    lo = jnp.where(small | neg, mn, 0.0)
    clo = jnp.where(small | neg, ncausal, c_ge0)
    above_mx = mx + jnp.abs(mx) * 2.0 ** -22 + 1e-37
    hi = jnp.where(neg, 0.0, above_mx)
    searching = ~small & ~zero_tie

    def bis_cond(st):
        return (st[0] < MAX_BISECT) & st[1]

    def bis_passes(_, st):
        lo, hi, clo, active = st
        for _ in range(BISECT_CHUNK):
            mid = lo + (hi - lo) * 0.5
            live = (active > 0.0) & (mid > lo) & (mid < hi)
            c = count(mid)
            up = live & (c >= kf)
            lo = jnp.where(up, mid, lo)
            clo = jnp.where(up, c, clo)
            hi = jnp.where(live & (c < kf), mid, hi)
            active = ind(live & (clo > kf))
        return lo, hi, clo, active

    def bis_body(st):
        lo, hi, clo, active = bis_passes(0, st[2:])
        return st[0] + 1, any_set(active), lo, hi, clo, active

    active = ind(searching & (clo > kf))
    st = lax.cond(any_set(active),
                  lambda st: lax.fori_loop(0, BISECT_FIRST, bis_passes, st), lambda st: st,
                  (lo, hi, clo, active))
    st = lax.while_loop(bis_cond, bis_body, (jnp.int32(0), any_set(st[3])) + st)
    _, _, lo, hi, clo, _ = st

    unres = ind(searching & (clo > kf))

    def peel_cond(st):
        return (st[0] < MAX_PEEL) & st[1]

    def peel_body(st):
        it, _, lo, clo, cand, unres = st

        def body(j, s2):
            a8, m8 = s2
            blk = sc_ref[j]
            ge = blk >= cand
            below = jnp.where((blk >= lo) & ~ge, blk, -jnp.inf)
            return a8 + fold(ind(ge)), jnp.maximum(m8, fold_max(below))
        a8, m8 = lax.fori_loop(0, nblk, body, (zeros8, ninf8))
        c = total(a8)
        on = unres > 0.0
        found = on & (c >= kf)
        down = on & (c < kf)
        lo = jnp.where(found, cand, lo)
        clo = jnp.where(found, c, clo)
        cand = jnp.where(down, total_max(m8), cand)
        return it + 1, any_set(ind(down)), lo, clo, cand, ind(down)

    st = lax.while_loop(peel_cond, peel_body, (jnp.int32(0), any_set(unres), lo, clo, hi, unres))
    _, _, lo, clo, _, _ = st

    tied = any_set(ind(~small & (clo > kf)))

    @pl.when(jnp.logical_not(tied))
    def _():
        def body(j, c):
            mask_ref[j] = jnp.where(sc_ref[j] >= lo, 0.0, -jnp.inf)
            return c
        lax.fori_loop(0, nblk, body, 0)

    @pl.when(tied)
    def _():
        need = kf - count(lo, strict=True)
        earlier = ind(qry_in_blk < key_in_blk).astype(BF16)

        def body(j, seen):
            blk = sc_ref[j]
            eq = ind(blk == lo)
            rank = seen + jnp.dot(earlier, eq.astype(BF16), preferred_element_type=F32)
            keep = (blk > lo) | ((blk == lo) & (rank < need))
            mask_ref[j] = jnp.where(keep, 0.0, -jnp.inf)
            return seen + total(fold(eq))
        over_blocks(body, jnp.zeros((1, tw), F32))

    m_ref[...] = jnp.full(m_ref.shape, -jnp.inf, F32)
    acc_ref[...] = jnp.zeros(acc_ref.shape, F32)

    def attend(j, nkb, bias):
        tk = nkb * tq
        k0 = pl.multiple_of(j * tq, tq)
        heads = lambda f: jnp.concatenate([f(h) for h in range(N_HEADS)], axis=1)
        for g in range(nbat):
            cols = slice(g * tq, (g + 1) * tq)
            kv = ckv_ref[g, pl.ds(k0, tk), :]
            kvt1 = jnp.concatenate([ckvt_ref[g, j + b] for b in range(nkb)], axis=1)
            mask = jnp.concatenate([mask_ref[j + b, :, cols] for b in range(nkb)], axis=0)
            qa = heads(lambda h: qat_ref[g, h * KV_LATENT:(h + 1) * KV_LATENT, :])
            s = jnp.dot(kv, qa, preferred_element_type=F32)
            if bias is not None:
                s = s + heads(bias)
            s = s + heads(lambda h: mask)
            m_prev = m_ref[g]
            m_new = jnp.maximum(m_prev, jnp.max(s, axis=0, keepdims=True))
            m_safe = jnp.where(m_new == -jnp.inf, 0.0, m_new)
            alpha = jnp.exp2(m_prev - m_safe)
            p = jnp.exp2(s - m_safe).astype(BF16)
            m_ref[g] = m_new
            pv = jnp.dot(kvt1, p, preferred_element_type=F32)
            acc_ref[g] = alpha * acc_ref[g] + pv

    nfar = jnp.maximum(i - 1, 0)

    def far_pair(pi, c):
        attend(2 * pi, 2, None)
        return c
    lax.fori_loop(0, lax.shift_right_logical(nfar, 1), far_pair, 0)

    @pl.when((nfar & 1) == 1)
    def _():
        attend(nfar - 1, 1, None)

    @pl.when(i >= 1)
    def _():
        attend(i - 1, 2, lambda h: bias_ref[h])

    @pl.when(i == 0)
    def _():
        attend(0, 1, lambda h: bias_ref[h, tq:, :])

    for g in range(nbat):
        outs = []
        for h in range(N_HEADS):
            a = acc_ref[g, :, h * tq:(h + 1) * tq]
            ctx = (a[:KV_LATENT, :] / a[KV_LATENT:KV_LATENT + 1, :]).astype(BF16)
            outs.append(jnp.dot(wuvt_ref[h], ctx, preferred_element_type=F32))
        out = jnp.concatenate(outs, axis=0)
        ms = jnp.mean(out * out, axis=0, keepdims=True)
        o_ref[g] = (out * lax.rsqrt(ms + EPS) * gout_ref[...]).astype(BF16)


def _dsa(qit, wit, qat, ki, ckv, ckvt, bias, wuvt, gout, n_keep):
    b, s, _ = ckv.shape
    tq = Q_BLOCK
    nb = s // tq
    nq = N_HEADS_IDX * IDX_DIM
    nbat = DSA_BATCH if b % DSA_BATCH == 0 else 1
    colblk = lambda r: pl.BlockSpec((nbat, r, tq), lambda bi, i: (bi, 0, i))
    return pl.pallas_call(
        functools.partial(_dsa_kernel, n_keep=n_keep),
        grid=(b // nbat, nb),
        in_specs=[
            colblk(nq), colblk(N_HEADS_IDX), colblk(N_HEADS * KV_LATENT),
            pl.BlockSpec((nbat, s, IDX_DIM), lambda bi, i: (bi, 0, 0)),
            pl.BlockSpec((nbat, s, KV_LATENT), lambda bi, i: (bi, 0, 0)),
            pl.BlockSpec((nbat, nb, KVT_ROWS, tq), lambda bi, i: (bi, 0, 0, 0)),
            _const_spec(bias.shape), _const_spec(wuvt.shape), _const_spec(gout.shape),
        ],
        out_specs=pl.BlockSpec((nbat, ATTN_WIDTH, tq), lambda bi, i: (bi, 0, i)),
        out_shape=jax.ShapeDtypeStruct((b, ATTN_WIDTH, s), BF16),
        scratch_shapes=[
            pltpu.VMEM((nb, tq, nbat * tq), F32),
            pltpu.VMEM((nb, tq, nbat * tq), F32),
            pltpu.VMEM((nbat, KVT_ROWS, N_HEADS * tq), F32),
            pltpu.VMEM((nbat, 1, N_HEADS * tq), F32),
        ],
        compiler_params=pltpu.CompilerParams(dimension_semantics=("arbitrary", "arbitrary"),
                                             vmem_limit_bytes=VMEM_LIMIT),
        name="dsa",
    )(qit, wit, qat, ki, ckv, ckvt, bias, wuvt, gout)


def _cast_kernel(*refs):
    n = len(refs) // 2
    for src, dst in zip(refs[:n], refs[n:]):
        dst[...] = src[...].astype(BF16)


def _to_bf16(*ws):
    r, c = ws[0].shape
    spec = pl.BlockSpec((r // CAST_STEPS, c), lambda i: (i, 0))
    return pl.pallas_call(
        _cast_kernel,
        grid=(CAST_STEPS,),
        in_specs=[spec] * len(ws),
        out_specs=[spec] * len(ws),
        out_shape=[jax.ShapeDtypeStruct(w.shape, BF16) for w in ws],
        compiler_params=pltpu.CompilerParams(dimension_semantics=("arbitrary",),
                                             vmem_limit_bytes=VMEM_LIMIT),
        name="to_bf16",
    )(*ws)


def kernel(x, ffn1_norm, ffn1_w_gate, ffn1_w_up, ffn1_w_down, mix_norm, w_in, kv_norm, w_uk, w_uv, rel_bias, conv_w, attn_out_norm, conv_out_norm, w_out, ffn2_norm, ffn2_w_gate, ffn2_w_up, ffn2_w_down, final_norm):
    b, s, d = x.shape
    depth = ffn1_norm.shape[0]
    n_keep = min(TOPK_MAX, s // 4)
    assert s % Q_BLOCK == 0 and s % PROJ_TOKENS == 0 and s % FFN_TOKENS == 0
    assert Q_BLOCK >= MAX_DISTANCE and n_keep <= Q_BLOCK
    row = lambda v: v.reshape(1, -1).astype(F32)
    offs = [0]
    for w in IN_SIZES:
        offs.append(offs[-1] + w)

    bias = _bias_tiles(rel_bias.astype(F32), Q_BLOCK)
    h = x.astype(F32).reshape(b * s, d)
    for l in range(depth):
        wg1, wu1, wg2, wu2 = _to_bf16(ffn1_w_gate[l], ffn1_w_up[l], ffn2_w_gate[l], ffn2_w_up[l])
        wd1, wd2 = _to_bf16(ffn1_w_down[l], ffn2_w_down[l])
        h = _ffn1(h, row(ffn1_norm[l]), wg1, wu1, wd1)

        wi = w_in[l]
        col = lambda k: wi[:, offs[k]:offs[k + 1]]
        wcat = jnp.concatenate(
            [col(1), col(3), jnp.zeros((d, LANES - IDX_DIM), wi.dtype), col(5), col(6), col(7)],
            axis=1).astype(BF16)
        wt = jnp.concatenate(
            [col(0).T, col(1).T, col(2).T, col(4).T,
             jnp.zeros((_R_END - _R_WI - N_HEADS_IDX, d), wi.dtype)], axis=0).astype(BF16)
        wukt = jnp.transpose(w_uk[l], (1, 0, 2)).astype(BF16)
        wuvt = jnp.transpose(w_uv[l], (1, 2, 0)).astype(BF16)

        qat, ckv, ckvt, ki, qit, wit, convn = _proj(
            h.reshape(b, s, d), row(mix_norm[l]), wcat, wt, wukt, row(kv_norm[l]),
            conv_w[l].astype(F32), row(conv_out_norm[l]))
        attn = _dsa(qit, wit, qat, ki, ckv, ckvt, bias, wuvt,
                    attn_out_norm[l].reshape(-1, 1).astype(F32), n_keep)

        last = l == depth - 1
        h = _out_ffn2(h, attn, convn.reshape(b * s, CONV_WIDTH),
                      w_out[l][:ATTN_WIDTH].astype(BF16), w_out[l][ATTN_WIDTH:].astype(BF16),
                      row(ffn2_norm[l]), wg2, wu2, wd2, row(final_norm), final_norm=last)
    return h.reshape(b, s, d).astype(x.dtype)
```

```python
import functools
import math

import jax
import jax.numpy as jnp
from jax import lax
from jax.experimental import pallas as pl
from jax.experimental.pallas import tpu as pltpu

F32 = jnp.float32
BF16 = jnp.bfloat16

EPS = 1e-6
N_HEADS = 8
HEAD_DIM = 64
ATTN_WIDTH = N_HEADS * HEAD_DIM
KV_LATENT = 128
N_HEADS_IDX = 8
IDX_DIM = 64
TOPK_MAX = 256
CONV_WIDTH = 512
CONV_K = 3
N_BUCKETS = 32
MAX_DISTANCE = 128
IN_SIZES = (ATTN_WIDTH, KV_LATENT, N_HEADS_IDX * IDX_DIM, IDX_DIM, N_HEADS_IDX,
            CONV_WIDTH, CONV_WIDTH, CONV_WIDTH)

SUBLANES = 8
LANES = 128

FFN_TOKENS = 1024
FFN_CHUNK = 256
PROJ_TOKENS = 1024
Q_BLOCK = 256
BISECT_CHUNK = 2
BISECT_FIRST = 8
MAX_BISECT = 128
CAST_STEPS = 8
DSA_BATCH = 4
SWEEP_ACCS = 2
MAX_PEEL = 4096
LOG2E = math.log2(math.e)
VMEM_LIMIT = 56 * 1024 * 1024


def _rms(x, g):
    return x * lax.rsqrt(jnp.mean(x * x, axis=-1, keepdims=True) + EPS) * g


def _const_spec(shape):
    nd = len(shape)
    return pl.BlockSpec(shape, lambda *_: (0,) * nd, pipeline_mode=pl.Buffered(1))


def _swiglu(xn, wg_ref, wu_ref, wd_ref):
    d_ff = wg_ref.shape[1]
    acc = None
    for c in range(d_ff // FFN_CHUNK):
        sl = slice(c * FFN_CHUNK, (c + 1) * FFN_CHUNK)
        g = jnp.dot(xn, wg_ref[:, sl], preferred_element_type=F32)
        u = jnp.dot(xn, wu_ref[:, sl], preferred_element_type=F32)
        a = (g * jax.nn.sigmoid(g) * u).astype(BF16)
        d = jnp.dot(a, wd_ref[sl, :], preferred_element_type=F32)
        acc = d if acc is None else acc + d
    return acc


def _ffn1_kernel(x_ref, g_ref, wg_ref, wu_ref, wd_ref, o_ref):
    x = x_ref[...]
    xn = _rms(x, g_ref[...]).astype(BF16)
    o_ref[...] = x + 0.5 * _swiglu(xn, wg_ref, wu_ref, wd_ref)


def _out_ffn2_kernel(h_ref, a_ref, c_ref, woa_ref, woc_ref, g_ref, wg_ref, wu_ref, wd_ref,
                     gf_ref, o_ref, *, final_norm):
    h = (h_ref[...]
         + jnp.dot(a_ref[...], woa_ref[...], preferred_element_type=F32)
         + jnp.dot(c_ref[...], woc_ref[...], preferred_element_type=F32))
    xn = _rms(h, g_ref[...]).astype(BF16)
    h = h + 0.5 * _swiglu(xn, wg_ref, wu_ref, wd_ref)
    o_ref[...] = _rms(h, gf_ref[...]) if final_norm else h


def _ffn1(x2, g, wg, wu, wd):
    n, d = x2.shape
    tile = pl.BlockSpec((FFN_TOKENS, d), lambda i: (i, 0))
    return pl.pallas_call(
        _ffn1_kernel,
        grid=(n // FFN_TOKENS,),
        in_specs=[tile, _const_spec(g.shape), _const_spec(wg.shape), _const_spec(wu.shape),
                  _const_spec(wd.shape)],
        out_specs=tile,
        out_shape=jax.ShapeDtypeStruct((n, d), F32),
        compiler_params=pltpu.CompilerParams(dimension_semantics=("arbitrary",),
                                             vmem_limit_bytes=VMEM_LIMIT),
        name="ffn1",
    )(x2, g, wg, wu, wd)


def _out_ffn2(h2, a2, c2, woa, woc, g, wg, wu, wd, gf, final_norm):
    n, d = h2.shape
    tile = pl.BlockSpec((FFN_TOKENS, d), lambda i: (i, 0))
    mix = pl.BlockSpec((FFN_TOKENS, a2.shape[1]), lambda i: (i, 0))
    return pl.pallas_call(
        functools.partial(_out_ffn2_kernel, final_norm=final_norm),
        grid=(n // FFN_TOKENS,),
        in_specs=[tile, mix, mix, _const_spec(woa.shape), _const_spec(woc.shape),
                  _const_spec(g.shape), _const_spec(wg.shape), _const_spec(wu.shape),
                  _const_spec(wd.shape), _const_spec(gf.shape)],
        out_specs=tile,
        out_shape=jax.ShapeDtypeStruct((n, d), F32),
        compiler_params=pltpu.CompilerParams(dimension_semantics=("arbitrary",),
                                             vmem_limit_bytes=VMEM_LIMIT),
        name="out_ffn2",
    )(h2, a2, c2, woa, woc, g, wg, wu, wd, gf)


_C_KV = 0
_C_KI = _C_KV + KV_LATENT
_C_GB = _C_KI + LANES
_C_GC = _C_GB + CONV_WIDTH
_C_X = _C_GC + CONV_WIDTH
_C_END = _C_X + CONV_WIDTH
_R_Q = 0
_R_KV = _R_Q + ATTN_WIDTH
_R_QI = _R_KV + KV_LATENT
_R_WI = _R_QI + N_HEADS_IDX * IDX_DIM
_R_END = _R_WI + 2 * SUBLANES
KVT_ROWS = KV_LATENT + 2 * SUBLANES


def _proj_kernel(h_ref, gm_ref, wcat_ref, wt_ref, wukt_ref, gkv_ref, gkvc_ref, cw_ref, gco_ref,
                 qat_ref, ckv_ref, ckvt_ref, ki_ref, qit_ref, wit_ref, convn_ref, carry_ref):
    tp = h_ref.shape[0]

    @pl.when(pl.program_id(1) == 0)
    def _():
        carry_ref[...] = jnp.zeros_like(carry_ref)

    un = _rms(h_ref[...], gm_ref[...]).astype(BF16)
    p = jnp.dot(un, wcat_ref[...], preferred_element_type=F32)
    ckv_ref[...] = _rms(p[:, _C_KV:_C_KI], gkv_ref[...]).astype(BF16)
    ki_ref[...] = p[:, _C_KI:_C_KI + IDX_DIM].astype(BF16)

    v = p[:, _C_GC:_C_X] * p[:, _C_X:_C_END]
    row = lax.broadcasted_iota(jnp.int32, v.shape, 0)
    prev1 = jnp.broadcast_to(carry_ref[SUBLANES - 1:SUBLANES, :], v.shape)
    prev2 = jnp.broadcast_to(carry_ref[SUBLANES - 2:SUBLANES - 1, :], v.shape)
    v1 = jnp.where(row == 0, prev1, pltpu.roll(v, 1, 0))
    v2 = jnp.where(row == 0, prev2, jnp.where(row == 1, prev1, pltpu.roll(v, 2, 0)))
    carry_ref[...] = v[tp - SUBLANES:, :]
    y = cw_ref[0:1, :] * v2 + cw_ref[1:2, :] * v1 + cw_ref[2:3, :] * v
    conv = p[:, _C_GB:_C_GC] * y
    convn_ref[...] = _rms(conv, gco_ref[...]).astype(BF16)

    pt = lax.dot_general(wt_ref[...], un, (((1,), (1,)), ((), ())),
                         preferred_element_type=F32)
    qt = pt[_R_Q:_R_KV, :].astype(BF16)
    for h in range(N_HEADS):
        qah = jnp.dot(wukt_ref[h], qt[h * HEAD_DIM:(h + 1) * HEAD_DIM, :],
                      preferred_element_type=F32)
        qat_ref[h * KV_LATENT:(h + 1) * KV_LATENT, :] = (qah * (HEAD_DIM ** -0.5 * LOG2E)).astype(BF16)
    kvt = pt[_R_KV:_R_QI, :]
    kvt = kvt * lax.rsqrt(jnp.mean(kvt * kvt, axis=0, keepdims=True) + EPS) * gkvc_ref[...]
    tk = ckvt_ref.shape[2]
    tail = jnp.concatenate([jnp.ones((SUBLANES, tk), F32), jnp.zeros((SUBLANES, tk), F32)], axis=0)
    for kb in range(tp // tk):
        blk = jnp.concatenate([kvt[:, kb * tk:(kb + 1) * tk], tail], axis=0)
        ckvt_ref[kb] = blk.astype(BF16)
    qit_ref[...] = pt[_R_QI:_R_WI, :].astype(BF16)
    wit_ref[...] = pt[_R_WI:_R_WI + N_HEADS_IDX, :] * ((N_HEADS_IDX * IDX_DIM) ** -0.5)


def _proj(h1, gm, wcat, wt, wukt, gkv, cw, gco):
    b, s, d = h1.shape
    tp = PROJ_TOKENS
    nq = N_HEADS_IDX * IDX_DIM
    nqa = N_HEADS * KV_LATENT
    gkvc = gkv.reshape(KV_LATENT, 1)
    tok = lambda w: pl.BlockSpec((None, tp, w), lambda bi, ti: (bi, ti, 0))
    tokt = lambda r: pl.BlockSpec((None, r, tp), lambda bi, ti: (bi, 0, ti))
    out_shape = (
        jax.ShapeDtypeStruct((b, nqa, s), BF16),
        jax.ShapeDtypeStruct((b, s, KV_LATENT), BF16),
        jax.ShapeDtypeStruct((b, s // Q_BLOCK, KVT_ROWS, Q_BLOCK), BF16),
        jax.ShapeDtypeStruct((b, s, IDX_DIM), BF16),
        jax.ShapeDtypeStruct((b, nq, s), BF16),
        jax.ShapeDtypeStruct((b, N_HEADS_IDX, s), F32),
        jax.ShapeDtypeStruct((b, s, CONV_WIDTH), BF16),
    )
    out_specs = (
        tokt(nqa), tok(KV_LATENT),
        pl.BlockSpec((None, tp // Q_BLOCK, KVT_ROWS, Q_BLOCK), lambda bi, ti: (bi, ti, 0, 0)),
        tok(IDX_DIM), tokt(nq), tokt(N_HEADS_IDX), tok(CONV_WIDTH),
    )
    return pl.pallas_call(
        _proj_kernel,
        grid=(b, s // tp),
        in_specs=[tok(d), _const_spec(gm.shape), _const_spec(wcat.shape), _const_spec(wt.shape),
                  _const_spec(wukt.shape), _const_spec(gkv.shape), _const_spec(gkvc.shape),
                  _const_spec(cw.shape), _const_spec(gco.shape)],
        out_specs=out_specs,
        out_shape=out_shape,
        scratch_shapes=[pltpu.VMEM((SUBLANES, CONV_WIDTH), F32)],
        compiler_params=pltpu.CompilerParams(dimension_semantics=("arbitrary", "arbitrary"),
                                             vmem_limit_bytes=VMEM_LIMIT),
        name="proj",
    )(h1, gm, wcat, wt, wukt, gkv, gkvc, cw, gco)


def _bucket_starts():
    e = N_BUCKETS // 2
    n = N_BUCKETS - e
    assert MAX_DISTANCE % e == 0
    r = MAX_DISTANCE // e
    starts, d = [], e
    for k in range(1, n):
        while d ** n < e ** n * r ** k:
            d += 1
        starts.append(d)
    return starts


def _rel_bucket(dist):
    max_exact = N_BUCKETS // 2
    large = max_exact
    for start in _bucket_starts():
        large = large + (dist >= start).astype(jnp.int32)
    return jnp.where(dist < max_exact, dist, large)


def _bias_kernel(rb_ref, o_ref):
    h = pl.program_id(0)
    tk, tq = o_ref.shape
    s = lax.broadcasted_iota(jnp.int32, (tk, tq), 0)
    t = lax.broadcasted_iota(jnp.int32, (tk, tq), 1)
    bucket = _rel_bucket(jnp.maximum(t - s + tq, 0))
    val = jnp.zeros((tk, tq), F32)
    for b in range(N_BUCKETS):
        val = jnp.where(bucket == b, rb_ref[b, h], val)
    o_ref[...] = (val - rb_ref[N_BUCKETS - 1, h]) * LOG2E


def _bias_tiles(rel_bias, tq):
    return pl.pallas_call(
        _bias_kernel,
        grid=(N_HEADS,),
        in_specs=[pl.BlockSpec(memory_space=pltpu.SMEM)],
        out_specs=pl.BlockSpec((None, 2 * tq, tq), lambda h: (h, 0, 0)),
        out_shape=jax.ShapeDtypeStruct((N_HEADS, 2 * tq, tq), F32),
        compiler_params=pltpu.CompilerParams(dimension_semantics=("arbitrary",)),
        name="bias_tiles",
    )(rel_bias)


def _dsa_kernel(qit_ref, wit_ref, qat_ref, ki_ref, ckv_ref, ckvt_ref, bias_ref, wuvt_ref, gout_ref,
                o_ref, sc_ref, mask_ref, acc_ref, m_ref, *, n_keep):
    nbat, tq = o_ref.shape[0], o_ref.shape[1]
    tw = nbat * tq
    i = pl.program_id(1)
    nblk = i + 1
    q0 = i * tq
    kf = float(n_keep)
    key_in_blk = lax.broadcasted_iota(jnp.int32, (tq, tq), 0)
    qry_in_blk = lax.broadcasted_iota(jnp.int32, (tq, tq), 1)
    ind = lambda mask: jnp.where(mask, 1.0, 0.0)
    batches = lambda f: jnp.concatenate([f(g) for g in range(nbat)], axis=1)
    part = lambda a: a.reshape(tq // (SWEEP_ACCS * SUBLANES), SWEEP_ACCS, SUBLANES, tw)
    fold = lambda a: part(a).sum(axis=0)
    fold_max = lambda a: part(a).max(axis=0)
    total = lambda a: a.sum(axis=0).sum(axis=0, keepdims=True)
    total_max = lambda a: a.max(axis=0).max(axis=0, keepdims=True)
    zeros8 = jnp.zeros((SWEEP_ACCS, SUBLANES, tw), F32)
    ninf8 = jnp.full((SWEEP_ACCS, SUBLANES, tw), -jnp.inf, F32)

    def score_body(j, carry):
        mn, mx = carry
        k0 = pl.multiple_of(j * tq, tq)
        causal1 = (k0 + key_in_blk) <= (q0 + qry_in_blk)

        def one(g):
            kib = ki_ref[g, pl.ds(k0, tq), :]
            acc = jnp.zeros((tq, tq), F32)
            for h in range(N_HEADS_IDX):
                d = jnp.dot(kib, qit_ref[g, h * IDX_DIM:(h + 1) * IDX_DIM, :],
                            preferred_element_type=F32)
                acc = acc + jnp.maximum(d, 0.0) * wit_ref[g, h:h + 1, :]
            return acc
        acc = batches(one)
        causal = batches(lambda g: causal1)
        masked = jnp.where(causal, acc, -jnp.inf)
        sc_ref[j] = masked
        mx = jnp.maximum(mx, jnp.max(masked, axis=0, keepdims=True))
        mn = jnp.minimum(mn, jnp.min(jnp.where(causal, acc, jnp.inf), axis=0, keepdims=True))
        return mn, mx

    def over_blocks(body, init):
        def pair(pi, c):
            return body(2 * pi + 1, body(2 * pi, c))
        c = lax.fori_loop(0, lax.shift_right_logical(nblk, 1), pair, init)
        return lax.cond((nblk & 1) == 1, lambda c: body(nblk - 1, c), lambda c: c, c)

    mn, mx = over_blocks(score_body,
                         (jnp.full((1, tw), jnp.inf, F32), jnp.full((1, tw), -jnp.inf, F32)))

    def count(bound, strict=False):
        def one(j, g):
            cols = slice(g * tq, (g + 1) * tq)
            blk = sc_ref[j, :, cols]
            hit = blk > bound[:, cols] if strict else blk >= bound[:, cols]
            return ind(hit).reshape(tq // (SWEEP_ACCS * SUBLANES), SWEEP_ACCS, SUBLANES, tq).sum(axis=0)
        a8 = lax.fori_loop(
            0, nblk,
            lambda j, a: a + jnp.concatenate([one(j, g) for g in range(nbat)], axis=-1), zeros8)
        return total(a8)

    def any_set(flag):
        return jnp.max(flag) > 0.0

    c_ge0 = count(jnp.zeros((1, tw), F32))
    c_gt0 = count(jnp.zeros((1, tw), F32), strict=True)
    ncausal = batches(lambda g: (q0 + lax.broadcasted_iota(jnp.int32, (1, tq), 1) + 1).astype(F32))
    small = ncausal <= kf
    neg = c_ge0 < kf
    zero_tie = (c_gt0 < kf) & ~neg
    lo = jnp.where(small | neg, mn, 0.0)
    clo = jnp.where(small | neg, ncausal, c_ge0)
    above_mx = mx + jnp.abs(mx) * 2.0 ** -22 + 1e-37
    hi = jnp.where(neg, 0.0, above_mx)
    searching = ~small & ~zero_tie

    def bis_cond(st):
        return (st[0] < MAX_BISECT) & st[1]

    def bis_passes(_, st):
        lo, hi, clo, active = st
        for _ in range(BISECT_CHUNK):
            mid = lo + (hi - lo) * 0.5
            live = (active > 0.0) & (mid > lo) & (mid < hi)
            c = count(mid)
            up = live & (c >= kf)
            lo = jnp.where(up, mid, lo)
            clo = jnp.where(up, c, clo)
            hi = jnp.where(live & (c < kf), mid, hi)
            active = ind(live & (clo > kf))
        return lo, hi, clo, active

    def bis_body(st):
        lo, hi, clo, active = bis_passes(0, st[2:])
        return st[0] + 1, any_set(active), lo, hi, clo, active

    active = ind(searching & (clo > kf))
    st = lax.cond(any_set(active),
                  lambda st: lax.fori_loop(0, BISECT_FIRST, bis_passes, st), lambda st: st,
                  (lo, hi, clo, active))
    st = lax.while_loop(bis_cond, bis_body, (jnp.int32(0), any_set(st[3])) + st)
    _, _, lo, hi, clo, _ = st

    unres = ind(searching & (clo > kf))

    def peel_cond(st):
        return (st[0] < MAX_PEEL) & st[1]

    def peel_body(st):
        it, _, lo, clo, cand, unres = st

        def body(j, s2):
            a8, m8 = s2
            blk = sc_ref[j]
            ge = blk >= cand
            below = jnp.where((blk >= lo) & ~ge, blk, -jnp.inf)
            return a8 + fold(ind(ge)), jnp.maximum(m8, fold_max(below))
        a8, m8 = lax.fori_loop(0, nblk, body, (zeros8, ninf8))
        c = total(a8)
        on = unres > 0.0
        found = on & (c >= kf)
        down = on & (c < kf)
        lo = jnp.where(found, cand, lo)
        clo = jnp.where(found, c, clo)
        cand = jnp.where(down, total_max(m8), cand)
        return it + 1, any_set(ind(down)), lo, clo, cand, ind(down)

    st = lax.while_loop(peel_cond, peel_body, (jnp.int32(0), any_set(unres), lo, clo, hi, unres))
    _, _, lo, clo, _, _ = st

    tied = any_set(ind(~small & (clo > kf)))

    @pl.when(jnp.logical_not(tied))
    def _():
        def body(j, c):
            mask_ref[j] = jnp.where(sc_ref[j] >= lo, 0.0, -jnp.inf)
            return c
        lax.fori_loop(0, nblk, body, 0)

    @pl.when(tied)
    def _():
        need = kf - count(lo, strict=True)
        earlier = ind(qry_in_blk < key_in_blk).astype(BF16)

        def body(j, seen):
            blk = sc_ref[j]
            eq = ind(blk == lo)
            rank = seen + jnp.dot(earlier, eq.astype(BF16), preferred_element_type=F32)
            keep = (blk > lo) | ((blk == lo) & (rank < need))
            mask_ref[j] = jnp.where(keep, 0.0, -jnp.inf)
            return seen + total(fold(eq))
        over_blocks(body, jnp.zeros((1, tw), F32))

    m_ref[...] = jnp.full(m_ref.shape, -jnp.inf, F32)
    acc_ref[...] = jnp.zeros(acc_ref.shape, F32)

    def attend(j, nkb, bias):
        tk = nkb * tq
        k0 = pl.multiple_of(j * tq, tq)
        heads = lambda f: jnp.concatenate([f(h) for h in range(N_HEADS)], axis=1)
        for g in range(nbat):
            cols = slice(g * tq, (g + 1) * tq)
            kv = ckv_ref[g, pl.ds(k0, tk), :]
            kvt1 = jnp.concatenate([ckvt_ref[g, j + b] for b in range(nkb)], axis=1)
            mask = jnp.concatenate([mask_ref[j + b, :, cols] for b in range(nkb)], axis=0)
            qa = heads(lambda h: qat_ref[g, h * KV_LATENT:(h + 1) * KV_LATENT, :])
            s = jnp.dot(kv, qa, preferred_element_type=F32)
            if bias is not None:
                s = s + heads(bias)
            s = s + heads(lambda h: mask)
            m_prev = m_ref[g]
            m_new = jnp.maximum(m_prev, jnp.max(s, axis=0, keepdims=True))
            m_safe = jnp.where(m_new == -jnp.inf, 0.0, m_new)
            alpha = jnp.exp2(m_prev - m_safe)
            p = jnp.exp2(s - m_safe).astype(BF16)
            m_ref[g] = m_new
            pv = jnp.dot(kvt1, p, preferred_element_type=F32)
            acc_ref[g] = alpha * acc_ref[g] + pv

    nfar = jnp.maximum(i - 1, 0)

    def far_pair(pi, c):
        attend(2 * pi, 2, None)
        return c
    lax.fori_loop(0, lax.shift_right_logical(nfar, 1), far_pair, 0)

    @pl.when((nfar & 1) == 1)
    def _():
        attend(nfar - 1, 1, None)

    @pl.when(i >= 1)
    def _():
        attend(i - 1, 2, lambda h: bias_ref[h])

    @pl.when(i == 0)
    def _():
        attend(0, 1, lambda h: bias_ref[h, tq:, :])

    for g in range(nbat):
        outs = []
        for h in range(N_HEADS):
            a = acc_ref[g, :, h * tq:(h + 1) * tq]
            ctx = (a[:KV_LATENT, :] / a[KV_LATENT:KV_LATENT + 1, :]).astype(BF16)
            outs.append(jnp.dot(wuvt_ref[h], ctx, preferred_element_type=F32))
        out = jnp.concatenate(outs, axis=0).T
        o_ref[g] = _rms(out, gout_ref[...]).astype(BF16)


def _dsa(qit, wit, qat, ki, ckv, ckvt, bias, wuvt, gout, n_keep):
    b, s, _ = ckv.shape
    tq = Q_BLOCK
    nb = s // tq
    nq = N_HEADS_IDX * IDX_DIM
    nbat = DSA_BATCH if b % DSA_BATCH == 0 else 1
    colblk = lambda r: pl.BlockSpec((nbat, r, tq), lambda bi, i: (bi, 0, i))
    return pl.pallas_call(
        functools.partial(_dsa_kernel, n_keep=n_keep),
        grid=(b // nbat, nb),
        in_specs=[
            colblk(nq), colblk(N_HEADS_IDX), colblk(N_HEADS * KV_LATENT),
            pl.BlockSpec((nbat, s, IDX_DIM), lambda bi, i: (bi, 0, 0)),
            pl.BlockSpec((nbat, s, KV_LATENT), lambda bi, i: (bi, 0, 0)),
            pl.BlockSpec((nbat, nb, KVT_ROWS, tq), lambda bi, i: (bi, 0, 0, 0)),
            _const_spec(bias.shape), _const_spec(wuvt.shape), _const_spec(gout.shape),
        ],
        out_specs=pl.BlockSpec((nbat, tq, ATTN_WIDTH), lambda bi, i: (bi, i, 0)),
        out_shape=jax.ShapeDtypeStruct((b, s, ATTN_WIDTH), BF16),
        scratch_shapes=[
            pltpu.VMEM((nb, tq, nbat * tq), F32),
            pltpu.VMEM((nb, tq, nbat * tq), F32),
            pltpu.VMEM((nbat, KVT_ROWS, N_HEADS * tq), F32),
            pltpu.VMEM((nbat, 1, N_HEADS * tq), F32),
        ],
        compiler_params=pltpu.CompilerParams(dimension_semantics=("arbitrary", "arbitrary"),
                                             vmem_limit_bytes=VMEM_LIMIT),
        name="dsa",
    )(qit, wit, qat, ki, ckv, ckvt, bias, wuvt, gout)


def _cast_kernel(*refs):
    n = len(refs) // 2
    for src, dst in zip(refs[:n], refs[n:]):
        dst[...] = src[...].astype(BF16)


def _to_bf16(*ws):
    r, c = ws[0].shape
    spec = pl.BlockSpec((r // CAST_STEPS, c), lambda i: (i, 0))
    return pl.pallas_call(
        _cast_kernel,
        grid=(CAST_STEPS,),
        in_specs=[spec] * len(ws),
        out_specs=[spec] * len(ws),
        out_shape=[jax.ShapeDtypeStruct(w.shape, BF16) for w in ws],
        compiler_params=pltpu.CompilerParams(dimension_semantics=("arbitrary",),
                                             vmem_limit_bytes=VMEM_LIMIT),
        name="to_bf16",
    )(*ws)


def kernel(x, ffn1_norm, ffn1_w_gate, ffn1_w_up, ffn1_w_down, mix_norm, w_in, kv_norm, w_uk, w_uv, rel_bias, conv_w, attn_out_norm, conv_out_norm, w_out, ffn2_norm, ffn2_w_gate, ffn2_w_up, ffn2_w_down, final_norm):
    b, s, d = x.shape
    depth = ffn1_norm.shape[0]
    n_keep = min(TOPK_MAX, s // 4)
    assert s % Q_BLOCK == 0 and s % PROJ_TOKENS == 0 and (b * s) % FFN_TOKENS == 0
    assert Q_BLOCK >= MAX_DISTANCE and n_keep <= Q_BLOCK
    row = lambda v: v.reshape(1, -1).astype(F32)
    offs = [0]
    for w in IN_SIZES:
        offs.append(offs[-1] + w)

    bias = _bias_tiles(rel_bias.astype(F32), Q_BLOCK)
    h = x.astype(F32).reshape(b * s, d)
    for l in range(depth):
        wg1, wu1, wg2, wu2 = _to_bf16(ffn1_w_gate[l], ffn1_w_up[l], ffn2_w_gate[l], ffn2_w_up[l])
        wd1, wd2 = _to_bf16(ffn1_w_down[l], ffn2_w_down[l])
        h = _ffn1(h, row(ffn1_norm[l]), wg1, wu1, wd1)

        wi = w_in[l]
        col = lambda k: wi[:, offs[k]:offs[k + 1]]
        wcat = jnp.concatenate(
            [col(1), col(3), jnp.zeros((d, LANES - IDX_DIM), wi.dtype), col(5), col(6), col(7)],
            axis=1).astype(BF16)
        wt = jnp.concatenate(
            [col(0).T, col(1).T, col(2).T, col(4).T,
             jnp.zeros((_R_END - _R_WI - N_HEADS_IDX, d), wi.dtype)], axis=0).astype(BF16)
        wukt = jnp.transpose(w_uk[l], (1, 0, 2)).astype(BF16)
        wuvt = jnp.transpose(w_uv[l], (1, 2, 0)).astype(BF16)

        qat, ckv, ckvt, ki, qit, wit, convn = _proj(
            h.reshape(b, s, d), row(mix_norm[l]), wcat, wt, wukt, row(kv_norm[l]),
            conv_w[l].astype(F32), row(conv_out_norm[l]))
        attn = _dsa(qit, wit, qat, ki, ckv, ckvt, bias, wuvt, row(attn_out_norm[l]), n_keep)

        last = l == depth - 1
        h = _out_ffn2(h, attn.reshape(b * s, ATTN_WIDTH), convn.reshape(b * s, CONV_WIDTH),
                      w_out[l][:ATTN_WIDTH].astype(BF16), w_out[l][ATTN_WIDTH:].astype(BF16),
                      row(ffn2_norm[l]), wg2, wu2, wd2, row(final_norm), final_norm=last)
    return h.reshape(b, s, d).astype(x.dtype)
```
